```python
import math
import jax, jax.numpy as jnp
from jax import lax
import numpy as np

D_MODEL = 2048
BATCH = 4
SEQ = 2048
DEPTH = 1
DEC_BATCH = 128
DEC_SEQ = 1
PAST_LEN = 16384
PAGE_SIZE = 128

RET_HEADS = 8
RET_DK = 128
RET_DV = 256
RET_CHUNK = 128
ROPE_THETA = 10000.0
GM_GROUPS = 8
GM_WIDTH = 2048
GM_CHUNK = 128
N_EXPERTS = 32
TOP_K = 4
D_EXPERT = 2048
SWIGLU_LIMIT = 7.0
SWIGLU_ALPHA = 1.702
MOE_BLOCK = 128
EPS = 1e-6

RET_QK = RET_HEADS * RET_DK
RET_V = RET_HEADS * RET_DV
SPLITS = (RET_QK, RET_QK, RET_V, RET_V, GM_WIDTH, GM_WIDTH, D_MODEL, D_MODEL)
D_IN = sum(SPLITS)

kernel_name = "hybrid_retention_gmlp_moe_step"

f32 = jnp.float32


def rmsnorm(x, g):
    xf = x.astype(f32)
    y = xf * lax.rsqrt(jnp.mean(xf * xf, axis=-1, keepdims=True) + EPS)
    return (y * g.astype(f32)).astype(x.dtype)


def rope(x, pos):
    half = x.shape[-1] // 2
    freq = ROPE_THETA ** (-jnp.arange(half, dtype=f32) / half)
    ang = pos.astype(f32)[:, None] * freq[None, :]
    cos, sin = jnp.cos(ang), jnp.sin(ang)
    x1, x2 = x[..., :half], x[..., half:]
    return jnp.concatenate([x1 * cos - x2 * sin, x1 * sin + x2 * cos], axis=-1)


def retention(q, k, v, s0):
    B, H, L, _ = q.shape
    C = min(RET_CHUNK, L)
    n = L // C
    lg = jnp.log1p(-jnp.exp2(-5.0 - jnp.arange(H, dtype=f32)))
    idx = jnp.arange(C, dtype=f32)
    diff = idx[:, None] - idx[None, :]
    intra = jnp.where(diff >= 0, jnp.exp(jnp.maximum(diff, 0.0)[None] * lg[:, None, None]), 0.0)
    q_dec = jnp.exp((idx + 1.0)[None, :] * lg[:, None])[None, :, :, None]
    k_dec = jnp.exp((C - 1.0 - idx)[None, :] * lg[:, None])[None, :, :, None]
    s_dec = jnp.exp(C * lg)[None, :, None, None]

    def chunk(s, qkv):
        qc, kc, vc = qkv
        scores = jnp.einsum('bhid,bhjd->bhij', qc, kc) * intra
        o = jnp.einsum('bhij,bhjv->bhiv', scores, vc) + jnp.einsum('bhid,bhdv->bhiv', qc * q_dec, s)
        s = s * s_dec + jnp.einsum('bhjd,bhjv->bhdv', kc * k_dec, vc)
        return s, o

    to_chunks = lambda t: t.reshape(B, H, n, C, t.shape[-1]).transpose(2, 0, 1, 3, 4)
    s, o = lax.scan(chunk, s0, (to_chunks(q), to_chunks(k), to_chunks(v)))
    o = o.transpose(1, 2, 0, 3, 4).reshape(B, H, L, v.shape[-1])
    return o, s


def mixer(h, pos, s0, w_in, ret_gn_w, gm_ln_w, gm_ln_b, gm_ws, gm_bs, w_oa, w_ob, w_out):
    B, L, _ = h.shape
    z = h @ w_in
    q, k, v, g, u, gv, ga, gb = jnp.split(z, np.cumsum(SPLITS)[:-1].tolist(), axis=-1)
    heads = lambda t, d: t.reshape(B, L, RET_HEADS, d).transpose(0, 2, 1, 3).astype(f32)
    qh = rope(heads(q, RET_DK), pos) * (RET_DK ** -0.5)
    kh = rope(heads(k, RET_DK), pos)
    vh = heads(v, RET_DV)
    o, s_new = retention(qh, kh, vh, s0.astype(f32))
    mu = jnp.mean(o, axis=-1, keepdims=True)
    var = jnp.mean(jnp.square(o - mu), axis=-1, keepdims=True)
    o = ((o - mu) * lax.rsqrt(var + EPS)).transpose(0, 2, 1, 3).reshape(B, L, RET_V) * ret_gn_w.astype(f32)
    ya = (jax.nn.silu(g) * o.astype(h.dtype)) @ w_oa
    u = jax.nn.gelu(u)
    gvf = jax.nn.gelu(gv).astype(f32)
    mu = jnp.mean(gvf, axis=-1, keepdims=True)
    var = jnp.mean(jnp.square(gvf - mu), axis=-1, keepdims=True)
    vn = ((gvf - mu) * lax.rsqrt(var + EPS)).astype(h.dtype) * gm_ln_w + gm_ln_b
    C = min(GM_CHUNK, L)
    ws = jnp.tril(gm_ws[:, :C, :C])
    bs = gm_bs[:, :C]
    vc = vn.reshape(B, L // C, C, GM_GROUPS, GM_WIDTH // GM_GROUPS)
    mixed = jnp.einsum('gts,bnsgc->bntgc', ws, vc) + bs.T[None, None, :, :, None]
    yb = (u * mixed.reshape(B, L, GM_WIDTH).astype(h.dtype)) @ w_ob
    y = (jax.nn.sigmoid(ga) * ya + jax.nn.sigmoid(gb) * yb) @ w_out
    return y, s_new, vn


def moe(h, w_router, b_router, w_gate_up, b_gate_up, w_down, b_down):
    T = h.shape[0]
    TK = T * TOP_K
    logits = (h @ w_router + b_router).astype(f32)
    top_val, top_idx = lax.top_k(logits, TOP_K)
    gates = jax.nn.softmax(top_val, axis=-1)
    flat_e = top_idx.reshape(-1)
    order = jnp.argsort(flat_e, stable=True)
    sorted_e = flat_e[order]
    tok = order // TOP_K
    counts = jnp.bincount(flat_e, length=N_EXPERTS)
    padded = ((counts + MOE_BLOCK - 1) // MOE_BLOCK) * MOE_BLOCK
    pad_end = jnp.cumsum(padded)
    pad_start = pad_end - padded
    start = jnp.cumsum(counts) - counts
    dest = pad_start[sorted_e] + (jnp.arange(TK) - start[sorted_e])
    n_blocks = -(-(TK + N_EXPERTS * (MOE_BLOCK - 1)) // MOE_BLOCK)
    n_rows = n_blocks * MOE_BLOCK
    xbuf = jnp.zeros((n_rows, h.shape[1]), h.dtype).at[dest].set(h[tok])
    block_e = jnp.minimum(jnp.searchsorted(pad_end, jnp.arange(n_blocks) * MOE_BLOCK, side='right'), N_EXPERTS - 1)

    def expert_block(args):
        xb, e = args
        gu = xb @ w_gate_up[e] + b_gate_up[e]
        gate, up = gu[:, :D_EXPERT], gu[:, D_EXPERT:]
        gate = jnp.minimum(gate, SWIGLU_LIMIT)
        up = jnp.clip(up, -SWIGLU_LIMIT, SWIGLU_LIMIT)
        act = (up + 1.0) * (gate * jax.nn.sigmoid(gate * SWIGLU_ALPHA))
        return act @ w_down[e] + b_down[e]

    ybuf = lax.map(expert_block, (xbuf.reshape(n_blocks, MOE_BLOCK, -1), block_e)).reshape(n_rows, -1)
    w_sorted = gates.reshape(-1)[order]
    out = jax.ops.segment_sum(ybuf[dest].astype(f32) * w_sorted[:, None], tok, num_segments=T)
    return out.astype(h.dtype)


def layer(x, c, pos, s0, norm1_w, norm2_w, w_ada, b_ada, w_in, ret_gn_w, gm_ln_w, gm_ln_b, gm_ws, gm_bs,
          w_oa, w_ob, w_out, w_router, b_router, w_gate_up, b_gate_up, w_down, b_down):
    B, L, D = x.shape
    mod = (c @ w_ada + b_ada)[:, None, :]
    sh1, sc1, g1, sh2, sc2, g2 = jnp.split(mod, 6, axis=-1)
    h = rmsnorm(x, norm1_w) * (1.0 + sc1) + sh1
    y, s_new, vn = mixer(h, pos, s0, w_in, ret_gn_w, gm_ln_w, gm_ln_b, gm_ws, gm_bs, w_oa, w_ob, w_out)
    x = x + (g1 * y).astype(x.dtype)
    h = rmsnorm(x, norm2_w) * (1.0 + sc2) + sh2
    f = moe(h.reshape(B * L, D), w_router, b_router, w_gate_up, b_gate_up, w_down, b_down).reshape(B, L, D)
    x = x + (g2 * f).astype(x.dtype)
    return x, s_new, vn


def setup_inputs(seed: int = 0) -> dict:
    key = jax.random.key(seed)
    ks = jax.random.split(key, 32)
    nrm = lambda k, shape, s: jax.random.normal(k, shape, f32) * s
    Dl = DEPTH
    return {
        "x_prompt": nrm(ks[0], (BATCH, SEQ, D_MODEL), 1.0),
        "x_sample": nrm(ks[1], (DEC_BATCH, DEC_SEQ, D_MODEL), 1.0),
        "c_prompt": nrm(ks[2], (BATCH, D_MODEL), 1.0),
        "c_sample": nrm(ks[3], (DEC_BATCH, D_MODEL), 1.0),
        "state_ret": nrm(ks[4], (Dl, DEC_BATCH, RET_HEADS, RET_DK, RET_DV), 1.0),
        "norm1_w": 1.0 + nrm(ks[5], (Dl, D_MODEL), 0.02),
        "norm2_w": 1.0 + nrm(ks[6], (Dl, D_MODEL), 0.02),
        "w_ada": nrm(ks[7], (Dl, D_MODEL, 6 * D_MODEL), 0.2 * D_MODEL ** -0.5),
        "b_ada": nrm(ks[8], (Dl, 6 * D_MODEL), 0.01),
        "w_in": nrm(ks[9], (Dl, D_MODEL, D_IN), D_MODEL ** -0.5),
        "ret_gn_w": 1.0 + nrm(ks[10], (Dl, RET_V), 0.02),
        "gm_ln_w": 1.0 + nrm(ks[11], (Dl, GM_WIDTH), 0.02),
        "gm_ln_b": nrm(ks[12], (Dl, GM_WIDTH), 0.01),
        "gm_ws": nrm(ks[13], (Dl, GM_GROUPS, GM_CHUNK, GM_CHUNK), GM_CHUNK ** -0.5),
        "gm_bs": 1.0 + nrm(ks[14], (Dl, GM_GROUPS, GM_CHUNK), 0.01),
        "w_oa": nrm(ks[15], (Dl, RET_V, D_MODEL), RET_V ** -0.5),
        "w_ob": nrm(ks[16], (Dl, GM_WIDTH, D_MODEL), GM_WIDTH ** -0.5),
        "w_out": nrm(ks[17], (Dl, D_MODEL, D_MODEL), D_MODEL ** -0.5),
        "w_router": nrm(ks[18], (Dl, D_MODEL, N_EXPERTS), D_MODEL ** -0.5),
        "b_router": nrm(ks[19], (Dl, N_EXPERTS), 0.01),
        "w_gate_up": nrm(ks[20], (Dl, N_EXPERTS, D_MODEL, 2 * D_EXPERT), D_MODEL ** -0.5),
        "b_gate_up": nrm(ks[21], (Dl, N_EXPERTS, 2 * D_EXPERT), 0.01),
        "w_down": nrm(ks[22], (Dl, N_EXPERTS, D_EXPERT, D_MODEL), D_EXPERT ** -0.5),
        "b_down": nrm(ks[23], (Dl, N_EXPERTS, D_MODEL), 0.01),
        "final_norm_w": 1.0 + nrm(ks[24], (D_MODEL,), 0.02),
    }


def reference(x_prompt, x_sample, c_prompt, c_sample, state_ret, norm1_w, norm2_w, w_ada, b_ada, w_in,
              ret_gn_w, gm_ln_w, gm_ln_b, gm_ws, gm_bs, w_oa, w_ob, w_out, w_router, b_router,
              w_gate_up, b_gate_up, w_down, b_down, final_norm_w):
    xp, xs = x_prompt, x_sample
    pos_p = jnp.arange(x_prompt.shape[1])
    pos_s = PAST_LEN + jnp.arange(x_sample.shape[1])
    s0_prompt = jnp.zeros((x_prompt.shape[0], RET_HEADS, RET_DK, RET_DV), f32)
    sp_list, ss_list, v_list = [], [], []
    for l in range(DEPTH):
        p = (norm1_w[l], norm2_w[l], w_ada[l], b_ada[l], w_in[l], ret_gn_w[l], gm_ln_w[l], gm_ln_b[l],
             gm_ws[l], gm_bs[l], w_oa[l], w_ob[l], w_out[l], w_router[l], b_router[l],
             w_gate_up[l], b_gate_up[l], w_down[l], b_down[l])
        xp, sp, _ = layer(xp, c_prompt, pos_p, s0_prompt, *p)
        xs, ss, vs = layer(xs, c_sample, pos_s, state_ret[l], *p)
        sp_list.append(sp)
        ss_list.append(ss)
        v_list.append(vs)
    y_prompt = rmsnorm(xp, final_norm_w)
    y_sample = rmsnorm(xs, final_norm_w)
    return (y_prompt, y_sample, jnp.stack(sp_list), jnp.stack(ss_list), jnp.stack(v_list))
```

```python
import functools

import numpy as np
import jax
import jax.numpy as jnp
from jax import lax
from jax.experimental import pallas as pl
from jax.experimental.pallas import tpu as pltpu

f32 = jnp.float32
bf16 = jnp.bfloat16
i32 = jnp.int32

D = 2048
BATCH = 4
SEQ = 2048
DEC_BATCH = 128
PAST_LEN = 16384
HEADS = 8
DK = 128
DV = 256
CHUNK = 128
ROPE_THETA = 10000.0
GROUPS = 8
GW = 2048
GC = GW // GROUPS
N_EXPERTS = 32
TOP_K = 4
D_EXPERT = 2048
SWIGLU_LIMIT = 7.0
SWIGLU_ALPHA = 1.702
EPS = 1e-6

T_P = BATCH * SEQ
T_S = DEC_BATCH
T_ALL = T_P + T_S
D_IN = 14336
COL_Q, COL_K = 0, 1
COL_V, COL_G, COL_U, COL_GV, COL_GA, COL_GB = 1, 2, 3, 4, 5, 6
MOD_SH1, MOD_SC1, MOD_G1, MOD_SH2, MOD_SC2, MOD_G2 = range(6)

LANES = 128
SUBLANES = 8
VMEM_LIMIT = 48 * 1024 * 1024

MOE_BM = 128
MOE_RB = 12
MOE_TF = 256
MOE_J = D_EXPERT // MOE_TF
TK_ALL = T_ALL * TOP_K
MOE_NBLK = (TK_ALL + N_EXPERTS * (MOE_BM - 1)) // MOE_BM
MOE_NROWS = MOE_NBLK * MOE_BM
MOE_PMAX = N_EXPERTS + MOE_NBLK // MOE_RB


def _cparams(sem, vmem=VMEM_LIMIT):
    return pltpu.CompilerParams(dimension_semantics=sem, vmem_limit_bytes=vmem)


def _ada_kernel(c_ref, w_ref, b_ref, o_ref):
    o_ref[...] = jnp.dot(c_ref[...].astype(bf16), w_ref[...].astype(bf16),
                         preferred_element_type=f32) + b_ref[...]


def _ada(c_all, w_ada, b_ada):
    m = c_all.shape[0]
    tn = 1024
    return pl.pallas_call(
        _ada_kernel,
        grid=(6 * D // tn,),
        in_specs=[pl.BlockSpec((m, D), lambda j: (0, 0)),
                  pl.BlockSpec((D, tn), lambda j: (0, j)),
                  pl.BlockSpec((1, tn), lambda j: (0, j))],
        out_specs=pl.BlockSpec((m, tn), lambda j: (0, j)),
        out_shape=jax.ShapeDtypeStruct((m, 6 * D), f32),
        compiler_params=_cparams(("arbitrary",)),
        name="ada_mod",
    )(c_all, w_ada, b_ada.reshape(1, 6 * D))


def _norm_mod_kernel(x_ref, nw_ref, sc_ref, sh_ref, o_ref):
    x = x_ref[...]
    y = x * lax.rsqrt(jnp.mean(x * x, axis=-1, keepdims=True) + EPS) * nw_ref[...]
    o_ref[...] = (y * (1.0 + sc_ref[...]) + sh_ref[...]).astype(o_ref.dtype)


NORM1_TM = T_S
NORM1_PT = T_P // NORM1_TM


def _norm1_kernel(xp_ref, xs_ref, nw_ref, scp_ref, shp_ref, scs_ref, shs_ref, o_ref):
    i = pl.program_id(0)

    @pl.when(i < NORM1_PT)
    def _():
        _norm_mod_kernel(xp_ref, nw_ref, scp_ref, shp_ref, o_ref)

    @pl.when(i == NORM1_PT)
    def _():
        _norm_mod_kernel(xs_ref, nw_ref, scs_ref, shs_ref, o_ref)


def _norm1(x_p, x_s, nw, mod_p, mod_s):
    tm = NORM1_TM
    per_b = SEQ // tm
    pt = lambda i: jnp.minimum(i, NORM1_PT - 1)
    return pl.pallas_call(
        _norm1_kernel,
        grid=(NORM1_PT + 1,),
        in_specs=[pl.BlockSpec((tm, D), lambda i: (pt(i), 0)),
                  pl.BlockSpec((tm, D), lambda i: (0, 0)),
                  pl.BlockSpec((1, D), lambda i: (0, 0)),
                  pl.BlockSpec((None, 1, D), lambda i: (pt(i) // per_b, 0, MOD_SC1)),
                  pl.BlockSpec((None, 1, D), lambda i: (pt(i) // per_b, 0, MOD_SH1)),
                  pl.BlockSpec((tm, D), lambda i: (0, MOD_SC1)),
                  pl.BlockSpec((tm, D), lambda i: (0, MOD_SH1))],
        out_specs=pl.BlockSpec((tm, D), lambda i: (i, 0)),
        out_shape=jax.ShapeDtypeStruct((T_ALL, D), bf16),
        compiler_params=_cparams(("arbitrary",)),
        name="norm1",
    )(x_p, x_s, nw, mod_p, mod_p, mod_s, mod_s)


def _proj_in_kernel(h_ref, w_ref, o_ref, wb_ref):
    @pl.when(pl.program_id(1) == 0)
    def _():
        wb_ref[...] = w_ref[...].astype(bf16)

    o_ref[...] = jnp.dot(h_ref[...], wb_ref[...], preferred_element_type=f32)


def _proj_in(h_all, w_in):
    tm, tn = 640, 1024
    return pl.pallas_call(
        _proj_in_kernel,
        grid=(D_IN // tn, T_ALL // tm),
        in_specs=[pl.BlockSpec((tm, D), lambda j, i: (i, 0)),
                  pl.BlockSpec((D, tn), lambda j, i: (0, j))],
        out_specs=pl.BlockSpec((tm, tn), lambda j, i: (i, j)),
        out_shape=jax.ShapeDtypeStruct((T_ALL, D_IN), f32),
        scratch_shapes=[pltpu.VMEM((D, tn), bf16)],
        compiler_params=_cparams(("arbitrary", "arbitrary")),
        name="proj_in",
    )(h_all, w_in)


def _retention_consts():
    h = np.arange(HEADS, dtype=np.float64)
    lg = np.log1p(-np.exp2(-5.0 - h))
    idx = np.arange(CHUNK, dtype=np.float64)
    diff = idx[:, None] - idx[None, :]
    intra = np.where(diff >= 0, np.exp(np.maximum(diff, 0.0)[None] * lg[:, None, None]), 0.0)
    q_dec = np.exp((idx + 1.0)[:, None] * lg[None, :])
    k_dec = np.exp((CHUNK - 1.0 - idx)[:, None] * lg[None, :])
    s_dec = np.exp(CHUNK * lg)
    gamma = np.exp(lg)
    return (intra.astype(np.float32), q_dec.astype(np.float32), k_dec.astype(np.float32),
            [float(v) for v in s_dec], [float(v) for v in gamma])


def _rope_tables(pos):
    half = DK // 2
    freq = ROPE_THETA ** (-np.arange(half, dtype=np.float64) / half)
    ang = np.asarray(pos, dtype=np.float64)[:, None] * freq[None, :]
    cos, sin = np.cos(ang), np.sin(ang)
    cos_t = np.concatenate([cos, cos], axis=-1).astype(np.float32)
    sin_t = np.concatenate([-sin, sin], axis=-1).astype(np.float32)
    return cos_t, sin_t


def _group_norm_gate(o, gnw, g):
    mu = jnp.mean(o, axis=-1, keepdims=True)
    var = jnp.mean(jnp.square(o - mu), axis=-1, keepdims=True)
    on = (o - mu) * lax.rsqrt(var + EPS) * gnw
    return jax.nn.silu(g) * on


def _ret_prompt_kernel(s_dec, q_ref, k_ref, v_ref, g_ref, cos_ref, sin_ref, intra_ref, qd_ref, kd_ref,
                       gnw_ref, a_ref, s_ref):
    n = pl.program_id(1)

    @pl.when(n == 0)
    def _():
        s_ref[...] = jnp.zeros_like(s_ref)

    cos = cos_ref[...]
    sin = sin_ref[...]
    qd = qd_ref[...]
    kd = kd_ref[...]
    for h in range(HEADS):
        qh = q_ref[:, h * DK:(h + 1) * DK]
        kh = k_ref[:, h * DK:(h + 1) * DK]
        qh = (qh * cos + pltpu.roll(qh, DK // 2, 1) * sin) * (DK ** -0.5)
        kh = kh * cos + pltpu.roll(kh, DK // 2, 1) * sin
        vh = v_ref[:, h * DV:(h + 1) * DV].astype(bf16)
        s_old = s_ref[h]
        scores = lax.dot_general(qh.astype(bf16), kh.astype(bf16), (((1,), (1,)), ((), ())),
                                 preferred_element_type=f32) * intra_ref[h]
        o = jnp.dot(scores.astype(bf16), vh, preferred_element_type=f32)
        o = o + jnp.dot((qh * qd[:, h:h + 1]).astype(bf16), s_old.astype(bf16), preferred_element_type=f32)
        kw_t = (kh * kd[:, h:h + 1]).T.astype(bf16)
        s_ref[h] = s_old * s_dec[h] + jnp.dot(kw_t, vh, preferred_element_type=f32)
        a_ref[:, h * DV:(h + 1) * DV] = _group_norm_gate(
            o, gnw_ref[:, h * DV:(h + 1) * DV], g_ref[:, h * DV:(h + 1) * DV]).astype(a_ref.dtype)


def _ret_prompt(z_all, gn_w):
    intra, q_dec, k_dec, s_dec, _ = _retention_consts()
    cos_t, sin_t = _rope_tables(np.arange(SEQ))
    nch = SEQ // CHUNK
    row = lambda b, n: b * nch + n
    return pl.pallas_call(
        functools.partial(_ret_prompt_kernel, s_dec),
        grid=(BATCH, nch),
        in_specs=[pl.BlockSpec((CHUNK, HEADS * DK), lambda b, n: (row(b, n), COL_Q)),
                  pl.BlockSpec((CHUNK, HEADS * DK), lambda b, n: (row(b, n), COL_K)),
                  pl.BlockSpec((CHUNK, HEADS * DV), lambda b, n: (row(b, n), COL_V)),
                  pl.BlockSpec((CHUNK, HEADS * DV), lambda b, n: (row(b, n), COL_G)),
                  pl.BlockSpec((CHUNK, DK), lambda b, n: (n, 0)),
                  pl.BlockSpec((CHUNK, DK), lambda b, n: (n, 0)),
                  pl.BlockSpec((HEADS, CHUNK, CHUNK), lambda b, n: (0, 0, 0)),
                  pl.BlockSpec((CHUNK, HEADS), lambda b, n: (0, 0)),
                  pl.BlockSpec((CHUNK, HEADS), lambda b, n: (0, 0)),
                  pl.BlockSpec((1, HEADS * DV), lambda b, n: (0, 0))],
        out_specs=[pl.BlockSpec((CHUNK, HEADS * DV), lambda b, n: (row(b, n), 0)),
                   pl.BlockSpec((None, HEADS, DK, DV), lambda b, n: (b, 0, 0, 0))],
        out_shape=[jax.ShapeDtypeStruct((T_P, HEADS * DV), bf16),
                   jax.ShapeDtypeStruct((BATCH, HEADS, DK, DV), f32)],
        compiler_params=_cparams(("arbitrary", "arbitrary")),
        name="retention_prompt",
    )(z_all, z_all, z_all, z_all, jnp.asarray(cos_t), jnp.asarray(sin_t), jnp.asarray(intra),
      jnp.asarray(q_dec), jnp.asarray(k_dec), gn_w.reshape(1, HEADS * DV))


RET_S_BT = 4


def _ret_sample_kernel(gamma, qt_ref, kt_ref, v_ref, g_ref, cos_ref, sin_ref, gnw_ref, s_in_ref,
                       a_ref, s_out_ref):
    cos = cos_ref[...]
    sin = sin_ref[...]

    def rope_t(x):
        rolled = jnp.concatenate([x[DK // 2:], x[:DK // 2]], axis=0)
        return x * cos + rolled * sin

    def body(t, carry):
        qt = rope_t(qt_ref[t]) * (DK ** -0.5)
        kt = rope_t(kt_ref[t])
        v8 = v_ref[t]
        g8 = g_ref[t]
        for h in range(HEADS):
            s_new = s_in_ref[t, h] * gamma[h] + kt[:, h:h + 1] * v8[h:h + 1, :]
            s_out_ref[t, h] = s_new
            o = jnp.sum(qt[:, h:h + 1] * s_new, axis=0, keepdims=True)
            a_ref[t, h:h + 1, :] = _group_norm_gate(o, gnw_ref[h:h + 1, :], g8[h:h + 1, :])
        return carry

    lax.fori_loop(0, RET_S_BT, body, 0)


def _ret_sample(qt, kt, v3, g3, gn_w, state):
    _, _, _, _, gamma = _retention_consts()
    cos_t, sin_t = _rope_tables(np.array([PAST_LEN]))
    cos8 = np.ascontiguousarray(np.broadcast_to(cos_t[0][:, None], (DK, HEADS)))
    sin8 = np.ascontiguousarray(np.broadcast_to(sin_t[0][:, None], (DK, HEADS)))
    bt = RET_S_BT
    return pl.pallas_call(
        functools.partial(_ret_sample_kernel, gamma),
        grid=(T_S // bt,),
        in_specs=[pl.BlockSpec((bt, DK, HEADS), lambda i: (i, 0, 0)),
                  pl.BlockSpec((bt, DK, HEADS), lambda i: (i, 0, 0)),
                  pl.BlockSpec((bt, HEADS, DV), lambda i: (i, 0, 0)),
                  pl.BlockSpec((bt, HEADS, DV), lambda i: (i, 0, 0)),
                  pl.BlockSpec((DK, HEADS), lambda i: (0, 0)),
                  pl.BlockSpec((DK, HEADS), lambda i: (0, 0)),
                  pl.BlockSpec((HEADS, DV), lambda i: (0, 0)),
                  pl.BlockSpec((bt, HEADS, DK, DV), lambda i: (i, 0, 0, 0))],
        out_specs=[pl.BlockSpec((bt, HEADS, DV), lambda i: (i, 0, 0)),
                   pl.BlockSpec((bt, HEADS, DK, DV), lambda i: (i, 0, 0, 0))],
        out_shape=[jax.ShapeDtypeStruct((T_S, HEADS, DV), f32),
                   jax.ShapeDtypeStruct((T_S, HEADS, DK, DV), f32)],
        compiler_params=_cparams(("arbitrary",)),
        name="retention_sample",
    )(qt, kt, v3, g3, jnp.asarray(cos8), jnp.asarray(sin8), gn_w.reshape(HEADS, DV), state)


def _layer_norm(x, w, b):
    mu = jnp.mean(x, axis=-1, keepdims=True)
    var = jnp.mean(jnp.square(x - mu), axis=-1, keepdims=True)
    return (x - mu) * lax.rsqrt(var + EPS) * w + b


def _gmlp_prompt_kernel(u_ref, gv_ref, ws_ref, bst_ref, lnw_ref, lnb_ref, o_ref):
    vn = _layer_norm(jax.nn.gelu(gv_ref[...]), lnw_ref[...], lnb_ref[...]).astype(bf16)
    r = lax.broadcasted_iota(i32, (CHUNK, CHUNK), 0)
    c = lax.broadcasted_iota(i32, (CHUNK, CHUNK), 1)
    causal = r >= c
    bst = bst_ref[...]
    for g in range(GROUPS):
        w = jnp.where(causal, ws_ref[g], 0.0).astype(bf16)
        mixed = jnp.dot(w, vn[:, g * GC:(g + 1) * GC], preferred_element_type=f32) + bst[:, g:g + 1]
        o_ref[:, g * GC:(g + 1) * GC] = (jax.nn.gelu(u_ref[:, g * GC:(g + 1) * GC]) * mixed).astype(o_ref.dtype)


def _gmlp_prompt(z_all, gm_ws, gm_bs, ln_w, ln_b):
    return pl.pallas_call(
        _gmlp_prompt_kernel,
        grid=(T_P // CHUNK,),
        in_specs=[pl.BlockSpec((CHUNK, GW), lambda i: (i, COL_U)),
                  pl.BlockSpec((CHUNK, GW), lambda i: (i, COL_GV)),
                  pl.BlockSpec((GROUPS, CHUNK, CHUNK), lambda i: (0, 0, 0)),
                  pl.BlockSpec((CHUNK, GROUPS), lambda i: (0, 0)),
                  pl.BlockSpec((1, GW), lambda i: (0, 0)),
                  pl.BlockSpec((1, GW), lambda i: (0, 0))],
        out_specs=pl.BlockSpec((CHUNK, GW), lambda i: (i, 0)),
        out_shape=jax.ShapeDtypeStruct((T_P, GW), bf16),
        compiler_params=_cparams(("arbitrary",)),
        name="gmlp_prompt",
    )(z_all, z_all, gm_ws, gm_bs.T, ln_w.reshape(1, GW), ln_b.reshape(1, GW))


def _gmlp_sample_kernel(u_ref, gv_ref, w0_ref, b0_ref, lnw_ref, lnb_ref, vn_ref, o_ref):
    vn = _layer_norm(jax.nn.gelu(gv_ref[...]), lnw_ref[...], lnb_ref[...])
    vn_ref[...] = vn
    o_ref[...] = (jax.nn.gelu(u_ref[...]) * (vn * w0_ref[...] + b0_ref[...])).astype(o_ref.dtype)


def _gmlp_sample(z_all, gm_ws, gm_bs, ln_w, ln_b):
    w0 = jnp.repeat(gm_ws[:, 0, 0], GC).reshape(1, GW)
    b0 = jnp.repeat(gm_bs[:, 0], GC).reshape(1, GW)
    blk = T_P // T_S
    return pl.pallas_call(
        _gmlp_sample_kernel,
        grid=(1,),
        in_specs=[pl.BlockSpec((T_S, GW), lambda i: (blk, COL_U)),
                  pl.BlockSpec((T_S, GW), lambda i: (blk, COL_GV)),
                  pl.BlockSpec((1, GW), lambda i: (0, 0)),
                  pl.BlockSpec((1, GW), lambda i: (0, 0)),
                  pl.BlockSpec((1, GW), lambda i: (0, 0)),
                  pl.BlockSpec((1, GW), lambda i: (0, 0))],
        out_specs=[pl.BlockSpec((T_S, GW), lambda i: (0, 0)),
                   pl.BlockSpec((T_S, GW), lambda i: (0, 0))],
        out_shape=[jax.ShapeDtypeStruct((T_S, GW), f32),
                   jax.ShapeDtypeStruct((T_S, GW), bf16)],
        compiler_params=_cparams(("arbitrary",)),
        name="gmlp_sample",
    )(z_all, z_all, w0, b0, ln_w.reshape(1, GW), ln_b.reshape(1, GW))


def _merge_kernel(a_ref, b_ref, ga_ref, gb_ref, woa_ref, wob_ref, o_ref):
    ya = jnp.dot(a_ref[...].astype(bf16), woa_ref[...], preferred_element_type=f32)
    yb = jnp.dot(b_ref[...].astype(bf16), wob_ref[...], preferred_element_type=f32)
    o_ref[...] = (jax.nn.sigmoid(ga_ref[...]) * ya + jax.nn.sigmoid(gb_ref[...]) * yb).astype(o_ref.dtype)


def _merge(a, bm, z_all, w_oa, w_ob, tm, z_blk0):
    m = a.shape[0]
    resident = lambda: pl.BlockSpec((D, D), lambda i: (0, 0), pipeline_mode=pl.Buffered(1))
    return pl.pallas_call(
        _merge_kernel,
        grid=(m // tm,),
        in_specs=[pl.BlockSpec((tm, D), lambda i: (i, 0)),
                  pl.BlockSpec((tm, D), lambda i: (i, 0)),
                  pl.BlockSpec((tm, D), lambda i: (z_blk0 + i, COL_GA)),
                  pl.BlockSpec((tm, D), lambda i: (z_blk0 + i, COL_GB)),
                  resident(), resident()],
        out_specs=pl.BlockSpec((tm, D), lambda i: (i, 0)),
        out_shape=jax.ShapeDtypeStruct((m, D), bf16),
        compiler_params=_cparams(("arbitrary",)),
        name="merge",
    )(a, bm, z_all, z_all, w_oa, w_ob)


def _post_kernel(m_ref, x_ref, g1_ref, sc_ref, sh_ref, nw_ref, wout_ref, wr_ref, br_ref,
                 x1_ref, h2_ref, idx_ref, gate_ref, rank_ref, cnt_ref):
    i = pl.program_id(0)

    @pl.when(i == 0)
    def _():
        cnt_ref[...] = jnp.zeros_like(cnt_ref)

    tm = m_ref.shape[0]
    y = jnp.dot(m_ref[...], wout_ref[...], preferred_element_type=f32)
    x1 = x_ref[...] + g1_ref[...] * y
    x1_ref[...] = x1
    xn = x1 * lax.rsqrt(jnp.mean(x1 * x1, axis=-1, keepdims=True) + EPS) * nw_ref[...]
    h2 = xn * (1.0 + sc_ref[...]) + sh_ref[...]
    h2_ref[...] = h2

    h_hi = h2.astype(bf16)
    h_lo = (h2 - h_hi.astype(f32)).astype(bf16)
    p_hi = jnp.dot(h_hi, wr_ref[...], preferred_element_type=f32)
    p_lo = jnp.dot(h_lo, wr_ref[...], preferred_element_type=f32)
    logits = p_hi[:, :LANES] + p_hi[:, LANES:] + p_lo[:, :LANES] + br_ref[...]

    lane = lax.broadcasted_iota(i32, (tm, LANES), 1).astype(f32)
    col = lax.broadcasted_iota(i32, (tm, TOP_K), 1)
    work = jnp.where(lane < N_EXPERTS, logits, -jnp.inf)
    member = jnp.zeros((tm, LANES), f32)
    vals, sels = [], []
    idx_out = jnp.zeros((tm, TOP_K), i32)
    for k in range(TOP_K):
        mx = jnp.max(work, axis=1, keepdims=True)
        ix = jnp.min(jnp.where(work == mx, lane, float(LANES)), axis=1, keepdims=True)
        sel = lane == ix
        vals.append(mx)
        sels.append(sel)
        idx_out = jnp.where(col == k, ix.astype(i32), idx_out)
        member = jnp.where(sel, 1.0, member)
        work = jnp.where(sel, -jnp.inf, work)
    idx_ref[...] = idx_out

    exps = [jnp.exp(v - vals[0]) for v in vals]
    den = exps[0] + exps[1] + exps[2] + exps[3]
    gate_out = jnp.zeros((tm, TOP_K), f32)
    for k in range(TOP_K):
        gate_out = jnp.where(col == k, exps[k] / den, gate_out)
    gate_ref[...] = gate_out

    r = lax.broadcasted_iota(i32, (tm, tm), 0)
    c = lax.broadcasted_iota(i32, (tm, tm), 1)
    lower = jnp.where(c < r, 1.0, 0.0).astype(bf16)
    before = jnp.dot(lower, member.astype(bf16), preferred_element_type=f32) + cnt_ref[...]
    rank_out = jnp.zeros((tm, TOP_K), i32)
    for k in range(TOP_K):
        rk = jnp.sum(jnp.where(sels[k], before, 0.0), axis=1, keepdims=True)
        rank_out = jnp.where(col == k, rk.astype(i32), rank_out)
    rank_ref[...] = rank_out
    cnt_ref[...] += jnp.sum(member, axis=0, keepdims=True)


def _post(m, x, mod, mod_is_rows, nw, w_out, w_router, b_router, tm, per_b):
    mm = m.shape[0]
    if mod_is_rows:
        mspec = lambda chunk: pl.BlockSpec((tm, D), lambda i: (i, chunk))
    else:
        mspec = lambda chunk: pl.BlockSpec((None, 1, D), lambda i: (i // per_b, 0, chunk))
    in_specs = [pl.BlockSpec((tm, D), lambda i: (i, 0)),
                pl.BlockSpec((tm, D), lambda i: (i, 0)),
                mspec(MOD_G1), mspec(MOD_SC2), mspec(MOD_SH2),
                pl.BlockSpec((1, D), lambda i: (0, 0)),
                pl.BlockSpec((D, D), lambda i: (0, 0), pipeline_mode=pl.Buffered(1)),
                pl.BlockSpec((D, 2 * LANES), lambda i: (0, 0)),
                pl.BlockSpec((1, LANES), lambda i: (0, 0))]
    args = [m, x, mod, mod, mod, nw, w_out, w_router, b_router]
    return pl.pallas_call(
        _post_kernel,
        grid=(mm // tm,),
        in_specs=in_specs,
        out_specs=[pl.BlockSpec((tm, D), lambda i: (i, 0)),
                   pl.BlockSpec((tm, D), lambda i: (i, 0)),
                   pl.BlockSpec((tm, TOP_K), lambda i: (i, 0)),
                   pl.BlockSpec((tm, TOP_K), lambda i: (i, 0)),
                   pl.BlockSpec((tm, TOP_K), lambda i: (i, 0)),
                   pl.BlockSpec((1, LANES), lambda i: (0, 0))],
        out_shape=[jax.ShapeDtypeStruct((mm, D), f32),
                   jax.ShapeDtypeStruct((mm, D), f32),
                   jax.ShapeDtypeStruct((mm, TOP_K), i32),
                   jax.ShapeDtypeStruct((mm, TOP_K), f32),
                   jax.ShapeDtypeStruct((mm, TOP_K), i32),
                   jax.ShapeDtypeStruct((1, LANES), f32)],
        compiler_params=_cparams(("arbitrary",)),
        name="post_mixer",
    )(*args)


DISP_TB = 128


DISP_PT = T_P // DISP_TB


def _dispatch_kernel(zblk_ref, dest_ref, h2p_ref, h2s_ref, xbuf_ref, zero_ref, sem):
    i = pl.program_id(0)

    def row_copy(src_ref, t, d):
        return pltpu.make_async_copy(src_ref.at[pl.ds(t, 1)], xbuf_ref.at[pl.ds(d, 1)], sem)

    def zero_copy(blk):
        return pltpu.make_async_copy(zero_ref, xbuf_ref.at[pl.ds(blk * MOE_BM, MOE_BM)], sem)

    @pl.when(i == 0)
    def _():
        zero_ref[...] = jnp.zeros_like(zero_ref)
        used = zblk_ref[N_EXPERTS]

        def zstart(e, c):
            zero_copy(zblk_ref[e]).start()
            return c

        def tstart(b, c):
            zero_copy(b).start()
            return c

        def zwait(e, c):
            zero_copy(0).wait()
            return c

        lax.fori_loop(0, N_EXPERTS, zstart, 0)
        lax.fori_loop(used, MOE_NBLK, tstart, 0)
        lax.fori_loop(0, N_EXPERTS, zwait, 0)
        lax.fori_loop(used, MOE_NBLK, zwait, 0)

    def issue(src_ref, base):
        def start(n, c):
            t = base + lax.shift_right_logical(n, 2)
            row_copy(src_ref, t, dest_ref[0, 0, n]).start()
            return c
        lax.fori_loop(0, DISP_TB * TOP_K, start, 0)

    @pl.when(i < DISP_PT)
    def _():
        issue(h2p_ref, i * DISP_TB)

    @pl.when(i == DISP_PT)
    def _():
        issue(h2s_ref, 0)

    def wait(n, c):
        row_copy(h2p_ref, 0, 0).wait()
        return c

    lax.fori_loop(0, DISP_TB * TOP_K, wait, 0)


def _dispatch(zero_blk, dest_all, h2_p, h2_s):
    nt = T_ALL // DISP_TB
    return pl.pallas_call(
        _dispatch_kernel,
        grid_spec=pltpu.PrefetchScalarGridSpec(
            num_scalar_prefetch=1,
            grid=(nt,),
            in_specs=[pl.BlockSpec((1, 1, DISP_TB * TOP_K), lambda i, zb: (i, 0, 0), memory_space=pltpu.SMEM),
                      pl.BlockSpec(memory_space=pl.ANY),
                      pl.BlockSpec(memory_space=pl.ANY)],
            out_specs=pl.BlockSpec(memory_space=pl.ANY),
            scratch_shapes=[pltpu.VMEM((MOE_BM, D), f32), pltpu.SemaphoreType.DMA(())]),
        out_shape=jax.ShapeDtypeStruct((MOE_NROWS, D), f32),
        compiler_params=_cparams(("arbitrary",)),
        name="moe_dispatch",
    )(zero_blk, dest_all.reshape(nt, 1, DISP_TB * TOP_K), h2_p, h2_s)


def _moe_ffn_kernel(e_ref, sb_ref, nb_ref, xbuf_ref, wg_ref, wu_ref, bg_ref, bu_ref, wd_ref, bd_ref,
                    ybuf_ref, x_ref, acc_ref, wgb_ref, wub_ref, wdb_ref, sem_in, sem_out):
    p = pl.program_id(0)
    j = pl.program_id(1)
    nb = nb_ref[p]
    sb = sb_ref[p]

    def in_copy(r):
        return pltpu.make_async_copy(xbuf_ref.at[pl.ds((sb + r) * MOE_BM, MOE_BM)],
                                     x_ref.at[pl.ds(r * MOE_BM, MOE_BM)], sem_in)

    def out_copy(r):
        return pltpu.make_async_copy(acc_ref.at[pl.ds(r * MOE_BM, MOE_BM)],
                                     ybuf_ref.at[pl.ds((sb + r) * MOE_BM, MOE_BM)], sem_out)

    def for_blocks(fn):
        def body(r, c):
            fn(r)
            return c
        lax.fori_loop(0, nb, body, 0)

    @pl.when((p == 0) & (j == 0))
    def _():
        used = nb_ref[MOE_PMAX]
        acc_ref[0:MOE_BM, :] = jnp.zeros((MOE_BM, D), f32)

        def tail_copy(b):
            return pltpu.make_async_copy(acc_ref.at[pl.ds(0, MOE_BM)],
                                         ybuf_ref.at[pl.ds(b * MOE_BM, MOE_BM)], sem_out)

        def tstart(b, c):
            tail_copy(b).start()
            return c

        def twait(b, c):
            tail_copy(0).wait()
            return c

        lax.fori_loop(used, MOE_NBLK, tstart, 0)
        lax.fori_loop(used, MOE_NBLK, twait, 0)

    @pl.when(nb > 0)
    def _():
        @pl.when(j == 0)
        def _():
            for_blocks(lambda r: in_copy(r).start())
            bd = jnp.broadcast_to(bd_ref[...], (MOE_BM, D))

            def init(r):
                acc_ref[pl.ds(pl.multiple_of(r * MOE_BM, MOE_BM), MOE_BM), :] = bd
            for_blocks(init)
            for_blocks(lambda r: in_copy(r).wait())

        wgb_ref[...] = wg_ref[...].astype(bf16)
        wub_ref[...] = wu_ref[...].astype(bf16)
        wdb_ref[...] = wd_ref[...].astype(bf16)
        bg = bg_ref[...]
        bu = bu_ref[...]

        def block(r):
            rows = pl.ds(pl.multiple_of(r * MOE_BM, MOE_BM), MOE_BM)
            x = x_ref[rows, :].astype(bf16)
            gate = jnp.dot(x, wgb_ref[...], preferred_element_type=f32) + bg
            up = jnp.dot(x, wub_ref[...], preferred_element_type=f32) + bu
            gate = jnp.minimum(gate, SWIGLU_LIMIT)
            up = jnp.clip(up, -SWIGLU_LIMIT, SWIGLU_LIMIT)
            act = (up + 1.0) * (gate * jax.nn.sigmoid(gate * SWIGLU_ALPHA))
            acc_ref[rows, :] += jnp.dot(act.astype(bf16), wdb_ref[...], preferred_element_type=f32)
        for_blocks(block)

        @pl.when(j == MOE_J - 1)
        def _():
            for_blocks(lambda r: out_copy(r).start())
            for_blocks(lambda r: out_copy(r).wait())


def _moe_ffn(e_arr, sb_arr, nb_arr, xbuf, w_gate_up, b_gate_up, w_down, b_down):
    def jj(p, j, nb):
        return jnp.where(nb[p] > 0, j, MOE_J - 1)

    in_specs = [
        pl.BlockSpec(memory_space=pl.ANY),
        pl.BlockSpec((None, D, MOE_TF), lambda p, j, e, sb, nb: (e[p], 0, jj(p, j, nb))),
        pl.BlockSpec((None, D, MOE_TF), lambda p, j, e, sb, nb: (e[p], 0, MOE_J + jj(p, j, nb))),
        pl.BlockSpec((None, 1, MOE_TF), lambda p, j, e, sb, nb: (e[p], 0, jj(p, j, nb))),
        pl.BlockSpec((None, 1, MOE_TF), lambda p, j, e, sb, nb: (e[p], 0, MOE_J + jj(p, j, nb))),
        pl.BlockSpec((None, MOE_TF, D), lambda p, j, e, sb, nb: (e[p], jj(p, j, nb), 0)),
        pl.BlockSpec((None, 1, D), lambda p, j, e, sb, nb: (e[p], 0, 0)),
    ]
    return pl.pallas_call(
        _moe_ffn_kernel,
        grid_spec=pltpu.PrefetchScalarGridSpec(
            num_scalar_prefetch=3,
            grid=(MOE_PMAX, MOE_J),
            in_specs=in_specs,
            out_specs=pl.BlockSpec(memory_space=pl.ANY),
            scratch_shapes=[pltpu.VMEM((MOE_RB * MOE_BM, D), f32),
                            pltpu.VMEM((MOE_RB * MOE_BM, D), f32),
                            pltpu.VMEM((D, MOE_TF), bf16),
                            pltpu.VMEM((D, MOE_TF), bf16),
                            pltpu.VMEM((MOE_TF, D), bf16),
                            pltpu.SemaphoreType.DMA(()),
                            pltpu.SemaphoreType.DMA(())]),
        out_shape=jax.ShapeDtypeStruct((MOE_NROWS, D), f32),
        compiler_params=_cparams(("arbitrary", "arbitrary"), vmem=56 * 1024 * 1024),
        name="moe_ffn",
    )(e_arr, sb_arr, nb_arr, xbuf, w_gate_up, w_gate_up,
      b_gate_up.reshape(N_EXPERTS, 1, 2 * D_EXPERT), b_gate_up.reshape(N_EXPERTS, 1, 2 * D_EXPERT),
      w_down, b_down.reshape(N_EXPERTS, 1, D))


COMB_TB = 128


def _combine_kernel(dest_ref, ybuf_ref, gate_ref, x1_ref, g2_ref, fw_ref, o_ref, rows_ref, sem):
    def row_copy(n, d):
        return pltpu.make_async_copy(ybuf_ref.at[pl.ds(d, 1)], rows_ref.at[pl.ds(n, 1)], sem)

    def start(n, c):
        slot = (n & (TOP_K - 1)) * COMB_TB + lax.shift_right_logical(n, 2)
        row_copy(slot, dest_ref[0, 0, n]).start()
        return c

    def wait(n, c):
        row_copy(0, 0).wait()
        return c

    lax.fori_loop(0, COMB_TB * TOP_K, start, 0)
    lax.fori_loop(0, COMB_TB * TOP_K, wait, 0)

    gates = gate_ref[...]
    f = gates[:, 0:1] * rows_ref[0:COMB_TB, :]
    for k in range(1, TOP_K):
        f = f + gates[:, k:k + 1] * rows_ref[k * COMB_TB:(k + 1) * COMB_TB, :]
    x2 = x1_ref[...] + g2_ref[...] * f
    o_ref[...] = x2 * lax.rsqrt(jnp.mean(x2 * x2, axis=-1, keepdims=True) + EPS) * fw_ref[...]


def _combine(dest, ybuf, gates, x1, mod, mod_is_rows, per_b, final_w):
    mm = x1.shape[0]
    nt = mm // COMB_TB
    if mod_is_rows:
        g2spec = pl.BlockSpec((COMB_TB, D), lambda i: (i, MOD_G2))
    else:
        g2spec = pl.BlockSpec((None, 1, D), lambda i: (i // per_b, 0, MOD_G2))
    return pl.pallas_call(
        _combine_kernel,
        grid=(nt,),
        in_specs=[pl.BlockSpec((1, 1, COMB_TB * TOP_K), lambda i: (i, 0, 0), memory_space=pltpu.SMEM),
                  pl.BlockSpec(memory_space=pl.ANY),
                  pl.BlockSpec((COMB_TB, TOP_K), lambda i: (i, 0)),
                  pl.BlockSpec((COMB_TB, D), lambda i: (i, 0)),
                  g2spec,
                  pl.BlockSpec((1, D), lambda i: (0, 0))],
        out_specs=pl.BlockSpec((COMB_TB, D), lambda i: (i, 0)),
        out_shape=jax.ShapeDtypeStruct((mm, D), f32),
        scratch_shapes=[pltpu.VMEM((TOP_K * COMB_TB, D), f32), pltpu.SemaphoreType.DMA(())],
        compiler_params=_cparams(("arbitrary",)),
        name="moe_combine",
    )(dest.reshape(nt, 1, COMB_TB * TOP_K), ybuf, gates, x1, mod, final_w.reshape(1, D))


def _routing_tables(cnt_p, cnt_s, idx_p, rank_p, idx_s, rank_s):
    cp = cnt_p[0, :N_EXPERTS].astype(i32)
    cs = cnt_s[0, :N_EXPERTS].astype(i32)
    nblk = (cp + cs + MOE_BM - 1) // MOE_BM
    blk_end = jnp.cumsum(nblk)
    blk_start = blk_end - nblk
    row_start = blk_start * MOE_BM
    dest_p = row_start[idx_p] + rank_p
    dest_s = (row_start + cp)[idx_s] + rank_s
    used = blk_end[-1:]
    zero_blk = jnp.concatenate([jnp.maximum(blk_end - 1, 0), used]).astype(i32)
    npass = (nblk + MOE_RB - 1) // MOE_RB
    pass_end = jnp.cumsum(npass)
    total = pass_end[-1]
    pid = jnp.arange(MOE_PMAX, dtype=i32)
    pid_c = jnp.minimum(pid, total - 1)
    pe = jnp.minimum(jnp.sum((pass_end[None, :] <= pid_c[:, None]).astype(i32), axis=1), N_EXPERTS - 1)
    local = pid_c - (pass_end - npass)[pe]
    active = pid < total
    sb = jnp.where(active, blk_start[pe] + local * MOE_RB, 0).astype(i32)
    nb = jnp.where(active, jnp.clip(nblk[pe] - local * MOE_RB, 0, MOE_RB), 0)
    nb = jnp.concatenate([nb, used]).astype(i32)
    return dest_p.astype(i32), dest_s.astype(i32), zero_blk, pe, sb, nb


def kernel(x_prompt, x_sample, c_prompt, c_sample, state_ret, norm1_w, norm2_w, w_ada, b_ada, w_in,
           ret_gn_w, gm_ln_w, gm_ln_b, gm_ws, gm_bs, w_oa, w_ob, w_out, w_router, b_router,
           w_gate_up, b_gate_up, w_down, b_down, final_norm_w):
    x_p = x_prompt.reshape(T_P, D)
    x_s = x_sample.reshape(T_S, D)
    per_b = lambda tm: SEQ // tm

    mod = _ada(jnp.concatenate([c_prompt, c_sample], axis=0), w_ada[0], b_ada[0])
    mod_p = mod[:BATCH].reshape(BATCH, 1, 6 * D)
    mod_s = mod[BATCH:]

    n1 = norm1_w[0].reshape(1, D)
    h_all = _norm1(x_p, x_s, n1, mod_p, mod_s)
    z_all = _proj_in(h_all, w_in[0])

    a_p, s_p = _ret_prompt(z_all, ret_gn_w[0])
    z_s = z_all[T_P:]
    qt = z_s[:, :HEADS * DK].reshape(T_S, HEADS, DK).transpose(0, 2, 1)
    kt = z_s[:, HEADS * DK:2 * HEADS * DK].reshape(T_S, HEADS, DK).transpose(0, 2, 1)
    v3 = z_s[:, 2 * HEADS * DK:2 * HEADS * DK + HEADS * DV].reshape(T_S, HEADS, DV)
    g3 = z_s[:, 2 * HEADS * DK + HEADS * DV:2 * HEADS * DK + 2 * HEADS * DV].reshape(T_S, HEADS, DV)
    a_s, s_s = _ret_sample(qt, kt, v3, g3, ret_gn_w[0], state_ret[0])
    bm_p = _gmlp_prompt(z_all, gm_ws[0], gm_bs[0], gm_ln_w[0], gm_ln_b[0])
    vn_s, bm_s = _gmlp_sample(z_all, gm_ws[0], gm_bs[0], gm_ln_w[0], gm_ln_b[0])

    woa, wob, wout = w_oa[0].astype(bf16), w_ob[0].astype(bf16), w_out[0].astype(bf16)
    m_p = _merge(a_p, bm_p, z_all, woa, wob, 256, 0)
    m_s = _merge(a_s.reshape(T_S, HEADS * DV), bm_s, z_all, woa, wob, T_S, T_P // T_S)

    n2 = norm2_w[0].reshape(1, D)
    wr_hi = w_router[0].astype(bf16)
    wr_lo = (w_router[0] - wr_hi.astype(f32)).astype(bf16)
    lane_pad = ((0, 0), (0, LANES - N_EXPERTS))
    wr_cat = jnp.concatenate([jnp.pad(wr_hi, lane_pad), jnp.pad(wr_lo, lane_pad)], axis=1)
    br = jnp.pad(b_router[0].reshape(1, N_EXPERTS), lane_pad)
    x1_p, h2_p, idx_p, gate_p, rank_p, cnt_p = _post(
        m_p, x_p, mod_p, False, n2, wout, wr_cat, br, 512, per_b(512))
    x1_s, h2_s, idx_s, gate_s, rank_s, cnt_s = _post(
        m_s, x_s, mod_s, True, n2, wout, wr_cat, br, T_S, 1)

    dest_p, dest_s, zero_blk, pe, sb, nb = _routing_tables(cnt_p, cnt_s, idx_p, rank_p, idx_s, rank_s)
    xbuf = _dispatch(zero_blk, jnp.concatenate([dest_p, dest_s], axis=0), h2_p, h2_s)
    ybuf = _moe_ffn(pe, sb, nb, xbuf, w_gate_up[0], b_gate_up[0], w_down[0], b_down[0])
    y_p = _combine(dest_p, ybuf, gate_p, x1_p, mod_p, False, per_b(COMB_TB), final_norm_w)
    y_s = _combine(dest_s, ybuf, gate_s, x1_s, mod_s, True, 1, final_norm_w)

    return (y_p.reshape(BATCH, SEQ, D),
            y_s.reshape(DEC_BATCH, 1, D),
            s_p.reshape(1, BATCH, HEADS, DK, DV),
            s_s.reshape(1, DEC_BATCH, HEADS, DK, DV),
            vn_s.reshape(1, DEC_BATCH, 1, GW))
```

```python
import functools

import numpy as np
import jax
import jax.numpy as jnp
from jax import lax
from jax.experimental import pallas as pl
from jax.experimental.pallas import tpu as pltpu

f32 = jnp.float32
bf16 = jnp.bfloat16
i32 = jnp.int32

D = 2048
BATCH = 4
SEQ = 2048
DEC_BATCH = 128
PAST_LEN = 16384
HEADS = 8
DK = 128
DV = 256
CHUNK = 128
ROPE_THETA = 10000.0
GROUPS = 8
GW = 2048
GC = GW // GROUPS
N_EXPERTS = 32
TOP_K = 4
D_EXPERT = 2048
SWIGLU_LIMIT = 7.0
SWIGLU_ALPHA = 1.702
EPS = 1e-6

T_P = BATCH * SEQ
T_S = DEC_BATCH
T_ALL = T_P + T_S
D_IN = 14336
COL_Q, COL_K = 0, 1
COL_V, COL_G, COL_U, COL_GV, COL_GA, COL_GB = 1, 2, 3, 4, 5, 6
MOD_SH1, MOD_SC1, MOD_G1, MOD_SH2, MOD_SC2, MOD_G2 = range(6)

LANES = 128
SUBLANES = 8
VMEM_LIMIT = 48 * 1024 * 1024

MOE_BM = 128
MOE_RB = 12
MOE_TF = 256
MOE_J = D_EXPERT // MOE_TF
TK_ALL = T_ALL * TOP_K
MOE_NBLK = (TK_ALL + N_EXPERTS * (MOE_BM - 1)) // MOE_BM
MOE_NROWS = MOE_NBLK * MOE_BM
MOE_PMAX = N_EXPERTS + MOE_NBLK // MOE_RB


def _cparams(sem, vmem=VMEM_LIMIT):
    return pltpu.CompilerParams(dimension_semantics=sem, vmem_limit_bytes=vmem)


def _ada_kernel(c_ref, w_ref, b_ref, o_ref):
    o_ref[...] = jnp.dot(c_ref[...].astype(bf16), w_ref[...].astype(bf16),
                         preferred_element_type=f32) + b_ref[...]


def _ada(c_all, w_ada, b_ada):
    m = c_all.shape[0]
    tn = 1024
    return pl.pallas_call(
        _ada_kernel,
        grid=(6 * D // tn,),
        in_specs=[pl.BlockSpec((m, D), lambda j: (0, 0)),
                  pl.BlockSpec((D, tn), lambda j: (0, j)),
                  pl.BlockSpec((1, tn), lambda j: (0, j))],
        out_specs=pl.BlockSpec((m, tn), lambda j: (0, j)),
        out_shape=jax.ShapeDtypeStruct((m, 6 * D), f32),
        compiler_params=_cparams(("arbitrary",)),
        name="ada_mod",
    )(c_all, w_ada, b_ada.reshape(1, 6 * D))


def _norm_mod_kernel(x_ref, nw_ref, sc_ref, sh_ref, o_ref):
    x = x_ref[...]
    y = x * lax.rsqrt(jnp.mean(x * x, axis=-1, keepdims=True) + EPS) * nw_ref[...]
    o_ref[...] = (y * (1.0 + sc_ref[...]) + sh_ref[...]).astype(o_ref.dtype)


NORM1_TM = T_S
NORM1_PT = T_P // NORM1_TM


def _norm1_kernel(xp_ref, xs_ref, nw_ref, scp_ref, shp_ref, scs_ref, shs_ref, o_ref):
    i = pl.program_id(0)

    @pl.when(i < NORM1_PT)
    def _():
        _norm_mod_kernel(xp_ref, nw_ref, scp_ref, shp_ref, o_ref)

    @pl.when(i == NORM1_PT)
    def _():
        _norm_mod_kernel(xs_ref, nw_ref, scs_ref, shs_ref, o_ref)


def _norm1(x_p, x_s, nw, mod_p, mod_s):
    tm = NORM1_TM
    per_b = SEQ // tm
    pt = lambda i: jnp.minimum(i, NORM1_PT - 1)
    return pl.pallas_call(
        _norm1_kernel,
        grid=(NORM1_PT + 1,),
        in_specs=[pl.BlockSpec((tm, D), lambda i: (pt(i), 0)),
                  pl.BlockSpec((tm, D), lambda i: (0, 0)),
                  pl.BlockSpec((1, D), lambda i: (0, 0)),
                  pl.BlockSpec((None, 1, D), lambda i: (pt(i) // per_b, 0, MOD_SC1)),
                  pl.BlockSpec((None, 1, D), lambda i: (pt(i) // per_b, 0, MOD_SH1)),
                  pl.BlockSpec((tm, D), lambda i: (0, MOD_SC1)),
                  pl.BlockSpec((tm, D), lambda i: (0, MOD_SH1))],
        out_specs=pl.BlockSpec((tm, D), lambda i: (i, 0)),
        out_shape=jax.ShapeDtypeStruct((T_ALL, D), bf16),
        compiler_params=_cparams(("arbitrary",)),
        name="norm1",
    )(x_p, x_s, nw, mod_p, mod_p, mod_s, mod_s)


def _proj_in_kernel(h_ref, w_ref, o_ref, wb_ref):
    @pl.when(pl.program_id(1) == 0)
    def _():
        wb_ref[...] = w_ref[...].astype(bf16)

    o_ref[...] = jnp.dot(h_ref[...], wb_ref[...], preferred_element_type=f32)


def _proj_in(h_all, w_in):
    tm, tn = 640, 1024
    return pl.pallas_call(
        _proj_in_kernel,
        grid=(D_IN // tn, T_ALL // tm),
        in_specs=[pl.BlockSpec((tm, D), lambda j, i: (i, 0)),
                  pl.BlockSpec((D, tn), lambda j, i: (0, j))],
        out_specs=pl.BlockSpec((tm, tn), lambda j, i: (i, j)),
        out_shape=jax.ShapeDtypeStruct((T_ALL, D_IN), f32),
        scratch_shapes=[pltpu.VMEM((D, tn), bf16)],
        compiler_params=_cparams(("arbitrary", "arbitrary")),
        name="proj_in",
    )(h_all, w_in)


def _retention_consts():
    h = np.arange(HEADS, dtype=np.float64)
    lg = np.log1p(-np.exp2(-5.0 - h))
    idx = np.arange(CHUNK, dtype=np.float64)
    diff = idx[:, None] - idx[None, :]
    intra = np.where(diff >= 0, np.exp(np.maximum(diff, 0.0)[None] * lg[:, None, None]), 0.0)
    q_dec = np.exp((idx + 1.0)[:, None] * lg[None, :])
    k_dec = np.exp((CHUNK - 1.0 - idx)[:, None] * lg[None, :])
    s_dec = np.exp(CHUNK * lg)
    gamma = np.exp(lg)
    return (intra.astype(np.float32), q_dec.astype(np.float32), k_dec.astype(np.float32),
            [float(v) for v in s_dec], [float(v) for v in gamma])


def _rope_tables(pos):
    half = DK // 2
    freq = ROPE_THETA ** (-np.arange(half, dtype=np.float64) / half)
    ang = np.asarray(pos, dtype=np.float64)[:, None] * freq[None, :]
    cos, sin = np.cos(ang), np.sin(ang)
    cos_t = np.concatenate([cos, cos], axis=-1).astype(np.float32)
    sin_t = np.concatenate([-sin, sin], axis=-1).astype(np.float32)
    return cos_t, sin_t


def _group_norm_gate(o, gnw, g):
    mu = jnp.mean(o, axis=-1, keepdims=True)
    var = jnp.mean(jnp.square(o - mu), axis=-1, keepdims=True)
    on = (o - mu) * lax.rsqrt(var + EPS) * gnw
    return jax.nn.silu(g) * on


def _ret_prompt_kernel(s_dec, q_ref, k_ref, v_ref, g_ref, cos_ref, sin_ref, intra_ref, qd_ref, kd_ref,
                       gnw_ref, a_ref, s_ref):
    n = pl.program_id(1)

    @pl.when(n == 0)
    def _():
        s_ref[...] = jnp.zeros_like(s_ref)

    cos = cos_ref[...]
    sin = sin_ref[...]
    qd = qd_ref[...]
    kd = kd_ref[...]
    for h in range(HEADS):
        qh = q_ref[:, h * DK:(h + 1) * DK]
        kh = k_ref[:, h * DK:(h + 1) * DK]
        qh = (qh * cos + pltpu.roll(qh, DK // 2, 1) * sin) * (DK ** -0.5)
        kh = kh * cos + pltpu.roll(kh, DK // 2, 1) * sin
        vh = v_ref[:, h * DV:(h + 1) * DV].astype(bf16)
        s_old = s_ref[h]
        scores = lax.dot_general(qh.astype(bf16), kh.astype(bf16), (((1,), (1,)), ((), ())),
                                 preferred_element_type=f32) * intra_ref[h]
        o = jnp.dot(scores.astype(bf16), vh, preferred_element_type=f32)
        o = o + jnp.dot((qh * qd[:, h:h + 1]).astype(bf16), s_old.astype(bf16), preferred_element_type=f32)
        kw_t = (kh * kd[:, h:h + 1]).T.astype(bf16)
        s_ref[h] = s_old * s_dec[h] + jnp.dot(kw_t, vh, preferred_element_type=f32)
        a_ref[:, h * DV:(h + 1) * DV] = _group_norm_gate(
            o, gnw_ref[:, h * DV:(h + 1) * DV], g_ref[:, h * DV:(h + 1) * DV]).astype(a_ref.dtype)


def _ret_prompt(z_all, gn_w):
    intra, q_dec, k_dec, s_dec, _ = _retention_consts()
    cos_t, sin_t = _rope_tables(np.arange(SEQ))
    nch = SEQ // CHUNK
    row = lambda b, n: b * nch + n
    return pl.pallas_call(
        functools.partial(_ret_prompt_kernel, s_dec),
        grid=(BATCH, nch),
        in_specs=[pl.BlockSpec((CHUNK, HEADS * DK), lambda b, n: (row(b, n), COL_Q)),
                  pl.BlockSpec((CHUNK, HEADS * DK), lambda b, n: (row(b, n), COL_K)),
                  pl.BlockSpec((CHUNK, HEADS * DV), lambda b, n: (row(b, n), COL_V)),
                  pl.BlockSpec((CHUNK, HEADS * DV), lambda b, n: (row(b, n), COL_G)),
                  pl.BlockSpec((CHUNK, DK), lambda b, n: (n, 0)),
                  pl.BlockSpec((CHUNK, DK), lambda b, n: (n, 0)),
                  pl.BlockSpec((HEADS, CHUNK, CHUNK), lambda b, n: (0, 0, 0)),
                  pl.BlockSpec((CHUNK, HEADS), lambda b, n: (0, 0)),
                  pl.BlockSpec((CHUNK, HEADS), lambda b, n: (0, 0)),
                  pl.BlockSpec((1, HEADS * DV), lambda b, n: (0, 0))],
        out_specs=[pl.BlockSpec((CHUNK, HEADS * DV), lambda b, n: (row(b, n), 0)),
                   pl.BlockSpec((None, HEADS, DK, DV), lambda b, n: (b, 0, 0, 0))],
        out_shape=[jax.ShapeDtypeStruct((T_P, HEADS * DV), bf16),
                   jax.ShapeDtypeStruct((BATCH, HEADS, DK, DV), f32)],
        compiler_params=_cparams(("arbitrary", "arbitrary")),
        name="retention_prompt",
    )(z_all, z_all, z_all, z_all, jnp.asarray(cos_t), jnp.asarray(sin_t), jnp.asarray(intra),
      jnp.asarray(q_dec), jnp.asarray(k_dec), gn_w.reshape(1, HEADS * DV))


RET_S_BT = 4


def _ret_sample_kernel(gamma, qt_ref, kt_ref, v_ref, g_ref, cos_ref, sin_ref, gnw_ref, s_in_ref,
                       a_ref, s_out_ref):
    cos = cos_ref[...]
    sin = sin_ref[...]

    def rope_t(x):
        rolled = jnp.concatenate([x[DK // 2:], x[:DK // 2]], axis=0)
        return x * cos + rolled * sin

    def body(t, carry):
        qt = rope_t(qt_ref[t]) * (DK ** -0.5)
        kt = rope_t(kt_ref[t])
        v8 = v_ref[t]
        g8 = g_ref[t]
        for h in range(HEADS):
            s_new = s_in_ref[t, h] * gamma[h] + kt[:, h:h + 1] * v8[h:h + 1, :]
            s_out_ref[t, h] = s_new
            o = jnp.sum(qt[:, h:h + 1] * s_new, axis=0, keepdims=True)
            a_ref[t, h:h + 1, :] = _group_norm_gate(o, gnw_ref[h:h + 1, :], g8[h:h + 1, :])
        return carry

    lax.fori_loop(0, RET_S_BT, body, 0)


def _ret_sample(qt, kt, v3, g3, gn_w, state):
    _, _, _, _, gamma = _retention_consts()
    cos_t, sin_t = _rope_tables(np.array([PAST_LEN]))
    cos8 = np.ascontiguousarray(np.broadcast_to(cos_t[0][:, None], (DK, HEADS)))
    sin8 = np.ascontiguousarray(np.broadcast_to(sin_t[0][:, None], (DK, HEADS)))
    bt = RET_S_BT
    return pl.pallas_call(
        functools.partial(_ret_sample_kernel, gamma),
        grid=(T_S // bt,),
        in_specs=[pl.BlockSpec((bt, DK, HEADS), lambda i: (i, 0, 0)),
                  pl.BlockSpec((bt, DK, HEADS), lambda i: (i, 0, 0)),
                  pl.BlockSpec((bt, HEADS, DV), lambda i: (i, 0, 0)),
                  pl.BlockSpec((bt, HEADS, DV), lambda i: (i, 0, 0)),
                  pl.BlockSpec((DK, HEADS), lambda i: (0, 0)),
                  pl.BlockSpec((DK, HEADS), lambda i: (0, 0)),
                  pl.BlockSpec((HEADS, DV), lambda i: (0, 0)),
                  pl.BlockSpec((bt, HEADS, DK, DV), lambda i: (i, 0, 0, 0))],
        out_specs=[pl.BlockSpec((bt, HEADS, DV), lambda i: (i, 0, 0)),
                   pl.BlockSpec((bt, HEADS, DK, DV), lambda i: (i, 0, 0, 0))],
        out_shape=[jax.ShapeDtypeStruct((T_S, HEADS, DV), f32),
                   jax.ShapeDtypeStruct((T_S, HEADS, DK, DV), f32)],
        compiler_params=_cparams(("arbitrary",)),
        name="retention_sample",
    )(qt, kt, v3, g3, jnp.asarray(cos8), jnp.asarray(sin8), gn_w.reshape(HEADS, DV), state)


def _layer_norm(x, w, b):
    mu = jnp.mean(x, axis=-1, keepdims=True)
    var = jnp.mean(jnp.square(x - mu), axis=-1, keepdims=True)
    return (x - mu) * lax.rsqrt(var + EPS) * w + b


def _gmlp_prompt_kernel(u_ref, gv_ref, ws_ref, bst_ref, lnw_ref, lnb_ref, o_ref):
    vn = _layer_norm(jax.nn.gelu(gv_ref[...]), lnw_ref[...], lnb_ref[...]).astype(bf16)
    r = lax.broadcasted_iota(i32, (CHUNK, CHUNK), 0)
    c = lax.broadcasted_iota(i32, (CHUNK, CHUNK), 1)
    causal = r >= c
    bst = bst_ref[...]
    for g in range(GROUPS):
        w = jnp.where(causal, ws_ref[g], 0.0).astype(bf16)
        mixed = jnp.dot(w, vn[:, g * GC:(g + 1) * GC], preferred_element_type=f32) + bst[:, g:g + 1]
        o_ref[:, g * GC:(g + 1) * GC] = (jax.nn.gelu(u_ref[:, g * GC:(g + 1) * GC]) * mixed).astype(o_ref.dtype)


def _gmlp_prompt(z_all, gm_ws, gm_bs, ln_w, ln_b):
    return pl.pallas_call(
        _gmlp_prompt_kernel,
        grid=(T_P // CHUNK,),
        in_specs=[pl.BlockSpec((CHUNK, GW), lambda i: (i, COL_U)),
                  pl.BlockSpec((CHUNK, GW), lambda i: (i, COL_GV)),
                  pl.BlockSpec((GROUPS, CHUNK, CHUNK), lambda i: (0, 0, 0)),
                  pl.BlockSpec((CHUNK, GROUPS), lambda i: (0, 0)),
                  pl.BlockSpec((1, GW), lambda i: (0, 0)),
                  pl.BlockSpec((1, GW), lambda i: (0, 0))],
        out_specs=pl.BlockSpec((CHUNK, GW), lambda i: (i, 0)),
        out_shape=jax.ShapeDtypeStruct((T_P, GW), bf16),
        compiler_params=_cparams(("arbitrary",)),
        name="gmlp_prompt",
    )(z_all, z_all, gm_ws, gm_bs.T, ln_w.reshape(1, GW), ln_b.reshape(1, GW))


def _gmlp_sample_kernel(u_ref, gv_ref, w0_ref, b0_ref, lnw_ref, lnb_ref, vn_ref, o_ref):
    vn = _layer_norm(jax.nn.gelu(gv_ref[...]), lnw_ref[...], lnb_ref[...])
    vn_ref[...] = vn
    o_ref[...] = (jax.nn.gelu(u_ref[...]) * (vn * w0_ref[...] + b0_ref[...])).astype(o_ref.dtype)


def _gmlp_sample(z_all, gm_ws, gm_bs, ln_w, ln_b):
    w0 = jnp.repeat(gm_ws[:, 0, 0], GC).reshape(1, GW)
    b0 = jnp.repeat(gm_bs[:, 0], GC).reshape(1, GW)
    blk = T_P // T_S
    return pl.pallas_call(
        _gmlp_sample_kernel,
        grid=(1,),
        in_specs=[pl.BlockSpec((T_S, GW), lambda i: (blk, COL_U)),
                  pl.BlockSpec((T_S, GW), lambda i: (blk, COL_GV)),
                  pl.BlockSpec((1, GW), lambda i: (0, 0)),
                  pl.BlockSpec((1, GW), lambda i: (0, 0)),
                  pl.BlockSpec((1, GW), lambda i: (0, 0)),
                  pl.BlockSpec((1, GW), lambda i: (0, 0))],
        out_specs=[pl.BlockSpec((T_S, GW), lambda i: (0, 0)),
                   pl.BlockSpec((T_S, GW), lambda i: (0, 0))],
        out_shape=[jax.ShapeDtypeStruct((T_S, GW), f32),
                   jax.ShapeDtypeStruct((T_S, GW), bf16)],
        compiler_params=_cparams(("arbitrary",)),
        name="gmlp_sample",
    )(z_all, z_all, w0, b0, ln_w.reshape(1, GW), ln_b.reshape(1, GW))


def _merge_kernel(a_ref, b_ref, ga_ref, gb_ref, woa_ref, wob_ref, o_ref):
    ya = jnp.dot(a_ref[...].astype(bf16), woa_ref[...], preferred_element_type=f32)
    yb = jnp.dot(b_ref[...].astype(bf16), wob_ref[...], preferred_element_type=f32)
    o_ref[...] = (jax.nn.sigmoid(ga_ref[...]) * ya + jax.nn.sigmoid(gb_ref[...]) * yb).astype(o_ref.dtype)


def _merge(a, bm, z_all, w_oa, w_ob, tm, z_blk0):
    m = a.shape[0]
    resident = lambda: pl.BlockSpec((D, D), lambda i: (0, 0), pipeline_mode=pl.Buffered(1))
    return pl.pallas_call(
        _merge_kernel,
        grid=(m // tm,),
        in_specs=[pl.BlockSpec((tm, D), lambda i: (i, 0)),
                  pl.BlockSpec((tm, D), lambda i: (i, 0)),
                  pl.BlockSpec((tm, D), lambda i: (z_blk0 + i, COL_GA)),
                  pl.BlockSpec((tm, D), lambda i: (z_blk0 + i, COL_GB)),
                  resident(), resident()],
        out_specs=pl.BlockSpec((tm, D), lambda i: (i, 0)),
        out_shape=jax.ShapeDtypeStruct((m, D), bf16),
        compiler_params=_cparams(("arbitrary",)),
        name="merge",
    )(a, bm, z_all, z_all, w_oa, w_ob)


def _post_kernel(m_ref, x_ref, g1_ref, sc_ref, sh_ref, nw_ref, wout_ref, wr_ref, br_ref,
                 x1_ref, h2_ref, idx_ref, gate_ref, rank_ref, cnt_ref):
    i = pl.program_id(0)

    @pl.when(i == 0)
    def _():
        cnt_ref[...] = jnp.zeros_like(cnt_ref)

    tm = m_ref.shape[0]
    y = jnp.dot(m_ref[...], wout_ref[...], preferred_element_type=f32)
    x1 = x_ref[...] + g1_ref[...] * y
    x1_ref[...] = x1
    xn = x1 * lax.rsqrt(jnp.mean(x1 * x1, axis=-1, keepdims=True) + EPS) * nw_ref[...]
    h2 = xn * (1.0 + sc_ref[...]) + sh_ref[...]
    h2_ref[...] = h2

    h_hi = h2.astype(bf16)
    h_lo = (h2 - h_hi.astype(f32)).astype(bf16)
    p_hi = jnp.dot(h_hi, wr_ref[...], preferred_element_type=f32)
    p_lo = jnp.dot(h_lo, wr_ref[...], preferred_element_type=f32)
    logits = p_hi[:, :LANES] + p_hi[:, LANES:] + p_lo[:, :LANES] + br_ref[...]

    lane = lax.broadcasted_iota(i32, (tm, LANES), 1).astype(f32)
    col = lax.broadcasted_iota(i32, (tm, TOP_K), 1)
    work = jnp.where(lane < N_EXPERTS, logits, -jnp.inf)
    member = jnp.zeros((tm, LANES), f32)
    vals, sels = [], []
    idx_out = jnp.zeros((tm, TOP_K), i32)
    for k in range(TOP_K):
        mx = jnp.max(work, axis=1, keepdims=True)
        ix = jnp.min(jnp.where(work == mx, lane, float(LANES)), axis=1, keepdims=True)
        sel = lane == ix
        vals.append(mx)
        sels.append(sel)
        idx_out = jnp.where(col == k, ix.astype(i32), idx_out)
        member = jnp.where(sel, 1.0, member)
        work = jnp.where(sel, -jnp.inf, work)
    idx_ref[...] = idx_out

    exps = [jnp.exp(v - vals[0]) for v in vals]
    den = exps[0] + exps[1] + exps[2] + exps[3]
    gate_out = jnp.zeros((tm, TOP_K), f32)
    for k in range(TOP_K):
        gate_out = jnp.where(col == k, exps[k] / den, gate_out)
    gate_ref[...] = gate_out

    r = lax.broadcasted_iota(i32, (tm, tm), 0)
    c = lax.broadcasted_iota(i32, (tm, tm), 1)
    lower = jnp.where(c < r, 1.0, 0.0).astype(bf16)
    before = jnp.dot(lower, member.astype(bf16), preferred_element_type=f32) + cnt_ref[...]
    rank_out = jnp.zeros((tm, TOP_K), i32)
    for k in range(TOP_K):
        rk = jnp.sum(jnp.where(sels[k], before, 0.0), axis=1, keepdims=True)
        rank_out = jnp.where(col == k, rk.astype(i32), rank_out)
    rank_ref[...] = rank_out
    cnt_ref[...] += jnp.sum(member, axis=0, keepdims=True)


def _post(m, x, mod, mod_is_rows, nw, w_out, w_router, b_router, tm, per_b):
    mm = m.shape[0]
    if mod_is_rows:
        mspec = lambda chunk: pl.BlockSpec((tm, D), lambda i: (i, chunk))
    else:
        mspec = lambda chunk: pl.BlockSpec((None, 1, D), lambda i: (i // per_b, 0, chunk))
    in_specs = [pl.BlockSpec((tm, D), lambda i: (i, 0)),
                pl.BlockSpec((tm, D), lambda i: (i, 0)),
                mspec(MOD_G1), mspec(MOD_SC2), mspec(MOD_SH2),
                pl.BlockSpec((1, D), lambda i: (0, 0)),
                pl.BlockSpec((D, D), lambda i: (0, 0), pipeline_mode=pl.Buffered(1)),
                pl.BlockSpec((D, 2 * LANES), lambda i: (0, 0)),
                pl.BlockSpec((1, LANES), lambda i: (0, 0))]
    args = [m, x, mod, mod, mod, nw, w_out, w_router, b_router]
    return pl.pallas_call(
        _post_kernel,
        grid=(mm // tm,),
        in_specs=in_specs,
        out_specs=[pl.BlockSpec((tm, D), lambda i: (i, 0)),
                   pl.BlockSpec((tm, D), lambda i: (i, 0)),
                   pl.BlockSpec((tm, TOP_K), lambda i: (i, 0)),
                   pl.BlockSpec((tm, TOP_K), lambda i: (i, 0)),
                   pl.BlockSpec((tm, TOP_K), lambda i: (i, 0)),
                   pl.BlockSpec((1, LANES), lambda i: (0, 0))],
        out_shape=[jax.ShapeDtypeStruct((mm, D), f32),
                   jax.ShapeDtypeStruct((mm, D), f32),
                   jax.ShapeDtypeStruct((mm, TOP_K), i32),
                   jax.ShapeDtypeStruct((mm, TOP_K), f32),
                   jax.ShapeDtypeStruct((mm, TOP_K), i32),
                   jax.ShapeDtypeStruct((1, LANES), f32)],
        compiler_params=_cparams(("arbitrary",)),
        name="post_mixer",
    )(*args)


DISP_TB = 128


DISP_PT = T_P // DISP_TB


def _dispatch_kernel(zblk_ref, dest_ref, h2p_ref, h2s_ref, xbuf_ref, zero_ref, sem):
    i = pl.program_id(0)

    def row_copy(src_ref, t, d):
        return pltpu.make_async_copy(src_ref.at[pl.ds(t, 1)], xbuf_ref.at[pl.ds(d, 1)], sem)

    def zero_copy(blk):
        return pltpu.make_async_copy(zero_ref, xbuf_ref.at[pl.ds(blk * MOE_BM, MOE_BM)], sem)

    @pl.when(i == 0)
    def _():
        zero_ref[...] = jnp.zeros_like(zero_ref)
        used = zblk_ref[N_EXPERTS]

        def zstart(e, c):
            zero_copy(zblk_ref[e]).start()
            return c

        def tstart(b, c):
            zero_copy(b).start()
            return c

        def zwait(e, c):
            zero_copy(0).wait()
            return c

        lax.fori_loop(0, N_EXPERTS, zstart, 0)
        lax.fori_loop(used, MOE_NBLK, tstart, 0)
        lax.fori_loop(0, N_EXPERTS, zwait, 0)
        lax.fori_loop(used, MOE_NBLK, zwait, 0)

    def issue(src_ref):
        def group(gi, c):
            for tt in range(SUBLANES):
                t = pl.multiple_of(gi * SUBLANES, SUBLANES) + tt
                for k in range(TOP_K):
                    d = dest_ref[0, 0, (gi * SUBLANES + tt) * TOP_K + k]
                    row_copy(src_ref, t, d).start()
            return c
        lax.fori_loop(0, DISP_TB // SUBLANES, group, 0)

    @pl.when(i < DISP_PT)
    def _():
        issue(h2p_ref)

    @pl.when(i == DISP_PT)
    def _():
        issue(h2s_ref)

    def wait_group(gi, c):
        for _ in range(SUBLANES * TOP_K):
            row_copy(h2p_ref, 0, 0).wait()
        return c

    lax.fori_loop(0, DISP_TB // SUBLANES, wait_group, 0)


def _dispatch(zero_blk, dest_all, h2_p, h2_s):
    nt = T_ALL // DISP_TB
    return pl.pallas_call(
        _dispatch_kernel,
        grid_spec=pltpu.PrefetchScalarGridSpec(
            num_scalar_prefetch=1,
            grid=(nt,),
            in_specs=[pl.BlockSpec((1, 1, DISP_TB * TOP_K), lambda i, zb: (i, 0, 0), memory_space=pltpu.SMEM),
                      pl.BlockSpec((DISP_TB, D), lambda i, zb: (jnp.minimum(i, DISP_PT - 1), 0)),
                      pl.BlockSpec((DISP_TB, D), lambda i, zb: (0, 0))],
            out_specs=pl.BlockSpec(memory_space=pl.ANY),
            scratch_shapes=[pltpu.VMEM((MOE_BM, D), f32), pltpu.SemaphoreType.DMA(())]),
        out_shape=jax.ShapeDtypeStruct((MOE_NROWS, D), f32),
        compiler_params=_cparams(("arbitrary",)),
        name="moe_dispatch",
    )(zero_blk, dest_all.reshape(nt, 1, DISP_TB * TOP_K), h2_p, h2_s)


def _moe_ffn_kernel(e_ref, sb_ref, nb_ref, xbuf_ref, wg_ref, wu_ref, bg_ref, bu_ref, wd_ref, bd_ref,
                    ybuf_ref, x_ref, acc_ref, wgb_ref, wub_ref, wdb_ref, sem_in, sem_out):
    p = pl.program_id(0)
    j = pl.program_id(1)
    nb = nb_ref[p]
    sb = sb_ref[p]

    def in_copy(r):
        return pltpu.make_async_copy(xbuf_ref.at[pl.ds((sb + r) * MOE_BM, MOE_BM)],
                                     x_ref.at[pl.ds(r * MOE_BM, MOE_BM)], sem_in)

    def out_copy(r):
        return pltpu.make_async_copy(acc_ref.at[pl.ds(r * MOE_BM, MOE_BM)],
                                     ybuf_ref.at[pl.ds((sb + r) * MOE_BM, MOE_BM)], sem_out)

    def for_blocks(fn):
        def body(r, c):
            fn(r)
            return c
        lax.fori_loop(0, nb, body, 0)

    @pl.when((p == 0) & (j == 0))
    def _():
        used = nb_ref[MOE_PMAX]
        acc_ref[0:MOE_BM, :] = jnp.zeros((MOE_BM, D), f32)

        def tail_copy(b):
            return pltpu.make_async_copy(acc_ref.at[pl.ds(0, MOE_BM)],
                                         ybuf_ref.at[pl.ds(b * MOE_BM, MOE_BM)], sem_out)

        def tstart(b, c):
            tail_copy(b).start()
            return c

        def twait(b, c):
            tail_copy(0).wait()
            return c

        lax.fori_loop(used, MOE_NBLK, tstart, 0)
        lax.fori_loop(used, MOE_NBLK, twait, 0)

    @pl.when(nb > 0)
    def _():
        @pl.when(j == 0)
        def _():
            for_blocks(lambda r: in_copy(r).start())
            bd = jnp.broadcast_to(bd_ref[...], (MOE_BM, D))

            def init(r):
                acc_ref[pl.ds(pl.multiple_of(r * MOE_BM, MOE_BM), MOE_BM), :] = bd
            for_blocks(init)
            for_blocks(lambda r: in_copy(r).wait())

        wgb_ref[...] = wg_ref[...].astype(bf16)
        wub_ref[...] = wu_ref[...].astype(bf16)
        wdb_ref[...] = wd_ref[...].astype(bf16)
        bg = bg_ref[...]
        bu = bu_ref[...]

        def ffn_rows(row0, n_rows):
            rows = pl.ds(row0, n_rows)
            x = x_ref[rows, :].astype(bf16)
            gate = jnp.dot(x, wgb_ref[...], preferred_element_type=f32) + bg
            up = jnp.dot(x, wub_ref[...], preferred_element_type=f32) + bu
            gate = jnp.minimum(gate, SWIGLU_LIMIT)
            up = jnp.clip(up, -SWIGLU_LIMIT, SWIGLU_LIMIT)
            act = (up + 1.0) * (gate * jax.nn.sigmoid(gate * SWIGLU_ALPHA))
            acc_ref[rows, :] += jnp.dot(act.astype(bf16), wdb_ref[...], preferred_element_type=f32)

        def pair(q, c):
            ffn_rows(pl.multiple_of(q * (2 * MOE_BM), 2 * MOE_BM), 2 * MOE_BM)
            return c
        lax.fori_loop(0, lax.shift_right_logical(nb, 1), pair, 0)

        @pl.when((nb & 1) == 1)
        def _():
            ffn_rows(pl.multiple_of((nb - 1) * MOE_BM, MOE_BM), MOE_BM)

        @pl.when(j == MOE_J - 1)
        def _():
            for_blocks(lambda r: out_copy(r).start())
            for_blocks(lambda r: out_copy(r).wait())


def _moe_ffn(e_arr, sb_arr, nb_arr, xbuf, w_gate_up, b_gate_up, w_down, b_down):
    def jj(p, j, nb):
        return jnp.where(nb[p] > 0, j, MOE_J - 1)

    in_specs = [
        pl.BlockSpec(memory_space=pl.ANY),
        pl.BlockSpec((None, D, MOE_TF), lambda p, j, e, sb, nb: (e[p], 0, jj(p, j, nb))),
        pl.BlockSpec((None, D, MOE_TF), lambda p, j, e, sb, nb: (e[p], 0, MOE_J + jj(p, j, nb))),
        pl.BlockSpec((None, 1, MOE_TF), lambda p, j, e, sb, nb: (e[p], 0, jj(p, j, nb))),
        pl.BlockSpec((None, 1, MOE_TF), lambda p, j, e, sb, nb: (e[p], 0, MOE_J + jj(p, j, nb))),
        pl.BlockSpec((None, MOE_TF, D), lambda p, j, e, sb, nb: (e[p], jj(p, j, nb), 0)),
        pl.BlockSpec((None, 1, D), lambda p, j, e, sb, nb: (e[p], 0, 0)),
    ]
    return pl.pallas_call(
        _moe_ffn_kernel,
        grid_spec=pltpu.PrefetchScalarGridSpec(
            num_scalar_prefetch=3,
            grid=(MOE_PMAX, MOE_J),
            in_specs=in_specs,
            out_specs=pl.BlockSpec(memory_space=pl.ANY),
            scratch_shapes=[pltpu.VMEM((MOE_RB * MOE_BM, D), f32),
                            pltpu.VMEM((MOE_RB * MOE_BM, D), f32),
                            pltpu.VMEM((D, MOE_TF), bf16),
                            pltpu.VMEM((D, MOE_TF), bf16),
                            pltpu.VMEM((MOE_TF, D), bf16),
                            pltpu.SemaphoreType.DMA(()),
                            pltpu.SemaphoreType.DMA(())]),
        out_shape=jax.ShapeDtypeStruct((MOE_NROWS, D), f32),
        compiler_params=_cparams(("arbitrary", "arbitrary"), vmem=56 * 1024 * 1024),
        name="moe_ffn",
    )(e_arr, sb_arr, nb_arr, xbuf, w_gate_up, w_gate_up,
      b_gate_up.reshape(N_EXPERTS, 1, 2 * D_EXPERT), b_gate_up.reshape(N_EXPERTS, 1, 2 * D_EXPERT),
      w_down, b_down.reshape(N_EXPERTS, 1, D))


COMB_TB = 128


def _combine_kernel(dest_ref, ybuf_ref, gate_ref, x1_ref, g2_ref, fw_ref, o_ref, rows_ref, sem):
    def row_copy(n, d):
        return pltpu.make_async_copy(ybuf_ref.at[pl.ds(d, 1)], rows_ref.at[pl.ds(n, 1)], sem)

    def start_group(gi, c):
        for tt in range(SUBLANES):
            t = pl.multiple_of(gi * SUBLANES, SUBLANES) + tt
            for k in range(TOP_K):
                row_copy(k * COMB_TB + t, dest_ref[0, 0, (gi * SUBLANES + tt) * TOP_K + k]).start()
        return c

    def wait_group(gi, c):
        for _ in range(SUBLANES * TOP_K):
            row_copy(0, 0).wait()
        return c

    lax.fori_loop(0, COMB_TB // SUBLANES, start_group, 0)
    lax.fori_loop(0, COMB_TB // SUBLANES, wait_group, 0)

    gates = gate_ref[...]
    f = gates[:, 0:1] * rows_ref[0:COMB_TB, :]
    for k in range(1, TOP_K):
        f = f + gates[:, k:k + 1] * rows_ref[k * COMB_TB:(k + 1) * COMB_TB, :]
    x2 = x1_ref[...] + g2_ref[...] * f
    o_ref[...] = x2 * lax.rsqrt(jnp.mean(x2 * x2, axis=-1, keepdims=True) + EPS) * fw_ref[...]


def _combine(dest, ybuf, gates, x1, mod, mod_is_rows, per_b, final_w):
    mm = x1.shape[0]
    nt = mm // COMB_TB
    if mod_is_rows:
        g2spec = pl.BlockSpec((COMB_TB, D), lambda i: (i, MOD_G2))
    else:
        g2spec = pl.BlockSpec((None, 1, D), lambda i: (i // per_b, 0, MOD_G2))
    return pl.pallas_call(
        _combine_kernel,
        grid=(nt,),
        in_specs=[pl.BlockSpec((1, 1, COMB_TB * TOP_K), lambda i: (i, 0, 0), memory_space=pltpu.SMEM),
                  pl.BlockSpec(memory_space=pl.ANY),
                  pl.BlockSpec((COMB_TB, TOP_K), lambda i: (i, 0)),
                  pl.BlockSpec((COMB_TB, D), lambda i: (i, 0)),
                  g2spec,
                  pl.BlockSpec((1, D), lambda i: (0, 0))],
        out_specs=pl.BlockSpec((COMB_TB, D), lambda i: (i, 0)),
        out_shape=jax.ShapeDtypeStruct((mm, D), f32),
        scratch_shapes=[pltpu.VMEM((TOP_K * COMB_TB, D), f32), pltpu.SemaphoreType.DMA(())],
        compiler_params=_cparams(("arbitrary",)),
        name="moe_combine",
    )(dest.reshape(nt, 1, COMB_TB * TOP_K), ybuf, gates, x1, mod, final_w.reshape(1, D))


def _routing_tables(cnt_p, cnt_s, idx_p, rank_p, idx_s, rank_s):
    cp = cnt_p[0, :N_EXPERTS].astype(i32)
    cs = cnt_s[0, :N_EXPERTS].astype(i32)
    nblk = (cp + cs + MOE_BM - 1) // MOE_BM
    blk_end = jnp.cumsum(nblk)
    blk_start = blk_end - nblk
    row_start = blk_start * MOE_BM
    dest_p = row_start[idx_p] + rank_p
    dest_s = (row_start + cp)[idx_s] + rank_s
    used = blk_end[-1:]
    zero_blk = jnp.concatenate([jnp.maximum(blk_end - 1, 0), used]).astype(i32)
    npass = (nblk + MOE_RB - 1) // MOE_RB
    pass_end = jnp.cumsum(npass)
    total = pass_end[-1]
    pid = jnp.arange(MOE_PMAX, dtype=i32)
    pid_c = jnp.minimum(pid, total - 1)
    pe = jnp.minimum(jnp.sum((pass_end[None, :] <= pid_c[:, None]).astype(i32), axis=1), N_EXPERTS - 1)
    local = pid_c - (pass_end - npass)[pe]
    active = pid < total
    sb = jnp.where(active, blk_start[pe] + local * MOE_RB, 0).astype(i32)
    nb = jnp.where(active, jnp.clip(nblk[pe] - local * MOE_RB, 0, MOE_RB), 0)
    nb = jnp.concatenate([nb, used]).astype(i32)
    return dest_p.astype(i32), dest_s.astype(i32), zero_blk, pe, sb, nb


def kernel(x_prompt, x_sample, c_prompt, c_sample, state_ret, norm1_w, norm2_w, w_ada, b_ada, w_in,
           ret_gn_w, gm_ln_w, gm_ln_b, gm_ws, gm_bs, w_oa, w_ob, w_out, w_router, b_router,
           w_gate_up, b_gate_up, w_down, b_down, final_norm_w):
    x_p = x_prompt.reshape(T_P, D)
    x_s = x_sample.reshape(T_S, D)
    per_b = lambda tm: SEQ // tm

    mod = _ada(jnp.concatenate([c_prompt, c_sample], axis=0), w_ada[0], b_ada[0])
    mod_p = mod[:BATCH].reshape(BATCH, 1, 6 * D)
    mod_s = mod[BATCH:]

    n1 = norm1_w[0].reshape(1, D)
    h_all = _norm1(x_p, x_s, n1, mod_p, mod_s)
    z_all = _proj_in(h_all, w_in[0])

    a_p, s_p = _ret_prompt(z_all, ret_gn_w[0])
    z_s = z_all[T_P:]
    qt = z_s[:, :HEADS * DK].reshape(T_S, HEADS, DK).transpose(0, 2, 1)
    kt = z_s[:, HEADS * DK:2 * HEADS * DK].reshape(T_S, HEADS, DK).transpose(0, 2, 1)
    v3 = z_s[:, 2 * HEADS * DK:2 * HEADS * DK + HEADS * DV].reshape(T_S, HEADS, DV)
    g3 = z_s[:, 2 * HEADS * DK + HEADS * DV:2 * HEADS * DK + 2 * HEADS * DV].reshape(T_S, HEADS, DV)
    a_s, s_s = _ret_sample(qt, kt, v3, g3, ret_gn_w[0], state_ret[0])
    bm_p = _gmlp_prompt(z_all, gm_ws[0], gm_bs[0], gm_ln_w[0], gm_ln_b[0])
    vn_s, bm_s = _gmlp_sample(z_all, gm_ws[0], gm_bs[0], gm_ln_w[0], gm_ln_b[0])

    woa, wob, wout = w_oa[0].astype(bf16), w_ob[0].astype(bf16), w_out[0].astype(bf16)
    m_p = _merge(a_p, bm_p, z_all, woa, wob, 256, 0)
    m_s = _merge(a_s.reshape(T_S, HEADS * DV), bm_s, z_all, woa, wob, T_S, T_P // T_S)

    n2 = norm2_w[0].reshape(1, D)
    wr_hi = w_router[0].astype(bf16)
    wr_lo = (w_router[0] - wr_hi.astype(f32)).astype(bf16)
    lane_pad = ((0, 0), (0, LANES - N_EXPERTS))
    wr_cat = jnp.concatenate([jnp.pad(wr_hi, lane_pad), jnp.pad(wr_lo, lane_pad)], axis=1)
    br = jnp.pad(b_router[0].reshape(1, N_EXPERTS), lane_pad)
    x1_p, h2_p, idx_p, gate_p, rank_p, cnt_p = _post(
        m_p, x_p, mod_p, False, n2, wout, wr_cat, br, 512, per_b(512))
    x1_s, h2_s, idx_s, gate_s, rank_s, cnt_s = _post(
        m_s, x_s, mod_s, True, n2, wout, wr_cat, br, T_S, 1)

    dest_p, dest_s, zero_blk, pe, sb, nb = _routing_tables(cnt_p, cnt_s, idx_p, rank_p, idx_s, rank_s)
    xbuf = _dispatch(zero_blk, jnp.concatenate([dest_p, dest_s], axis=0), h2_p, h2_s)
    ybuf = _moe_ffn(pe, sb, nb, xbuf, w_gate_up[0], b_gate_up[0], w_down[0], b_down[0])
    y_p = _combine(dest_p, ybuf, gate_p, x1_p, mod_p, False, per_b(COMB_TB), final_norm_w)
    y_s = _combine(dest_s, ybuf, gate_s, x1_s, mod_s, True, 1, final_norm_w)

    return (y_p.reshape(BATCH, SEQ, D),
            y_s.reshape(DEC_BATCH, 1, D),
            s_p.reshape(1, BATCH, HEADS, DK, DV),
            s_s.reshape(1, DEC_BATCH, HEADS, DK, DV),
            vn_s.reshape(1, DEC_BATCH, 1, GW))
```

```python
import functools

import numpy as np
import jax
import jax.numpy as jnp
from jax import lax
from jax.experimental import pallas as pl
from jax.experimental.pallas import tpu as pltpu

f32 = jnp.float32
bf16 = jnp.bfloat16
i32 = jnp.int32

D = 2048
BATCH = 4
SEQ = 2048
DEC_BATCH = 128
PAST_LEN = 16384
HEADS = 8
DK = 128
DV = 256
CHUNK = 128
ROPE_THETA = 10000.0
GROUPS = 8
GW = 2048
GC = GW // GROUPS
N_EXPERTS = 32
TOP_K = 4
D_EXPERT = 2048
SWIGLU_LIMIT = 7.0
SWIGLU_ALPHA = 1.702
EPS = 1e-6

T_P = BATCH * SEQ
T_S = DEC_BATCH
T_ALL = T_P + T_S
D_IN = 14336
COL_Q, COL_K = 0, 1
COL_V, COL_G, COL_U, COL_GV, COL_GA, COL_GB = 1, 2, 3, 4, 5, 6
MOD_SH1, MOD_SC1, MOD_G1, MOD_SH2, MOD_SC2, MOD_G2 = range(6)

LANES = 128
SUBLANES = 8
VMEM_LIMIT = 48 * 1024 * 1024

MOE_BM = 128
MOE_RB = 12
MOE_TF = 256
MOE_J = D_EXPERT // MOE_TF
TK_ALL = T_ALL * TOP_K
MOE_NBLK = (TK_ALL + N_EXPERTS * (MOE_BM - 1)) // MOE_BM
MOE_NROWS = MOE_NBLK * MOE_BM
MOE_PMAX = N_EXPERTS + MOE_NBLK // MOE_RB


def _cparams(sem, vmem=VMEM_LIMIT):
    return pltpu.CompilerParams(dimension_semantics=sem, vmem_limit_bytes=vmem)


def _ada_kernel(c_ref, w_ref, b_ref, o_ref):
    o_ref[...] = jnp.dot(c_ref[...].astype(bf16), w_ref[...].astype(bf16),
                         preferred_element_type=f32) + b_ref[...]


def _ada(c_all, w_ada, b_ada):
    m = c_all.shape[0]
    tn = 1024
    return pl.pallas_call(
        _ada_kernel,
        grid=(6 * D // tn,),
        in_specs=[pl.BlockSpec((m, D), lambda j: (0, 0)),
                  pl.BlockSpec((D, tn), lambda j: (0, j)),
                  pl.BlockSpec((1, tn), lambda j: (0, j))],
        out_specs=pl.BlockSpec((m, tn), lambda j: (0, j)),
        out_shape=jax.ShapeDtypeStruct((m, 6 * D), f32),
        compiler_params=_cparams(("arbitrary",)),
        name="ada_mod",
    )(c_all, w_ada, b_ada.reshape(1, 6 * D))


def _norm_mod_kernel(x_ref, nw_ref, sc_ref, sh_ref, o_ref):
    x = x_ref[...]
    y = x * lax.rsqrt(jnp.mean(x * x, axis=-1, keepdims=True) + EPS) * nw_ref[...]
    o_ref[...] = (y * (1.0 + sc_ref[...]) + sh_ref[...]).astype(o_ref.dtype)


NORM1_TM = T_S
NORM1_PT = T_P // NORM1_TM


def _norm1_kernel(xp_ref, xs_ref, nw_ref, scp_ref, shp_ref, scs_ref, shs_ref, o_ref):
    i = pl.program_id(0)

    @pl.when(i < NORM1_PT)
    def _():
        _norm_mod_kernel(xp_ref, nw_ref, scp_ref, shp_ref, o_ref)

    @pl.when(i == NORM1_PT)
    def _():
        _norm_mod_kernel(xs_ref, nw_ref, scs_ref, shs_ref, o_ref)


def _norm1(x_p, x_s, nw, mod_p, mod_s):
    tm = NORM1_TM
    per_b = SEQ // tm
    pt = lambda i: jnp.minimum(i, NORM1_PT - 1)
    return pl.pallas_call(
        _norm1_kernel,
        grid=(NORM1_PT + 1,),
        in_specs=[pl.BlockSpec((tm, D), lambda i: (pt(i), 0)),
                  pl.BlockSpec((tm, D), lambda i: (0, 0)),
                  pl.BlockSpec((1, D), lambda i: (0, 0)),
                  pl.BlockSpec((None, 1, D), lambda i: (pt(i) // per_b, 0, MOD_SC1)),
                  pl.BlockSpec((None, 1, D), lambda i: (pt(i) // per_b, 0, MOD_SH1)),
                  pl.BlockSpec((tm, D), lambda i: (0, MOD_SC1)),
                  pl.BlockSpec((tm, D), lambda i: (0, MOD_SH1))],
        out_specs=pl.BlockSpec((tm, D), lambda i: (i, 0)),
        out_shape=jax.ShapeDtypeStruct((T_ALL, D), bf16),
        compiler_params=_cparams(("arbitrary",)),
        name="norm1",
    )(x_p, x_s, nw, mod_p, mod_p, mod_s, mod_s)


def _proj_in_kernel(h_ref, w_ref, o_ref, wb_ref):
    @pl.when(pl.program_id(1) == 0)
    def _():
        wb_ref[...] = w_ref[...].astype(bf16)

    o_ref[...] = jnp.dot(h_ref[...], wb_ref[...], preferred_element_type=f32)


def _proj_in(h_all, w_in):
    tm, tn = 640, 1024
    return pl.pallas_call(
        _proj_in_kernel,
        grid=(D_IN // tn, T_ALL // tm),
        in_specs=[pl.BlockSpec((tm, D), lambda j, i: (i, 0)),
                  pl.BlockSpec((D, tn), lambda j, i: (0, j))],
        out_specs=pl.BlockSpec((tm, tn), lambda j, i: (i, j)),
        out_shape=jax.ShapeDtypeStruct((T_ALL, D_IN), f32),
        scratch_shapes=[pltpu.VMEM((D, tn), bf16)],
        compiler_params=_cparams(("arbitrary", "arbitrary")),
        name="proj_in",
    )(h_all, w_in)


def _retention_consts():
    h = np.arange(HEADS, dtype=np.float64)
    lg = np.log1p(-np.exp2(-5.0 - h))
    idx = np.arange(CHUNK, dtype=np.float64)
    diff = idx[:, None] - idx[None, :]
    intra = np.where(diff >= 0, np.exp(np.maximum(diff, 0.0)[None] * lg[:, None, None]), 0.0)
    q_dec = np.exp((idx + 1.0)[:, None] * lg[None, :])
    k_dec = np.exp((CHUNK - 1.0 - idx)[:, None] * lg[None, :])
    s_dec = np.exp(CHUNK * lg)
    gamma = np.exp(lg)
    return (intra.astype(np.float32), q_dec.astype(np.float32), k_dec.astype(np.float32),
            [float(v) for v in s_dec], [float(v) for v in gamma])


def _rope_tables(pos):
    half = DK // 2
    freq = ROPE_THETA ** (-np.arange(half, dtype=np.float64) / half)
    ang = np.asarray(pos, dtype=np.float64)[:, None] * freq[None, :]
    cos, sin = np.cos(ang), np.sin(ang)
    cos_t = np.concatenate([cos, cos], axis=-1).astype(np.float32)
    sin_t = np.concatenate([-sin, sin], axis=-1).astype(np.float32)
    return cos_t, sin_t


def _group_norm_gate(o, gnw, g):
    mu = jnp.mean(o, axis=-1, keepdims=True)
    var = jnp.mean(jnp.square(o - mu), axis=-1, keepdims=True)
    on = (o - mu) * lax.rsqrt(var + EPS) * gnw
    return jax.nn.silu(g) * on


def _ret_prompt_kernel(s_dec, q_ref, k_ref, v_ref, g_ref, cos_ref, sin_ref, intra_ref, qd_ref, kd_ref,
                       gnw_ref, a_ref, s_ref):
    n = pl.program_id(1)

    @pl.when(n == 0)
    def _():
        s_ref[...] = jnp.zeros_like(s_ref)

    cos = cos_ref[...]
    sin = sin_ref[...]
    qd = qd_ref[...]
    kd = kd_ref[...]
    for h in range(HEADS):
        qh = q_ref[:, h * DK:(h + 1) * DK]
        kh = k_ref[:, h * DK:(h + 1) * DK]
        qh = (qh * cos + pltpu.roll(qh, DK // 2, 1) * sin) * (DK ** -0.5)
        kh = kh * cos + pltpu.roll(kh, DK // 2, 1) * sin
        vh = v_ref[:, h * DV:(h + 1) * DV].astype(bf16)
        s_old = s_ref[h]
        scores = lax.dot_general(qh.astype(bf16), kh.astype(bf16), (((1,), (1,)), ((), ())),
                                 preferred_element_type=f32) * intra_ref[h]
        o = jnp.dot(scores.astype(bf16), vh, preferred_element_type=f32)
        o = o + jnp.dot((qh * qd[:, h:h + 1]).astype(bf16), s_old.astype(bf16), preferred_element_type=f32)
        kw_t = (kh * kd[:, h:h + 1]).T.astype(bf16)
        s_ref[h] = s_old * s_dec[h] + jnp.dot(kw_t, vh, preferred_element_type=f32)
        a_ref[:, h * DV:(h + 1) * DV] = _group_norm_gate(
            o, gnw_ref[:, h * DV:(h + 1) * DV], g_ref[:, h * DV:(h + 1) * DV]).astype(a_ref.dtype)


def _ret_prompt(z_all, gn_w):
    intra, q_dec, k_dec, s_dec, _ = _retention_consts()
    cos_t, sin_t = _rope_tables(np.arange(SEQ))
    nch = SEQ // CHUNK
    row = lambda b, n: b * nch + n
    return pl.pallas_call(
        functools.partial(_ret_prompt_kernel, s_dec),
        grid=(BATCH, nch),
        in_specs=[pl.BlockSpec((CHUNK, HEADS * DK), lambda b, n: (row(b, n), COL_Q)),
                  pl.BlockSpec((CHUNK, HEADS * DK), lambda b, n: (row(b, n), COL_K)),
                  pl.BlockSpec((CHUNK, HEADS * DV), lambda b, n: (row(b, n), COL_V)),
                  pl.BlockSpec((CHUNK, HEADS * DV), lambda b, n: (row(b, n), COL_G)),
                  pl.BlockSpec((CHUNK, DK), lambda b, n: (n, 0)),
                  pl.BlockSpec((CHUNK, DK), lambda b, n: (n, 0)),
                  pl.BlockSpec((HEADS, CHUNK, CHUNK), lambda b, n: (0, 0, 0)),
                  pl.BlockSpec((CHUNK, HEADS), lambda b, n: (0, 0)),
                  pl.BlockSpec((CHUNK, HEADS), lambda b, n: (0, 0)),
                  pl.BlockSpec((1, HEADS * DV), lambda b, n: (0, 0))],
        out_specs=[pl.BlockSpec((CHUNK, HEADS * DV), lambda b, n: (row(b, n), 0)),
                   pl.BlockSpec((None, HEADS, DK, DV), lambda b, n: (b, 0, 0, 0))],
        out_shape=[jax.ShapeDtypeStruct((T_P, HEADS * DV), bf16),
                   jax.ShapeDtypeStruct((BATCH, HEADS, DK, DV), f32)],
        compiler_params=_cparams(("arbitrary", "arbitrary")),
        name="retention_prompt",
    )(z_all, z_all, z_all, z_all, jnp.asarray(cos_t), jnp.asarray(sin_t), jnp.asarray(intra),
      jnp.asarray(q_dec), jnp.asarray(k_dec), gn_w.reshape(1, HEADS * DV))


RET_S_BT = 4


def _ret_sample_kernel(gamma, qt_ref, kt_ref, v_ref, g_ref, cos_ref, sin_ref, gnw_ref, s_in_ref,
                       a_ref, s_out_ref):
    cos = cos_ref[...]
    sin = sin_ref[...]

    def rope_t(x):
        rolled = jnp.concatenate([x[DK // 2:], x[:DK // 2]], axis=0)
        return x * cos + rolled * sin

    def body(t, carry):
        qt = rope_t(qt_ref[t]) * (DK ** -0.5)
        kt = rope_t(kt_ref[t])
        v8 = v_ref[t]
        g8 = g_ref[t]
        for h in range(HEADS):
            s_new = s_in_ref[t, h] * gamma[h] + kt[:, h:h + 1] * v8[h:h + 1, :]
            s_out_ref[t, h] = s_new
            o = jnp.sum(qt[:, h:h + 1] * s_new, axis=0, keepdims=True)
            a_ref[t, h:h + 1, :] = _group_norm_gate(o, gnw_ref[h:h + 1, :], g8[h:h + 1, :])
        return carry

    lax.fori_loop(0, RET_S_BT, body, 0)


def _ret_sample(qt, kt, v3, g3, gn_w, state):
    _, _, _, _, gamma = _retention_consts()
    cos_t, sin_t = _rope_tables(np.array([PAST_LEN]))
    cos8 = np.ascontiguousarray(np.broadcast_to(cos_t[0][:, None], (DK, HEADS)))
    sin8 = np.ascontiguousarray(np.broadcast_to(sin_t[0][:, None], (DK, HEADS)))
    bt = RET_S_BT
    return pl.pallas_call(
        functools.partial(_ret_sample_kernel, gamma),
        grid=(T_S // bt,),
        in_specs=[pl.BlockSpec((bt, DK, HEADS), lambda i: (i, 0, 0)),
                  pl.BlockSpec((bt, DK, HEADS), lambda i: (i, 0, 0)),
                  pl.BlockSpec((bt, HEADS, DV), lambda i: (i, 0, 0)),
                  pl.BlockSpec((bt, HEADS, DV), lambda i: (i, 0, 0)),
                  pl.BlockSpec((DK, HEADS), lambda i: (0, 0)),
                  pl.BlockSpec((DK, HEADS), lambda i: (0, 0)),
                  pl.BlockSpec((HEADS, DV), lambda i: (0, 0)),
                  pl.BlockSpec((bt, HEADS, DK, DV), lambda i: (i, 0, 0, 0))],
        out_specs=[pl.BlockSpec((bt, HEADS, DV), lambda i: (i, 0, 0)),
                   pl.BlockSpec((bt, HEADS, DK, DV), lambda i: (i, 0, 0, 0))],
        out_shape=[jax.ShapeDtypeStruct((T_S, HEADS, DV), f32),
                   jax.ShapeDtypeStruct((T_S, HEADS, DK, DV), f32)],
        compiler_params=_cparams(("arbitrary",)),
        name="retention_sample",
    )(qt, kt, v3, g3, jnp.asarray(cos8), jnp.asarray(sin8), gn_w.reshape(HEADS, DV), state)


def _layer_norm(x, w, b):
    mu = jnp.mean(x, axis=-1, keepdims=True)
    var = jnp.mean(jnp.square(x - mu), axis=-1, keepdims=True)
    return (x - mu) * lax.rsqrt(var + EPS) * w + b


def _gmlp_prompt_kernel(u_ref, gv_ref, ws_ref, bst_ref, lnw_ref, lnb_ref, o_ref):
    vn = _layer_norm(jax.nn.gelu(gv_ref[...]), lnw_ref[...], lnb_ref[...]).astype(bf16)
    r = lax.broadcasted_iota(i32, (CHUNK, CHUNK), 0)
    c = lax.broadcasted_iota(i32, (CHUNK, CHUNK), 1)
    causal = r >= c
    bst = bst_ref[...]
    for g in range(GROUPS):
        w = jnp.where(causal, ws_ref[g], 0.0).astype(bf16)
        mixed = jnp.dot(w, vn[:, g * GC:(g + 1) * GC], preferred_element_type=f32) + bst[:, g:g + 1]
        o_ref[:, g * GC:(g + 1) * GC] = (jax.nn.gelu(u_ref[:, g * GC:(g + 1) * GC]) * mixed).astype(o_ref.dtype)


def _gmlp_prompt(z_all, gm_ws, gm_bs, ln_w, ln_b):
    return pl.pallas_call(
        _gmlp_prompt_kernel,
        grid=(T_P // CHUNK,),
        in_specs=[pl.BlockSpec((CHUNK, GW), lambda i: (i, COL_U)),
                  pl.BlockSpec((CHUNK, GW), lambda i: (i, COL_GV)),
                  pl.BlockSpec((GROUPS, CHUNK, CHUNK), lambda i: (0, 0, 0)),
                  pl.BlockSpec((CHUNK, GROUPS), lambda i: (0, 0)),
                  pl.BlockSpec((1, GW), lambda i: (0, 0)),
                  pl.BlockSpec((1, GW), lambda i: (0, 0))],
        out_specs=pl.BlockSpec((CHUNK, GW), lambda i: (i, 0)),
        out_shape=jax.ShapeDtypeStruct((T_P, GW), bf16),
        compiler_params=_cparams(("arbitrary",)),
        name="gmlp_prompt",
    )(z_all, z_all, gm_ws, gm_bs.T, ln_w.reshape(1, GW), ln_b.reshape(1, GW))


def _gmlp_sample_kernel(u_ref, gv_ref, w0_ref, b0_ref, lnw_ref, lnb_ref, vn_ref, o_ref):
    vn = _layer_norm(jax.nn.gelu(gv_ref[...]), lnw_ref[...], lnb_ref[...])
    vn_ref[...] = vn
    o_ref[...] = (jax.nn.gelu(u_ref[...]) * (vn * w0_ref[...] + b0_ref[...])).astype(o_ref.dtype)


def _gmlp_sample(z_all, gm_ws, gm_bs, ln_w, ln_b):
    w0 = jnp.repeat(gm_ws[:, 0, 0], GC).reshape(1, GW)
    b0 = jnp.repeat(gm_bs[:, 0], GC).reshape(1, GW)
    blk = T_P // T_S
    return pl.pallas_call(
        _gmlp_sample_kernel,
        grid=(1,),
        in_specs=[pl.BlockSpec((T_S, GW), lambda i: (blk, COL_U)),
                  pl.BlockSpec((T_S, GW), lambda i: (blk, COL_GV)),
                  pl.BlockSpec((1, GW), lambda i: (0, 0)),
                  pl.BlockSpec((1, GW), lambda i: (0, 0)),
                  pl.BlockSpec((1, GW), lambda i: (0, 0)),
                  pl.BlockSpec((1, GW), lambda i: (0, 0))],
        out_specs=[pl.BlockSpec((T_S, GW), lambda i: (0, 0)),
                   pl.BlockSpec((T_S, GW), lambda i: (0, 0))],
        out_shape=[jax.ShapeDtypeStruct((T_S, GW), f32),
                   jax.ShapeDtypeStruct((T_S, GW), bf16)],
        compiler_params=_cparams(("arbitrary",)),
        name="gmlp_sample",
    )(z_all, z_all, w0, b0, ln_w.reshape(1, GW), ln_b.reshape(1, GW))


def _merge_kernel(a_ref, b_ref, ga_ref, gb_ref, woa_ref, wob_ref, o_ref):
    ya = jnp.dot(a_ref[...].astype(bf16), woa_ref[...], preferred_element_type=f32)
    yb = jnp.dot(b_ref[...].astype(bf16), wob_ref[...], preferred_element_type=f32)
    o_ref[...] = (jax.nn.sigmoid(ga_ref[...]) * ya + jax.nn.sigmoid(gb_ref[...]) * yb).astype(o_ref.dtype)


def _merge(a, bm, z_all, w_oa, w_ob, tm, z_blk0):
    m = a.shape[0]
    resident = lambda: pl.BlockSpec((D, D), lambda i: (0, 0), pipeline_mode=pl.Buffered(1))
    return pl.pallas_call(
        _merge_kernel,
        grid=(m // tm,),
        in_specs=[pl.BlockSpec((tm, D), lambda i: (i, 0)),
                  pl.BlockSpec((tm, D), lambda i: (i, 0)),
                  pl.BlockSpec((tm, D), lambda i: (z_blk0 + i, COL_GA)),
                  pl.BlockSpec((tm, D), lambda i: (z_blk0 + i, COL_GB)),
                  resident(), resident()],
        out_specs=pl.BlockSpec((tm, D), lambda i: (i, 0)),
        out_shape=jax.ShapeDtypeStruct((m, D), bf16),
        compiler_params=_cparams(("arbitrary",)),
        name="merge",
    )(a, bm, z_all, z_all, w_oa, w_ob)


def _post_kernel(m_ref, x_ref, g1_ref, sc_ref, sh_ref, nw_ref, wout_ref, wr_ref, br_ref,
                 x1_ref, h2_ref, idx_ref, gate_ref, rank_ref, cnt_ref):
    i = pl.program_id(0)

    @pl.when(i == 0)
    def _():
        cnt_ref[...] = jnp.zeros_like(cnt_ref)

    tm = m_ref.shape[0]
    y = jnp.dot(m_ref[...], wout_ref[...], preferred_element_type=f32)
    x1 = x_ref[...] + g1_ref[...] * y
    x1_ref[...] = x1
    xn = x1 * lax.rsqrt(jnp.mean(x1 * x1, axis=-1, keepdims=True) + EPS) * nw_ref[...]
    h2 = xn * (1.0 + sc_ref[...]) + sh_ref[...]
    h2_ref[...] = h2

    h_hi = h2.astype(bf16)
    h_lo = (h2 - h_hi.astype(f32)).astype(bf16)
    p_hi = jnp.dot(h_hi, wr_ref[...], preferred_element_type=f32)
    p_lo = jnp.dot(h_lo, wr_ref[...], preferred_element_type=f32)
    logits = p_hi[:, :LANES] + p_hi[:, LANES:] + p_lo[:, :LANES] + br_ref[...]

    lane = lax.broadcasted_iota(i32, (tm, LANES), 1).astype(f32)
    col = lax.broadcasted_iota(i32, (tm, TOP_K), 1)
    work = jnp.where(lane < N_EXPERTS, logits, -jnp.inf)
    member = jnp.zeros((tm, LANES), f32)
    vals, sels = [], []
    idx_out = jnp.zeros((tm, TOP_K), i32)
    for k in range(TOP_K):
        mx = jnp.max(work, axis=1, keepdims=True)
        ix = jnp.min(jnp.where(work == mx, lane, float(LANES)), axis=1, keepdims=True)
        sel = lane == ix
        vals.append(mx)
        sels.append(sel)
        idx_out = jnp.where(col == k, ix.astype(i32), idx_out)
        member = jnp.where(sel, 1.0, member)
        work = jnp.where(sel, -jnp.inf, work)
    idx_ref[...] = idx_out

    exps = [jnp.exp(v - vals[0]) for v in vals]
    den = exps[0] + exps[1] + exps[2] + exps[3]
    gate_out = jnp.zeros((tm, TOP_K), f32)
    for k in range(TOP_K):
        gate_out = jnp.where(col == k, exps[k] / den, gate_out)
    gate_ref[...] = gate_out

    r = lax.broadcasted_iota(i32, (tm, tm), 0)
    c = lax.broadcasted_iota(i32, (tm, tm), 1)
    lower = jnp.where(c < r, 1.0, 0.0).astype(bf16)
    before = jnp.dot(lower, member.astype(bf16), preferred_element_type=f32) + cnt_ref[...]
    rank_out = jnp.zeros((tm, TOP_K), i32)
    for k in range(TOP_K):
        rk = jnp.sum(jnp.where(sels[k], before, 0.0), axis=1, keepdims=True)
        rank_out = jnp.where(col == k, rk.astype(i32), rank_out)
    rank_ref[...] = rank_out
    cnt_ref[...] += jnp.sum(member, axis=0, keepdims=True)


def _post(m, x, mod, mod_is_rows, nw, w_out, w_router, b_router, tm, per_b):
    mm = m.shape[0]
    if mod_is_rows:
        mspec = lambda chunk: pl.BlockSpec((tm, D), lambda i: (i, chunk))
    else:
        mspec = lambda chunk: pl.BlockSpec((None, 1, D), lambda i: (i // per_b, 0, chunk))
    in_specs = [pl.BlockSpec((tm, D), lambda i: (i, 0)),
                pl.BlockSpec((tm, D), lambda i: (i, 0)),
                mspec(MOD_G1), mspec(MOD_SC2), mspec(MOD_SH2),
                pl.BlockSpec((1, D), lambda i: (0, 0)),
                pl.BlockSpec((D, D), lambda i: (0, 0), pipeline_mode=pl.Buffered(1)),
                pl.BlockSpec((D, 2 * LANES), lambda i: (0, 0)),
                pl.BlockSpec((1, LANES), lambda i: (0, 0))]
    args = [m, x, mod, mod, mod, nw, w_out, w_router, b_router]
    return pl.pallas_call(
        _post_kernel,
        grid=(mm // tm,),
        in_specs=in_specs,
        out_specs=[pl.BlockSpec((tm, D), lambda i: (i, 0)),
                   pl.BlockSpec((tm, D), lambda i: (i, 0)),
                   pl.BlockSpec((tm, TOP_K), lambda i: (i, 0)),
                   pl.BlockSpec((tm, TOP_K), lambda i: (i, 0)),
                   pl.BlockSpec((tm, TOP_K), lambda i: (i, 0)),
                   pl.BlockSpec((1, LANES), lambda i: (0, 0))],
        out_shape=[jax.ShapeDtypeStruct((mm, D), f32),
                   jax.ShapeDtypeStruct((mm, D), f32),
                   jax.ShapeDtypeStruct((mm, TOP_K), i32),
                   jax.ShapeDtypeStruct((mm, TOP_K), f32),
                   jax.ShapeDtypeStruct((mm, TOP_K), i32),
                   jax.ShapeDtypeStruct((1, LANES), f32)],
        compiler_params=_cparams(("arbitrary",)),
        name="post_mixer",
    )(*args)


DISP_TB = 128


DISP_PT = T_P // DISP_TB


def _dispatch_kernel(zblk_ref, dest_ref, h2p_ref, h2s_ref, xbuf_ref, zero_ref, sem):
    i = pl.program_id(0)

    def row_copy(src_ref, t, d):
        return pltpu.make_async_copy(src_ref.at[pl.ds(t, 1)], xbuf_ref.at[pl.ds(d, 1)], sem)

    def zero_copy(blk):
        return pltpu.make_async_copy(zero_ref, xbuf_ref.at[pl.ds(blk * MOE_BM, MOE_BM)], sem)

    @pl.when(i == 0)
    def _():
        zero_ref[...] = jnp.zeros_like(zero_ref)
        used = zblk_ref[N_EXPERTS]

        def zstart(e, c):
            zero_copy(zblk_ref[e]).start()
            return c

        def tstart(b, c):
            zero_copy(b).start()
            return c

        def zwait(e, c):
            zero_copy(0).wait()
            return c

        lax.fori_loop(0, N_EXPERTS, zstart, 0)
        lax.fori_loop(used, MOE_NBLK, tstart, 0)
        lax.fori_loop(0, N_EXPERTS, zwait, 0)
        lax.fori_loop(used, MOE_NBLK, zwait, 0)

    def issue(src_ref):
        def group(gi, c):
            for tt in range(SUBLANES):
                t = pl.multiple_of(gi * SUBLANES, SUBLANES) + tt
                for k in range(TOP_K):
                    d = dest_ref[0, 0, (gi * SUBLANES + tt) * TOP_K + k]
                    row_copy(src_ref, t, d).start(priority=k % 2)
            return c
        lax.fori_loop(0, DISP_TB // SUBLANES, group, 0)

    @pl.when(i < DISP_PT)
    def _():
        issue(h2p_ref)

    @pl.when(i == DISP_PT)
    def _():
        issue(h2s_ref)

    def wait_group(gi, c):
        for _ in range(SUBLANES * TOP_K):
            row_copy(h2p_ref, 0, 0).wait()
        return c

    lax.fori_loop(0, DISP_TB // SUBLANES, wait_group, 0)


def _dispatch(zero_blk, dest_all, h2_p, h2_s):
    nt = T_ALL // DISP_TB
    return pl.pallas_call(
        _dispatch_kernel,
        grid_spec=pltpu.PrefetchScalarGridSpec(
            num_scalar_prefetch=1,
            grid=(nt,),
            in_specs=[pl.BlockSpec((1, 1, DISP_TB * TOP_K), lambda i, zb: (i, 0, 0), memory_space=pltpu.SMEM),
                      pl.BlockSpec((DISP_TB, D), lambda i, zb: (jnp.minimum(i, DISP_PT - 1), 0)),
                      pl.BlockSpec((DISP_TB, D), lambda i, zb: (0, 0))],
            out_specs=pl.BlockSpec(memory_space=pl.ANY),
            scratch_shapes=[pltpu.VMEM((MOE_BM, D), f32), pltpu.SemaphoreType.DMA(())]),
        out_shape=jax.ShapeDtypeStruct((MOE_NROWS, D), f32),
        compiler_params=_cparams(("arbitrary",)),
        name="moe_dispatch",
    )(zero_blk, dest_all.reshape(nt, 1, DISP_TB * TOP_K), h2_p, h2_s)


def _moe_ffn_kernel(e_ref, sb_ref, nb_ref, xbuf_ref, wg_ref, wu_ref, bg_ref, bu_ref, wd_ref, bd_ref,
                    ybuf_ref, x_ref, acc_ref, wgb_ref, wub_ref, wdb_ref, sem_in, sem_out):
    p = pl.program_id(0)
    j = pl.program_id(1)
    nb = nb_ref[p]
    sb = sb_ref[p]

    def in_copy(r):
        return pltpu.make_async_copy(xbuf_ref.at[pl.ds((sb + r) * MOE_BM, MOE_BM)],
                                     x_ref.at[pl.ds(r * MOE_BM, MOE_BM)], sem_in.at[r])

    def out_copy(r):
        return pltpu.make_async_copy(acc_ref.at[pl.ds(r * MOE_BM, MOE_BM)],
                                     ybuf_ref.at[pl.ds((sb + r) * MOE_BM, MOE_BM)], sem_out)

    def for_blocks(fn):
        def body(r, c):
            fn(r)
            return c
        lax.fori_loop(0, nb, body, 0)

    @pl.when((p == 0) & (j == 0))
    def _():
        used = nb_ref[MOE_PMAX]
        acc_ref[0:MOE_BM, :] = jnp.zeros((MOE_BM, D), f32)

        def tail_copy(b):
            return pltpu.make_async_copy(acc_ref.at[pl.ds(0, MOE_BM)],
                                         ybuf_ref.at[pl.ds(b * MOE_BM, MOE_BM)], sem_out)

        def tstart(b, c):
            tail_copy(b).start()
            return c

        def twait(b, c):
            tail_copy(0).wait()
            return c

        lax.fori_loop(used, MOE_NBLK, tstart, 0)
        lax.fori_loop(used, MOE_NBLK, twait, 0)

    first = j == 0
    last = j == MOE_J - 1

    @pl.when(nb > 0)
    def _():
        @pl.when(first)
        def _():
            for_blocks(lambda r: in_copy(r).start())
            bd = jnp.broadcast_to(bd_ref[...], (MOE_BM, D))

            def init(r):
                acc_ref[pl.ds(pl.multiple_of(r * MOE_BM, MOE_BM), MOE_BM), :] = bd
            for_blocks(init)

        wgb_ref[...] = wg_ref[...].astype(bf16)
        wub_ref[...] = wu_ref[...].astype(bf16)
        wdb_ref[...] = wd_ref[...].astype(bf16)
        bg = bg_ref[...]
        bu = bu_ref[...]

        def ffn_rows(b0, n_blk):
            rows = pl.ds(pl.multiple_of(b0 * MOE_BM, MOE_BM), n_blk * MOE_BM)
            x = x_ref[rows, :].astype(bf16)
            gate = jnp.dot(x, wgb_ref[...], preferred_element_type=f32) + bg
            up = jnp.dot(x, wub_ref[...], preferred_element_type=f32) + bu
            gate = jnp.minimum(gate, SWIGLU_LIMIT)
            up = jnp.clip(up, -SWIGLU_LIMIT, SWIGLU_LIMIT)
            act = (up + 1.0) * (gate * jax.nn.sigmoid(gate * SWIGLU_ALPHA))
            acc_ref[rows, :] += jnp.dot(act.astype(bf16), wdb_ref[...], preferred_element_type=f32)

        def trip(b0, groups):
            n_blk = sum(groups)

            @pl.when(first)
            def _():
                for b in range(n_blk):
                    in_copy(b0 + b).wait()

            off = 0
            for g in groups:
                ffn_rows(b0 + off, g)
                off += g

            @pl.when(last)
            def _():
                for b in range(n_blk):
                    out_copy(b0 + b).start()

        nquad = lax.shift_right_logical(nb, 2)

        def quad(q, c):
            trip(q * 4, (2, 2))
            return c
        lax.fori_loop(0, nquad, quad, 0)
        rem = nb & 3
        for n_rem, groups in ((1, (1,)), (2, (2,)), (3, (2, 1))):
            @pl.when(rem == n_rem)
            def _(groups=groups):
                trip(nquad * 4, groups)

        @pl.when(last)
        def _():
            for_blocks(lambda r: out_copy(r).wait())


def _moe_ffn(e_arr, sb_arr, nb_arr, xbuf, w_gate_up, b_gate_up, w_down, b_down):
    def jj(p, j, nb):
        return jnp.where(nb[p] > 0, j, MOE_J - 1)

    in_specs = [
        pl.BlockSpec(memory_space=pl.ANY),
        pl.BlockSpec((None, D, MOE_TF), lambda p, j, e, sb, nb: (e[p], 0, jj(p, j, nb))),
        pl.BlockSpec((None, D, MOE_TF), lambda p, j, e, sb, nb: (e[p], 0, MOE_J + jj(p, j, nb))),
        pl.BlockSpec((None, 1, MOE_TF), lambda p, j, e, sb, nb: (e[p], 0, jj(p, j, nb))),
        pl.BlockSpec((None, 1, MOE_TF), lambda p, j, e, sb, nb: (e[p], 0, MOE_J + jj(p, j, nb))),
        pl.BlockSpec((None, MOE_TF, D), lambda p, j, e, sb, nb: (e[p], jj(p, j, nb), 0)),
        pl.BlockSpec((None, 1, D), lambda p, j, e, sb, nb: (e[p], 0, 0)),
    ]
    return pl.pallas_call(
        _moe_ffn_kernel,
        grid_spec=pltpu.PrefetchScalarGridSpec(
            num_scalar_prefetch=3,
            grid=(MOE_PMAX, MOE_J),
            in_specs=in_specs,
            out_specs=pl.BlockSpec(memory_space=pl.ANY),
            scratch_shapes=[pltpu.VMEM((MOE_RB * MOE_BM, D), f32),
                            pltpu.VMEM((MOE_RB * MOE_BM, D), f32),
                            pltpu.VMEM((D, MOE_TF), bf16),
                            pltpu.VMEM((D, MOE_TF), bf16),
                            pltpu.VMEM((MOE_TF, D), bf16),
                            pltpu.SemaphoreType.DMA((MOE_RB,)),
                            pltpu.SemaphoreType.DMA(())]),
        out_shape=jax.ShapeDtypeStruct((MOE_NROWS, D), f32),
        compiler_params=_cparams(("arbitrary", "arbitrary"), vmem=56 * 1024 * 1024),
        name="moe_ffn",
    )(e_arr, sb_arr, nb_arr, xbuf, w_gate_up, w_gate_up,
      b_gate_up.reshape(N_EXPERTS, 1, 2 * D_EXPERT), b_gate_up.reshape(N_EXPERTS, 1, 2 * D_EXPERT),
      w_down, b_down.reshape(N_EXPERTS, 1, D))


COMB_TB = 128


def _combine_kernel(dest_ref, ybuf_ref, gate_ref, x1_ref, g2_ref, fw_ref, o_ref, rows_ref, sem):
    def row_copy(n, d):
        return pltpu.make_async_copy(ybuf_ref.at[pl.ds(d, 1)], rows_ref.at[pl.ds(n, 1)], sem)

    def start_group(gi, c):
        for tt in range(SUBLANES):
            t = pl.multiple_of(gi * SUBLANES, SUBLANES) + tt
            for k in range(TOP_K):
                d = dest_ref[0, 0, (gi * SUBLANES + tt) * TOP_K + k]
                row_copy(k * COMB_TB + t, d).start(priority=k % 2)
        return c

    def wait_group(gi, c):
        for _ in range(SUBLANES * TOP_K):
            row_copy(0, 0).wait()
        return c

    lax.fori_loop(0, COMB_TB // SUBLANES, start_group, 0)
    lax.fori_loop(0, COMB_TB // SUBLANES, wait_group, 0)

    gates = gate_ref[...]
    f = gates[:, 0:1] * rows_ref[0:COMB_TB, :]
    for k in range(1, TOP_K):
        f = f + gates[:, k:k + 1] * rows_ref[k * COMB_TB:(k + 1) * COMB_TB, :]
    x2 = x1_ref[...] + g2_ref[...] * f
    o_ref[...] = x2 * lax.rsqrt(jnp.mean(x2 * x2, axis=-1, keepdims=True) + EPS) * fw_ref[...]


def _combine(dest, ybuf, gates, x1, mod, mod_is_rows, per_b, final_w):
    mm = x1.shape[0]
    nt = mm // COMB_TB
    if mod_is_rows:
        g2spec = pl.BlockSpec((COMB_TB, D), lambda i: (i, MOD_G2))
    else:
        g2spec = pl.BlockSpec((None, 1, D), lambda i: (i // per_b, 0, MOD_G2))
    return pl.pallas_call(
        _combine_kernel,
        grid=(nt,),
        in_specs=[pl.BlockSpec((1, 1, COMB_TB * TOP_K), lambda i: (i, 0, 0), memory_space=pltpu.SMEM),
                  pl.BlockSpec(memory_space=pl.ANY),
                  pl.BlockSpec((COMB_TB, TOP_K), lambda i: (i, 0)),
                  pl.BlockSpec((COMB_TB, D), lambda i: (i, 0)),
                  g2spec,
                  pl.BlockSpec((1, D), lambda i: (0, 0))],
        out_specs=pl.BlockSpec((COMB_TB, D), lambda i: (i, 0)),
        out_shape=jax.ShapeDtypeStruct((mm, D), f32),
        scratch_shapes=[pltpu.VMEM((TOP_K * COMB_TB, D), f32), pltpu.SemaphoreType.DMA(())],
        compiler_params=_cparams(("arbitrary",)),
        name="moe_combine",
    )(dest.reshape(nt, 1, COMB_TB * TOP_K), ybuf, gates, x1, mod, final_w.reshape(1, D))


def _routing_tables(cnt_p, cnt_s, idx_p, rank_p, idx_s, rank_s):
    cp = cnt_p[0, :N_EXPERTS].astype(i32)
    cs = cnt_s[0, :N_EXPERTS].astype(i32)
    nblk = (cp + cs + MOE_BM - 1) // MOE_BM
    blk_end = jnp.cumsum(nblk)
    blk_start = blk_end - nblk
    row_start = blk_start * MOE_BM
    dest_p = row_start[idx_p.reshape(-1)] + rank_p.reshape(-1)
    dest_s = (row_start + cp)[idx_s.reshape(-1)] + rank_s.reshape(-1)
    used = blk_end[-1:]
    zero_blk = jnp.concatenate([jnp.maximum(blk_end - 1, 0), used]).astype(i32)
    npass = (nblk + MOE_RB - 1) // MOE_RB
    pass_end = jnp.cumsum(npass)
    total = pass_end[-1]
    pid = jnp.arange(MOE_PMAX, dtype=i32)
    pid_c = jnp.minimum(pid, total - 1)
    pe = jnp.minimum(jnp.sum((pass_end[None, :] <= pid_c[:, None]).astype(i32), axis=1), N_EXPERTS - 1)
    local = pid_c - (pass_end - npass)[pe]
    active = pid < total
    sb = jnp.where(active, blk_start[pe] + local * MOE_RB, 0).astype(i32)
    nb = jnp.where(active, jnp.clip(nblk[pe] - local * MOE_RB, 0, MOE_RB), 0)
    nb = jnp.concatenate([nb, used]).astype(i32)
    return dest_p.astype(i32), dest_s.astype(i32), zero_blk, pe, sb, nb


def kernel(x_prompt, x_sample, c_prompt, c_sample, state_ret, norm1_w, norm2_w, w_ada, b_ada, w_in,
           ret_gn_w, gm_ln_w, gm_ln_b, gm_ws, gm_bs, w_oa, w_ob, w_out, w_router, b_router,
           w_gate_up, b_gate_up, w_down, b_down, final_norm_w):
    x_p = x_prompt.reshape(T_P, D)
    x_s = x_sample.reshape(T_S, D)
    per_b = lambda tm: SEQ // tm

    mod = _ada(jnp.concatenate([c_prompt, c_sample], axis=0), w_ada[0], b_ada[0])
    mod_p = mod[:BATCH].reshape(BATCH, 1, 6 * D)
    mod_s = mod[BATCH:]

    n1 = norm1_w[0].reshape(1, D)
    h_all = _norm1(x_p, x_s, n1, mod_p, mod_s)
    z_all = _proj_in(h_all, w_in[0])

    a_p, s_p = _ret_prompt(z_all, ret_gn_w[0])
    z_s = z_all[T_P:]
    qt = z_s[:, :HEADS * DK].reshape(T_S, HEADS, DK).transpose(0, 2, 1)
    kt = z_s[:, HEADS * DK:2 * HEADS * DK].reshape(T_S, HEADS, DK).transpose(0, 2, 1)
    v3 = z_s[:, 2 * HEADS * DK:2 * HEADS * DK + HEADS * DV].reshape(T_S, HEADS, DV)
    g3 = z_s[:, 2 * HEADS * DK + HEADS * DV:2 * HEADS * DK + 2 * HEADS * DV].reshape(T_S, HEADS, DV)
    a_s, s_s = _ret_sample(qt, kt, v3, g3, ret_gn_w[0], state_ret[0])
    bm_p = _gmlp_prompt(z_all, gm_ws[0], gm_bs[0], gm_ln_w[0], gm_ln_b[0])
    vn_s, bm_s = _gmlp_sample(z_all, gm_ws[0], gm_bs[0], gm_ln_w[0], gm_ln_b[0])

    woa, wob, wout = w_oa[0].astype(bf16), w_ob[0].astype(bf16), w_out[0].astype(bf16)
    m_p = _merge(a_p, bm_p, z_all, woa, wob, 256, 0)
    m_s = _merge(a_s.reshape(T_S, HEADS * DV), bm_s, z_all, woa, wob, T_S, T_P // T_S)

    n2 = norm2_w[0].reshape(1, D)
    wr_hi = w_router[0].astype(bf16)
    wr_lo = (w_router[0] - wr_hi.astype(f32)).astype(bf16)
    lane_pad = ((0, 0), (0, LANES - N_EXPERTS))
    wr_cat = jnp.concatenate([jnp.pad(wr_hi, lane_pad), jnp.pad(wr_lo, lane_pad)], axis=1)
    br = jnp.pad(b_router[0].reshape(1, N_EXPERTS), lane_pad)
    x1_p, h2_p, idx_p, gate_p, rank_p, cnt_p = _post(
        m_p, x_p, mod_p, False, n2, wout, wr_cat, br, 512, per_b(512))
    x1_s, h2_s, idx_s, gate_s, rank_s, cnt_s = _post(
        m_s, x_s, mod_s, True, n2, wout, wr_cat, br, T_S, 1)

    dest_p, dest_s, zero_blk, pe, sb, nb = _routing_tables(cnt_p, cnt_s, idx_p, rank_p, idx_s, rank_s)
    xbuf = _dispatch(zero_blk, jnp.concatenate([dest_p, dest_s], axis=0), h2_p, h2_s)
    ybuf = _moe_ffn(pe, sb, nb, xbuf, w_gate_up[0], b_gate_up[0], w_down[0], b_down[0])
    y_p = _combine(dest_p, ybuf, gate_p, x1_p, mod_p, False, per_b(COMB_TB), final_norm_w)
    y_s = _combine(dest_s, ybuf, gate_s, x1_s, mod_s, True, 1, final_norm_w)

    return (y_p.reshape(BATCH, SEQ, D),
            y_s.reshape(DEC_BATCH, 1, D),
            s_p.reshape(1, BATCH, HEADS, DK, DV),
            s_s.reshape(1, DEC_BATCH, HEADS, DK, DV),
            vn_s.reshape(1, DEC_BATCH, 1, GW))
```

```python
import functools

import numpy as np
import jax
import jax.numpy as jnp
from jax import lax
from jax.experimental import pallas as pl
from jax.experimental.pallas import tpu as pltpu

f32 = jnp.float32
bf16 = jnp.bfloat16
i32 = jnp.int32

D = 2048
BATCH = 4
SEQ = 2048
DEC_BATCH = 128
PAST_LEN = 16384
HEADS = 8
DK = 128
DV = 256
CHUNK = 128
ROPE_THETA = 10000.0
GROUPS = 8
GW = 2048
GC = GW // GROUPS
N_EXPERTS = 32
TOP_K = 4
D_EXPERT = 2048
SWIGLU_LIMIT = 7.0
SWIGLU_ALPHA = 1.702
EPS = 1e-6

T_P = BATCH * SEQ
T_S = DEC_BATCH
T_ALL = T_P + T_S
D_IN = 14336
COL_Q, COL_K = 0, 1
COL_V, COL_G, COL_U, COL_GV, COL_GA, COL_GB = 1, 2, 3, 4, 5, 6
MOD_SH1, MOD_SC1, MOD_G1, MOD_SH2, MOD_SC2, MOD_G2 = range(6)

LANES = 128
SUBLANES = 8
VMEM_LIMIT = 48 * 1024 * 1024

MOE_BM = 128
MOE_RB = 12
MOE_TF = 256
MOE_J = D_EXPERT // MOE_TF
TK_ALL = T_ALL * TOP_K
MOE_NBLK = (TK_ALL + N_EXPERTS * (MOE_BM - 1)) // MOE_BM
MOE_NROWS = MOE_NBLK * MOE_BM
MOE_PMAX = N_EXPERTS + MOE_NBLK // MOE_RB


def _cparams(sem, vmem=VMEM_LIMIT):
    return pltpu.CompilerParams(dimension_semantics=sem, vmem_limit_bytes=vmem)


def _ada_kernel(c_ref, w_ref, b_ref, o_ref):
    o_ref[...] = jnp.dot(c_ref[...].astype(bf16), w_ref[...].astype(bf16),
                         preferred_element_type=f32) + b_ref[...]


def _ada(c_all, w_ada, b_ada):
    m = c_all.shape[0]
    tn = 1024
    return pl.pallas_call(
        _ada_kernel,
        grid=(6 * D // tn,),
        in_specs=[pl.BlockSpec((m, D), lambda j: (0, 0)),
                  pl.BlockSpec((D, tn), lambda j: (0, j)),
                  pl.BlockSpec((1, tn), lambda j: (0, j))],
        out_specs=pl.BlockSpec((m, tn), lambda j: (0, j)),
        out_shape=jax.ShapeDtypeStruct((m, 6 * D), f32),
        compiler_params=_cparams(("arbitrary",)),
        name="ada_mod",
    )(c_all, w_ada, b_ada.reshape(1, 6 * D))


def _norm_mod_kernel(x_ref, nw_ref, sc_ref, sh_ref, o_ref):
    x = x_ref[...]
    y = x * lax.rsqrt(jnp.mean(x * x, axis=-1, keepdims=True) + EPS) * nw_ref[...]
    o_ref[...] = (y * (1.0 + sc_ref[...]) + sh_ref[...]).astype(o_ref.dtype)


NORM1_TM = T_S
NORM1_PT = T_P // NORM1_TM


def _norm1_kernel(xp_ref, xs_ref, nw_ref, scp_ref, shp_ref, scs_ref, shs_ref, o_ref):
    i = pl.program_id(0)

    @pl.when(i < NORM1_PT)
    def _():
        _norm_mod_kernel(xp_ref, nw_ref, scp_ref, shp_ref, o_ref)

    @pl.when(i == NORM1_PT)
    def _():
        _norm_mod_kernel(xs_ref, nw_ref, scs_ref, shs_ref, o_ref)


def _norm1(x_p, x_s, nw, mod_p, mod_s):
    tm = NORM1_TM
    per_b = SEQ // tm
    pt = lambda i: jnp.minimum(i, NORM1_PT - 1)
    return pl.pallas_call(
        _norm1_kernel,
        grid=(NORM1_PT + 1,),
        in_specs=[pl.BlockSpec((tm, D), lambda i: (pt(i), 0)),
                  pl.BlockSpec((tm, D), lambda i: (0, 0)),
                  pl.BlockSpec((1, D), lambda i: (0, 0)),
                  pl.BlockSpec((None, 1, D), lambda i: (pt(i) // per_b, 0, MOD_SC1)),
                  pl.BlockSpec((None, 1, D), lambda i: (pt(i) // per_b, 0, MOD_SH1)),
                  pl.BlockSpec((tm, D), lambda i: (0, MOD_SC1)),
                  pl.BlockSpec((tm, D), lambda i: (0, MOD_SH1))],
        out_specs=pl.BlockSpec((tm, D), lambda i: (i, 0)),
        out_shape=jax.ShapeDtypeStruct((T_ALL, D), bf16),
        compiler_params=_cparams(("arbitrary",)),
        name="norm1",
    )(x_p, x_s, nw, mod_p, mod_p, mod_s, mod_s)


def _proj_in_kernel(h_ref, w_ref, o_ref, wb_ref):
    @pl.when(pl.program_id(1) == 0)
    def _():
        wb_ref[...] = w_ref[...].astype(bf16)

    o_ref[...] = jnp.dot(h_ref[...], wb_ref[...], preferred_element_type=f32)


def _proj_in(h_all, w_in):
    tm, tn = 640, 1024
    return pl.pallas_call(
        _proj_in_kernel,
        grid=(D_IN // tn, T_ALL // tm),
        in_specs=[pl.BlockSpec((tm, D), lambda j, i: (i, 0)),
                  pl.BlockSpec((D, tn), lambda j, i: (0, j))],
        out_specs=pl.BlockSpec((tm, tn), lambda j, i: (i, j)),
        out_shape=jax.ShapeDtypeStruct((T_ALL, D_IN), f32),
        scratch_shapes=[pltpu.VMEM((D, tn), bf16)],
        compiler_params=_cparams(("arbitrary", "arbitrary")),
        name="proj_in",
    )(h_all, w_in)


def _retention_consts():
    h = np.arange(HEADS, dtype=np.float64)
    lg = np.log1p(-np.exp2(-5.0 - h))
    idx = np.arange(CHUNK, dtype=np.float64)
    diff = idx[:, None] - idx[None, :]
    intra = np.where(diff >= 0, np.exp(np.maximum(diff, 0.0)[None] * lg[:, None, None]), 0.0)
    q_dec = np.exp((idx + 1.0)[:, None] * lg[None, :])
    k_dec = np.exp((CHUNK - 1.0 - idx)[:, None] * lg[None, :])
    s_dec = np.exp(CHUNK * lg)
    gamma = np.exp(lg)
    return (intra.astype(np.float32), q_dec.astype(np.float32), k_dec.astype(np.float32),
            [float(v) for v in s_dec], [float(v) for v in gamma])


def _rope_tables(pos):
    half = DK // 2
    freq = ROPE_THETA ** (-np.arange(half, dtype=np.float64) / half)
    ang = np.asarray(pos, dtype=np.float64)[:, None] * freq[None, :]
    cos, sin = np.cos(ang), np.sin(ang)
    cos_t = np.concatenate([cos, cos], axis=-1).astype(np.float32)
    sin_t = np.concatenate([-sin, sin], axis=-1).astype(np.float32)
    return cos_t, sin_t


def _group_norm_gate(o, gnw, g):
    mu = jnp.mean(o, axis=-1, keepdims=True)
    var = jnp.mean(jnp.square(o - mu), axis=-1, keepdims=True)
    on = (o - mu) * lax.rsqrt(var + EPS) * gnw
    return jax.nn.silu(g) * on


def _ret_prompt_kernel(s_dec, q_ref, k_ref, v_ref, g_ref, cos_ref, sin_ref, intra_ref, qd_ref, kd_ref,
                       gnw_ref, a_ref, s_ref):
    n = pl.program_id(1)

    @pl.when(n == 0)
    def _():
        s_ref[...] = jnp.zeros_like(s_ref)

    cos = cos_ref[...]
    sin = sin_ref[...]
    qd = qd_ref[...]
    kd = kd_ref[...]
    for h in range(HEADS):
        qh = q_ref[:, h * DK:(h + 1) * DK]
        kh = k_ref[:, h * DK:(h + 1) * DK]
        qh = (qh * cos + pltpu.roll(qh, DK // 2, 1) * sin) * (DK ** -0.5)
        kh = kh * cos + pltpu.roll(kh, DK // 2, 1) * sin
        vh = v_ref[:, h * DV:(h + 1) * DV].astype(bf16)
        s_old = s_ref[h]
        scores = lax.dot_general(qh.astype(bf16), kh.astype(bf16), (((1,), (1,)), ((), ())),
                                 preferred_element_type=f32) * intra_ref[h]
        o = jnp.dot(scores.astype(bf16), vh, preferred_element_type=f32)
        o = o + jnp.dot((qh * qd[:, h:h + 1]).astype(bf16), s_old.astype(bf16), preferred_element_type=f32)
        kw_t = (kh * kd[:, h:h + 1]).T.astype(bf16)
        s_ref[h] = s_old * s_dec[h] + jnp.dot(kw_t, vh, preferred_element_type=f32)
        a_ref[:, h * DV:(h + 1) * DV] = _group_norm_gate(
            o, gnw_ref[:, h * DV:(h + 1) * DV], g_ref[:, h * DV:(h + 1) * DV]).astype(a_ref.dtype)


def _ret_prompt(z_all, gn_w):
    intra, q_dec, k_dec, s_dec, _ = _retention_consts()
    cos_t, sin_t = _rope_tables(np.arange(SEQ))
    nch = SEQ // CHUNK
    row = lambda b, n: b * nch + n
    return pl.pallas_call(
        functools.partial(_ret_prompt_kernel, s_dec),
        grid=(BATCH, nch),
        in_specs=[pl.BlockSpec((CHUNK, HEADS * DK), lambda b, n: (row(b, n), COL_Q)),
                  pl.BlockSpec((CHUNK, HEADS * DK), lambda b, n: (row(b, n), COL_K)),
                  pl.BlockSpec((CHUNK, HEADS * DV), lambda b, n: (row(b, n), COL_V)),
                  pl.BlockSpec((CHUNK, HEADS * DV), lambda b, n: (row(b, n), COL_G)),
                  pl.BlockSpec((CHUNK, DK), lambda b, n: (n, 0)),
                  pl.BlockSpec((CHUNK, DK), lambda b, n: (n, 0)),
                  pl.BlockSpec((HEADS, CHUNK, CHUNK), lambda b, n: (0, 0, 0)),
                  pl.BlockSpec((CHUNK, HEADS), lambda b, n: (0, 0)),
                  pl.BlockSpec((CHUNK, HEADS), lambda b, n: (0, 0)),
                  pl.BlockSpec((1, HEADS * DV), lambda b, n: (0, 0))],
        out_specs=[pl.BlockSpec((CHUNK, HEADS * DV), lambda b, n: (row(b, n), 0)),
                   pl.BlockSpec((None, HEADS, DK, DV), lambda b, n: (b, 0, 0, 0))],
        out_shape=[jax.ShapeDtypeStruct((T_P, HEADS * DV), bf16),
                   jax.ShapeDtypeStruct((BATCH, HEADS, DK, DV), f32)],
        compiler_params=_cparams(("arbitrary", "arbitrary")),
        name="retention_prompt",
    )(z_all, z_all, z_all, z_all, jnp.asarray(cos_t), jnp.asarray(sin_t), jnp.asarray(intra),
      jnp.asarray(q_dec), jnp.asarray(k_dec), gn_w.reshape(1, HEADS * DV))


RET_S_BT = 4


def _ret_sample_kernel(gamma, qt_ref, kt_ref, v_ref, g_ref, cos_ref, sin_ref, gnw_ref, s_in_ref,
                       a_ref, s_out_ref):
    cos = cos_ref[...]
    sin = sin_ref[...]

    def rope_t(x):
        rolled = jnp.concatenate([x[DK // 2:], x[:DK // 2]], axis=0)
        return x * cos + rolled * sin

    def body(t, carry):
        qt = rope_t(qt_ref[t]) * (DK ** -0.5)
        kt = rope_t(kt_ref[t])
        v8 = v_ref[t]
        g8 = g_ref[t]
        for h in range(HEADS):
            s_new = s_in_ref[t, h] * gamma[h] + kt[:, h:h + 1] * v8[h:h + 1, :]
            s_out_ref[t, h] = s_new
            o = jnp.sum(qt[:, h:h + 1] * s_new, axis=0, keepdims=True)
            a_ref[t, h:h + 1, :] = _group_norm_gate(o, gnw_ref[h:h + 1, :], g8[h:h + 1, :])
        return carry

    lax.fori_loop(0, RET_S_BT, body, 0)


def _ret_sample(qt, kt, v3, g3, gn_w, state):
    _, _, _, _, gamma = _retention_consts()
    cos_t, sin_t = _rope_tables(np.array([PAST_LEN]))
    cos8 = np.ascontiguousarray(np.broadcast_to(cos_t[0][:, None], (DK, HEADS)))
    sin8 = np.ascontiguousarray(np.broadcast_to(sin_t[0][:, None], (DK, HEADS)))
    bt = RET_S_BT
    return pl.pallas_call(
        functools.partial(_ret_sample_kernel, gamma),
        grid=(T_S // bt,),
        in_specs=[pl.BlockSpec((bt, DK, HEADS), lambda i: (i, 0, 0)),
                  pl.BlockSpec((bt, DK, HEADS), lambda i: (i, 0, 0)),
                  pl.BlockSpec((bt, HEADS, DV), lambda i: (i, 0, 0)),
                  pl.BlockSpec((bt, HEADS, DV), lambda i: (i, 0, 0)),
                  pl.BlockSpec((DK, HEADS), lambda i: (0, 0)),
                  pl.BlockSpec((DK, HEADS), lambda i: (0, 0)),
                  pl.BlockSpec((HEADS, DV), lambda i: (0, 0)),
                  pl.BlockSpec((bt, HEADS, DK, DV), lambda i: (i, 0, 0, 0))],
        out_specs=[pl.BlockSpec((bt, HEADS, DV), lambda i: (i, 0, 0)),
                   pl.BlockSpec((bt, HEADS, DK, DV), lambda i: (i, 0, 0, 0))],
        out_shape=[jax.ShapeDtypeStruct((T_S, HEADS, DV), f32),
                   jax.ShapeDtypeStruct((T_S, HEADS, DK, DV), f32)],
        compiler_params=_cparams(("arbitrary",)),
        name="retention_sample",
    )(qt, kt, v3, g3, jnp.asarray(cos8), jnp.asarray(sin8), gn_w.reshape(HEADS, DV), state)


def _layer_norm(x, w, b):
    mu = jnp.mean(x, axis=-1, keepdims=True)
    var = jnp.mean(jnp.square(x - mu), axis=-1, keepdims=True)
    return (x - mu) * lax.rsqrt(var + EPS) * w + b


def _gmlp_prompt_kernel(u_ref, gv_ref, ws_ref, bst_ref, lnw_ref, lnb_ref, o_ref):
    vn = _layer_norm(jax.nn.gelu(gv_ref[...]), lnw_ref[...], lnb_ref[...]).astype(bf16)
    r = lax.broadcasted_iota(i32, (CHUNK, CHUNK), 0)
    c = lax.broadcasted_iota(i32, (CHUNK, CHUNK), 1)
    causal = r >= c
    bst = bst_ref[...]
    for g in range(GROUPS):
        w = jnp.where(causal, ws_ref[g], 0.0).astype(bf16)
        mixed = jnp.dot(w, vn[:, g * GC:(g + 1) * GC], preferred_element_type=f32) + bst[:, g:g + 1]
        o_ref[:, g * GC:(g + 1) * GC] = (jax.nn.gelu(u_ref[:, g * GC:(g + 1) * GC]) * mixed).astype(o_ref.dtype)


def _gmlp_prompt(z_all, gm_ws, gm_bs, ln_w, ln_b):
    return pl.pallas_call(
        _gmlp_prompt_kernel,
        grid=(T_P // CHUNK,),
        in_specs=[pl.BlockSpec((CHUNK, GW), lambda i: (i, COL_U)),
                  pl.BlockSpec((CHUNK, GW), lambda i: (i, COL_GV)),
                  pl.BlockSpec((GROUPS, CHUNK, CHUNK), lambda i: (0, 0, 0)),
                  pl.BlockSpec((CHUNK, GROUPS), lambda i: (0, 0)),
                  pl.BlockSpec((1, GW), lambda i: (0, 0)),
                  pl.BlockSpec((1, GW), lambda i: (0, 0))],
        out_specs=pl.BlockSpec((CHUNK, GW), lambda i: (i, 0)),
        out_shape=jax.ShapeDtypeStruct((T_P, GW), bf16),
        compiler_params=_cparams(("arbitrary",)),
        name="gmlp_prompt",
    )(z_all, z_all, gm_ws, gm_bs.T, ln_w.reshape(1, GW), ln_b.reshape(1, GW))


def _gmlp_sample_kernel(u_ref, gv_ref, w0_ref, b0_ref, lnw_ref, lnb_ref, vn_ref, o_ref):
    vn = _layer_norm(jax.nn.gelu(gv_ref[...]), lnw_ref[...], lnb_ref[...])
    vn_ref[...] = vn
    o_ref[...] = (jax.nn.gelu(u_ref[...]) * (vn * w0_ref[...] + b0_ref[...])).astype(o_ref.dtype)


def _gmlp_sample(z_all, gm_ws, gm_bs, ln_w, ln_b):
    w0 = jnp.repeat(gm_ws[:, 0, 0], GC).reshape(1, GW)
    b0 = jnp.repeat(gm_bs[:, 0], GC).reshape(1, GW)
    blk = T_P // T_S
    return pl.pallas_call(
        _gmlp_sample_kernel,
        grid=(1,),
        in_specs=[pl.BlockSpec((T_S, GW), lambda i: (blk, COL_U)),
                  pl.BlockSpec((T_S, GW), lambda i: (blk, COL_GV)),
                  pl.BlockSpec((1, GW), lambda i: (0, 0)),
                  pl.BlockSpec((1, GW), lambda i: (0, 0)),
                  pl.BlockSpec((1, GW), lambda i: (0, 0)),
                  pl.BlockSpec((1, GW), lambda i: (0, 0))],
        out_specs=[pl.BlockSpec((T_S, GW), lambda i: (0, 0)),
                   pl.BlockSpec((T_S, GW), lambda i: (0, 0))],
        out_shape=[jax.ShapeDtypeStruct((T_S, GW), f32),
                   jax.ShapeDtypeStruct((T_S, GW), bf16)],
        compiler_params=_cparams(("arbitrary",)),
        name="gmlp_sample",
    )(z_all, z_all, w0, b0, ln_w.reshape(1, GW), ln_b.reshape(1, GW))


def _merge_kernel(a_ref, b_ref, ga_ref, gb_ref, woa_ref, wob_ref, o_ref):
    ya = jnp.dot(a_ref[...].astype(bf16), woa_ref[...], preferred_element_type=f32)
    yb = jnp.dot(b_ref[...].astype(bf16), wob_ref[...], preferred_element_type=f32)
    o_ref[...] = (jax.nn.sigmoid(ga_ref[...]) * ya + jax.nn.sigmoid(gb_ref[...]) * yb).astype(o_ref.dtype)


def _merge(a, bm, z_all, w_oa, w_ob, tm, z_blk0):
    m = a.shape[0]
    resident = lambda: pl.BlockSpec((D, D), lambda i: (0, 0), pipeline_mode=pl.Buffered(1))
    return pl.pallas_call(
        _merge_kernel,
        grid=(m // tm,),
        in_specs=[pl.BlockSpec((tm, D), lambda i: (i, 0)),
                  pl.BlockSpec((tm, D), lambda i: (i, 0)),
                  pl.BlockSpec((tm, D), lambda i: (z_blk0 + i, COL_GA)),
                  pl.BlockSpec((tm, D), lambda i: (z_blk0 + i, COL_GB)),
                  resident(), resident()],
        out_specs=pl.BlockSpec((tm, D), lambda i: (i, 0)),
        out_shape=jax.ShapeDtypeStruct((m, D), bf16),
        compiler_params=_cparams(("arbitrary",)),
        name="merge",
    )(a, bm, z_all, z_all, w_oa, w_ob)


def _post_kernel(m_ref, x_ref, g1_ref, sc_ref, sh_ref, nw_ref, wout_ref, wr_ref, br_ref,
                 x1_ref, h2_ref, idx_ref, gate_ref, rank_ref, cnt_ref):
    i = pl.program_id(0)

    @pl.when(i == 0)
    def _():
        cnt_ref[...] = jnp.zeros_like(cnt_ref)

    tm = m_ref.shape[0]
    y = jnp.dot(m_ref[...], wout_ref[...], preferred_element_type=f32)
    x1 = x_ref[...] + g1_ref[...] * y
    x1_ref[...] = x1
    xn = x1 * lax.rsqrt(jnp.mean(x1 * x1, axis=-1, keepdims=True) + EPS) * nw_ref[...]
    h2 = xn * (1.0 + sc_ref[...]) + sh_ref[...]
    h2_ref[...] = h2

    h_hi = h2.astype(bf16)
    h_lo = (h2 - h_hi.astype(f32)).astype(bf16)
    p_hi = jnp.dot(h_hi, wr_ref[...], preferred_element_type=f32)
    p_lo = jnp.dot(h_lo, wr_ref[...], preferred_element_type=f32)
    logits = p_hi[:, :LANES] + p_hi[:, LANES:] + p_lo[:, :LANES] + br_ref[...]

    lane = lax.broadcasted_iota(i32, (tm, LANES), 1).astype(f32)
    col = lax.broadcasted_iota(i32, (tm, TOP_K), 1)
    work = jnp.where(lane < N_EXPERTS, logits, -jnp.inf)
    member = jnp.zeros((tm, LANES), f32)
    vals, sels = [], []
    idx_out = jnp.zeros((tm, TOP_K), i32)
    for k in range(TOP_K):
        mx = jnp.max(work, axis=1, keepdims=True)
        ix = jnp.min(jnp.where(work == mx, lane, float(LANES)), axis=1, keepdims=True)
        sel = lane == ix
        vals.append(mx)
        sels.append(sel)
        idx_out = jnp.where(col == k, ix.astype(i32), idx_out)
        member = jnp.where(sel, 1.0, member)
        work = jnp.where(sel, -jnp.inf, work)
    idx_ref[...] = idx_out

    exps = [jnp.exp(v - vals[0]) for v in vals]
    den = exps[0] + exps[1] + exps[2] + exps[3]
    gate_out = jnp.zeros((tm, TOP_K), f32)
    for k in range(TOP_K):
        gate_out = jnp.where(col == k, exps[k] / den, gate_out)
    gate_ref[...] = gate_out

    r = lax.broadcasted_iota(i32, (tm, tm), 0)
    c = lax.broadcasted_iota(i32, (tm, tm), 1)
    lower = jnp.where(c < r, 1.0, 0.0).astype(bf16)
    before = jnp.dot(lower, member.astype(bf16), preferred_element_type=f32) + cnt_ref[...]
    rank_out = jnp.zeros((tm, TOP_K), i32)
    for k in range(TOP_K):
        rk = jnp.sum(jnp.where(sels[k], before, 0.0), axis=1, keepdims=True)
        rank_out = jnp.where(col == k, rk.astype(i32), rank_out)
    rank_ref[...] = rank_out
    cnt_ref[...] += jnp.sum(member, axis=0, keepdims=True)


def _post(m, x, mod, mod_is_rows, nw, w_out, w_router, b_router, tm, per_b):
    mm = m.shape[0]
    if mod_is_rows:
        mspec = lambda chunk: pl.BlockSpec((tm, D), lambda i: (i, chunk))
    else:
        mspec = lambda chunk: pl.BlockSpec((None, 1, D), lambda i: (i // per_b, 0, chunk))
    in_specs = [pl.BlockSpec((tm, D), lambda i: (i, 0)),
                pl.BlockSpec((tm, D), lambda i: (i, 0)),
                mspec(MOD_G1), mspec(MOD_SC2), mspec(MOD_SH2),
                pl.BlockSpec((1, D), lambda i: (0, 0)),
                pl.BlockSpec((D, D), lambda i: (0, 0), pipeline_mode=pl.Buffered(1)),
                pl.BlockSpec((D, 2 * LANES), lambda i: (0, 0)),
                pl.BlockSpec((1, LANES), lambda i: (0, 0))]
    args = [m, x, mod, mod, mod, nw, w_out, w_router, b_router]
    return pl.pallas_call(
        _post_kernel,
        grid=(mm // tm,),
        in_specs=in_specs,
        out_specs=[pl.BlockSpec((tm, D), lambda i: (i, 0)),
                   pl.BlockSpec((tm, D), lambda i: (i, 0)),
                   pl.BlockSpec((tm, TOP_K), lambda i: (i, 0)),
                   pl.BlockSpec((tm, TOP_K), lambda i: (i, 0)),
                   pl.BlockSpec((tm, TOP_K), lambda i: (i, 0)),
                   pl.BlockSpec((1, LANES), lambda i: (0, 0))],
        out_shape=[jax.ShapeDtypeStruct((mm, D), f32),
                   jax.ShapeDtypeStruct((mm, D), f32),
                   jax.ShapeDtypeStruct((mm, TOP_K), i32),
                   jax.ShapeDtypeStruct((mm, TOP_K), f32),
                   jax.ShapeDtypeStruct((mm, TOP_K), i32),
                   jax.ShapeDtypeStruct((1, LANES), f32)],
        compiler_params=_cparams(("arbitrary",)),
        name="post_mixer",
    )(*args)


DISP_TB = 128


DISP_PT = T_P // DISP_TB


def _dispatch_kernel(zblk_ref, dest_ref, h2p_ref, h2s_ref, xbuf_ref, zero_ref, sem):
    i = pl.program_id(0)

    def row_copy(src_ref, t, d):
        return pltpu.make_async_copy(src_ref.at[pl.ds(t, 1)], xbuf_ref.at[pl.ds(d, 1)], sem)

    def zero_copy(blk):
        return pltpu.make_async_copy(zero_ref, xbuf_ref.at[pl.ds(blk * MOE_BM, MOE_BM)], sem)

    @pl.when(i == 0)
    def _():
        zero_ref[...] = jnp.zeros_like(zero_ref)
        used = zblk_ref[N_EXPERTS]

        def zstart(e, c):
            zero_copy(zblk_ref[e]).start()
            return c

        def tstart(b, c):
            zero_copy(b).start()
            return c

        def zwait(e, c):
            zero_copy(0).wait()
            return c

        lax.fori_loop(0, N_EXPERTS, zstart, 0)
        lax.fori_loop(used, MOE_NBLK, tstart, 0)
        lax.fori_loop(0, N_EXPERTS, zwait, 0)
        lax.fori_loop(used, MOE_NBLK, zwait, 0)

    def issue(src_ref):
        def group(gi, c):
            for tt in range(SUBLANES):
                t = pl.multiple_of(gi * SUBLANES, SUBLANES) + tt
                for k in range(TOP_K):
                    d = dest_ref[0, 0, (gi * SUBLANES + tt) * TOP_K + k]
                    row_copy(src_ref, t, d).start(priority=k % 2)
            return c
        lax.fori_loop(0, DISP_TB // SUBLANES, group, 0)

    @pl.when(i < DISP_PT)
    def _():
        issue(h2p_ref)

    @pl.when(i == DISP_PT)
    def _():
        issue(h2s_ref)

    def wait_group(gi, c):
        for _ in range(SUBLANES * TOP_K):
            row_copy(h2p_ref, 0, 0).wait()
        return c

    lax.fori_loop(0, DISP_TB // SUBLANES, wait_group, 0)


def _dispatch(zero_blk, dest_all, h2_p, h2_s):
    nt = T_ALL // DISP_TB
    return pl.pallas_call(
        _dispatch_kernel,
        grid_spec=pltpu.PrefetchScalarGridSpec(
            num_scalar_prefetch=1,
            grid=(nt,),
            in_specs=[pl.BlockSpec((1, 1, DISP_TB * TOP_K), lambda i, zb: (i, 0, 0), memory_space=pltpu.SMEM),
                      pl.BlockSpec((DISP_TB, D), lambda i, zb: (jnp.minimum(i, DISP_PT - 1), 0)),
                      pl.BlockSpec((DISP_TB, D), lambda i, zb: (0, 0))],
            out_specs=pl.BlockSpec(memory_space=pl.ANY),
            scratch_shapes=[pltpu.VMEM((MOE_BM, D), f32), pltpu.SemaphoreType.DMA(())]),
        out_shape=jax.ShapeDtypeStruct((MOE_NROWS, D), f32),
        compiler_params=_cparams(("arbitrary",)),
        name="moe_dispatch",
    )(zero_blk, dest_all.reshape(nt, 1, DISP_TB * TOP_K), h2_p, h2_s)


def _moe_ffn_kernel(e_ref, sb_ref, nb_ref, xbuf_ref, wg_ref, wu_ref, bg_ref, bu_ref, wd_ref, bd_ref,
                    ybuf_ref, x_ref, acc_ref, wgb_ref, wub_ref, wdb_ref, sem_in, sem_out):
    p = pl.program_id(0)
    j = pl.program_id(1)
    nb = nb_ref[p]
    sb = sb_ref[p]

    def in_copy(r):
        return pltpu.make_async_copy(xbuf_ref.at[pl.ds((sb + r) * MOE_BM, MOE_BM)],
                                     x_ref.at[pl.ds(r * MOE_BM, MOE_BM)], sem_in.at[r])

    def out_copy(r):
        return pltpu.make_async_copy(acc_ref.at[pl.ds(r * MOE_BM, MOE_BM)],
                                     ybuf_ref.at[pl.ds((sb + r) * MOE_BM, MOE_BM)], sem_out)

    def for_blocks(fn):
        def body(r, c):
            fn(r)
            return c
        lax.fori_loop(0, nb, body, 0)

    @pl.when((p == 0) & (j == 0))
    def _():
        used = nb_ref[MOE_PMAX]
        acc_ref[0:MOE_BM, :] = jnp.zeros((MOE_BM, D), f32)

        def tail_copy(b):
            return pltpu.make_async_copy(acc_ref.at[pl.ds(0, MOE_BM)],
                                         ybuf_ref.at[pl.ds(b * MOE_BM, MOE_BM)], sem_out)

        def tstart(b, c):
            tail_copy(b).start()
            return c

        def twait(b, c):
            tail_copy(0).wait()
            return c

        lax.fori_loop(used, MOE_NBLK, tstart, 0)
        lax.fori_loop(used, MOE_NBLK, twait, 0)

    first = j == 0
    last = j == MOE_J - 1

    @pl.when(nb > 0)
    def _():
        @pl.when(first)
        def _():
            for_blocks(lambda r: in_copy(r).start())
            bd = jnp.broadcast_to(bd_ref[...], (MOE_BM, D))

            def init(r):
                acc_ref[pl.ds(pl.multiple_of(r * MOE_BM, MOE_BM), MOE_BM), :] = bd
            for_blocks(init)

        wgb_ref[...] = wg_ref[...].astype(bf16)
        wub_ref[...] = wu_ref[...].astype(bf16)
        wdb_ref[...] = wd_ref[...].astype(bf16)
        bg = bg_ref[...]
        bu = bu_ref[...]

        def ffn_rows(b0, n_blk):
            rows = pl.ds(pl.multiple_of(b0 * MOE_BM, MOE_BM), n_blk * MOE_BM)
            x = x_ref[rows, :].astype(bf16)
            gate = jnp.dot(x, wgb_ref[...], preferred_element_type=f32) + bg
            up = jnp.dot(x, wub_ref[...], preferred_element_type=f32) + bu
            gate = jnp.minimum(gate, SWIGLU_LIMIT)
            up = jnp.clip(up, -SWIGLU_LIMIT, SWIGLU_LIMIT)
            act = (up + 1.0) * (gate * jax.nn.sigmoid(gate * SWIGLU_ALPHA))
            acc_ref[rows, :] += jnp.dot(act.astype(bf16), wdb_ref[...], preferred_element_type=f32)

        def trip(b0, groups):
            n_blk = sum(groups)

            @pl.when(first)
            def _():
                for b in range(n_blk):
                    in_copy(b0 + b).wait()

            off = 0
            for g in groups:
                ffn_rows(b0 + off, g)
                off += g

            @pl.when(last)
            def _():
                for b in range(n_blk):
                    out_copy(b0 + b).start()

        full = (2, 2, 2)
        per_trip = sum(full)
        ntrip = sum((nb >= per_trip * m).astype(i32) for m in range(1, MOE_RB // per_trip + 1))

        def full_trip(q, c):
            trip(q * per_trip, full)
            return c
        lax.fori_loop(0, ntrip, full_trip, 0)
        rem = nb - ntrip * per_trip
        for n_rem, groups in ((1, (1,)), (2, (2,)), (3, (2, 1)), (4, (2, 2)), (5, (2, 2, 1))):
            @pl.when(rem == n_rem)
            def _(groups=groups):
                trip(ntrip * per_trip, groups)

        @pl.when(last)
        def _():
            for_blocks(lambda r: out_copy(r).wait())


def _moe_ffn(e_arr, sb_arr, nb_arr, xbuf, w_gate_up, b_gate_up, w_down, b_down):
    def jj(p, j, nb):
        return jnp.where(nb[p] > 0, j, MOE_J - 1)

    in_specs = [
        pl.BlockSpec(memory_space=pl.ANY),
        pl.BlockSpec((None, D, MOE_TF), lambda p, j, e, sb, nb: (e[p], 0, jj(p, j, nb))),
        pl.BlockSpec((None, D, MOE_TF), lambda p, j, e, sb, nb: (e[p], 0, MOE_J + jj(p, j, nb))),
        pl.BlockSpec((None, 1, MOE_TF), lambda p, j, e, sb, nb: (e[p], 0, jj(p, j, nb))),
        pl.BlockSpec((None, 1, MOE_TF), lambda p, j, e, sb, nb: (e[p], 0, MOE_J + jj(p, j, nb))),
        pl.BlockSpec((None, MOE_TF, D), lambda p, j, e, sb, nb: (e[p], jj(p, j, nb), 0)),
        pl.BlockSpec((None, 1, D), lambda p, j, e, sb, nb: (e[p], 0, 0)),
    ]
    return pl.pallas_call(
        _moe_ffn_kernel,
        grid_spec=pltpu.PrefetchScalarGridSpec(
            num_scalar_prefetch=3,
            grid=(MOE_PMAX, MOE_J),
            in_specs=in_specs,
            out_specs=pl.BlockSpec(memory_space=pl.ANY),
            scratch_shapes=[pltpu.VMEM((MOE_RB * MOE_BM, D), f32),
                            pltpu.VMEM((MOE_RB * MOE_BM, D), f32),
                            pltpu.VMEM((D, MOE_TF), bf16),
                            pltpu.VMEM((D, MOE_TF), bf16),
                            pltpu.VMEM((MOE_TF, D), bf16),
                            pltpu.SemaphoreType.DMA((MOE_RB,)),
                            pltpu.SemaphoreType.DMA(())]),
        out_shape=jax.ShapeDtypeStruct((MOE_NROWS, D), f32),
        compiler_params=_cparams(("arbitrary", "arbitrary"), vmem=56 * 1024 * 1024),
        name="moe_ffn",
    )(e_arr, sb_arr, nb_arr, xbuf, w_gate_up, w_gate_up,
      b_gate_up.reshape(N_EXPERTS, 1, 2 * D_EXPERT), b_gate_up.reshape(N_EXPERTS, 1, 2 * D_EXPERT),
      w_down, b_down.reshape(N_EXPERTS, 1, D))


COMB_TB = 128


def _combine_kernel(dcur_ref, dnext_ref, ybuf_ref, gate_ref, x1_ref, g2_ref, fw_ref, o_ref, rows_ref, sem):
    i = pl.program_id(0)
    slot = i & 1

    def row_copy(buf, n, d):
        return pltpu.make_async_copy(ybuf_ref.at[pl.ds(d, 1)], rows_ref.at[buf, pl.ds(n, 1)], sem.at[buf])

    def wait_tile(buf):
        def wait_group(gi, c):
            for _ in range(SUBLANES * TOP_K):
                row_copy(buf, 0, 0).wait()
            return c
        lax.fori_loop(0, COMB_TB // SUBLANES, wait_group, 0)

    @pl.when(i == 0)
    def _():
        def start_group(gi, c):
            for tt in range(SUBLANES):
                t = pl.multiple_of(gi * SUBLANES, SUBLANES) + tt
                for k in range(TOP_K):
                    d = dcur_ref[0, 0, (gi * SUBLANES + tt) * TOP_K + k]
                    row_copy(0, k * COMB_TB + t, d).start(priority=k % 2)
            return c
        lax.fori_loop(0, COMB_TB // SUBLANES, start_group, 0)

    wait_tile(slot)

    for t in range(COMB_TB):
        for k in range(TOP_K):
            row_copy(1 - slot, k * COMB_TB + t, dnext_ref[0, 0, t * TOP_K + k]).start(priority=k % 2)

    gates = gate_ref[...]
    f = gates[:, 0:1] * rows_ref[slot, 0:COMB_TB, :]
    for k in range(1, TOP_K):
        f = f + gates[:, k:k + 1] * rows_ref[slot, k * COMB_TB:(k + 1) * COMB_TB, :]
    x2 = x1_ref[...] + g2_ref[...] * f
    o_ref[...] = x2 * lax.rsqrt(jnp.mean(x2 * x2, axis=-1, keepdims=True) + EPS) * fw_ref[...]

    @pl.when(i == pl.num_programs(0) - 1)
    def _():
        wait_tile(1 - slot)


def _combine(dest, ybuf, gates, x1, mod, mod_is_rows, per_b, final_w):
    mm = x1.shape[0]
    nt = mm // COMB_TB
    dest3 = dest.reshape(nt, 1, COMB_TB * TOP_K)
    if mod_is_rows:
        g2spec = pl.BlockSpec((COMB_TB, D), lambda i: (i, MOD_G2))
    else:
        g2spec = pl.BlockSpec((None, 1, D), lambda i: (i // per_b, 0, MOD_G2))
    return pl.pallas_call(
        _combine_kernel,
        grid=(nt,),
        in_specs=[pl.BlockSpec((1, 1, COMB_TB * TOP_K), lambda i: (i, 0, 0), memory_space=pltpu.SMEM),
                  pl.BlockSpec((1, 1, COMB_TB * TOP_K), lambda i: (jnp.minimum(i + 1, nt - 1), 0, 0),
                               memory_space=pltpu.SMEM),
                  pl.BlockSpec(memory_space=pl.ANY),
                  pl.BlockSpec((COMB_TB, TOP_K), lambda i: (i, 0)),
                  pl.BlockSpec((COMB_TB, D), lambda i: (i, 0)),
                  g2spec,
                  pl.BlockSpec((1, D), lambda i: (0, 0))],
        out_specs=pl.BlockSpec((COMB_TB, D), lambda i: (i, 0)),
        out_shape=jax.ShapeDtypeStruct((mm, D), f32),
        scratch_shapes=[pltpu.VMEM((2, TOP_K * COMB_TB, D), f32), pltpu.SemaphoreType.DMA((2,))],
        compiler_params=_cparams(("arbitrary",)),
        name="moe_combine",
    )(dest3, dest3, ybuf, gates, x1, mod, final_w.reshape(1, D))


def _routing_tables(cnt_p, cnt_s, idx_p, rank_p, idx_s, rank_s):
    cp = cnt_p[0, :N_EXPERTS].astype(i32)
    cs = cnt_s[0, :N_EXPERTS].astype(i32)
    nblk = (cp + cs + MOE_BM - 1) // MOE_BM
    blk_end = jnp.cumsum(nblk)
    blk_start = blk_end - nblk
    row_start = blk_start * MOE_BM
    dest_p = row_start[idx_p.reshape(-1)] + rank_p.reshape(-1)
    dest_s = (row_start + cp)[idx_s.reshape(-1)] + rank_s.reshape(-1)
    used = blk_end[-1:]
    zero_blk = jnp.concatenate([jnp.maximum(blk_end - 1, 0), used]).astype(i32)
    npass = (nblk + MOE_RB - 1) // MOE_RB
    pass_end = jnp.cumsum(npass)
    total = pass_end[-1]
    pid = jnp.arange(MOE_PMAX, dtype=i32)
    pid_c = jnp.minimum(pid, total - 1)
    pe = jnp.minimum(jnp.sum((pass_end[None, :] <= pid_c[:, None]).astype(i32), axis=1), N_EXPERTS - 1)
    local = pid_c - (pass_end - npass)[pe]
    active = pid < total
    sb = jnp.where(active, blk_start[pe] + local * MOE_RB, 0).astype(i32)
    nb = jnp.where(active, jnp.clip(nblk[pe] - local * MOE_RB, 0, MOE_RB), 0)
    nb = jnp.concatenate([nb, used]).astype(i32)
    return dest_p.astype(i32), dest_s.astype(i32), zero_blk, pe, sb, nb


def kernel(x_prompt, x_sample, c_prompt, c_sample, state_ret, norm1_w, norm2_w, w_ada, b_ada, w_in,
           ret_gn_w, gm_ln_w, gm_ln_b, gm_ws, gm_bs, w_oa, w_ob, w_out, w_router, b_router,
           w_gate_up, b_gate_up, w_down, b_down, final_norm_w):
    x_p = x_prompt.reshape(T_P, D)
    x_s = x_sample.reshape(T_S, D)
    per_b = lambda tm: SEQ // tm

    mod = _ada(jnp.concatenate([c_prompt, c_sample], axis=0), w_ada[0], b_ada[0])
    mod_p = mod[:BATCH].reshape(BATCH, 1, 6 * D)
    mod_s = mod[BATCH:]

    n1 = norm1_w[0].reshape(1, D)
    h_all = _norm1(x_p, x_s, n1, mod_p, mod_s)
    z_all = _proj_in(h_all, w_in[0])

    a_p, s_p = _ret_prompt(z_all, ret_gn_w[0])
    z_s = z_all[T_P:]
    qt = z_s[:, :HEADS * DK].reshape(T_S, HEADS, DK).transpose(0, 2, 1)
    kt = z_s[:, HEADS * DK:2 * HEADS * DK].reshape(T_S, HEADS, DK).transpose(0, 2, 1)
    v3 = z_s[:, 2 * HEADS * DK:2 * HEADS * DK + HEADS * DV].reshape(T_S, HEADS, DV)
    g3 = z_s[:, 2 * HEADS * DK + HEADS * DV:2 * HEADS * DK + 2 * HEADS * DV].reshape(T_S, HEADS, DV)
    a_s, s_s = _ret_sample(qt, kt, v3, g3, ret_gn_w[0], state_ret[0])
    bm_p = _gmlp_prompt(z_all, gm_ws[0], gm_bs[0], gm_ln_w[0], gm_ln_b[0])
    vn_s, bm_s = _gmlp_sample(z_all, gm_ws[0], gm_bs[0], gm_ln_w[0], gm_ln_b[0])

    woa, wob, wout = w_oa[0].astype(bf16), w_ob[0].astype(bf16), w_out[0].astype(bf16)
    m_p = _merge(a_p, bm_p, z_all, woa, wob, 256, 0)
    m_s = _merge(a_s.reshape(T_S, HEADS * DV), bm_s, z_all, woa, wob, T_S, T_P // T_S)

    n2 = norm2_w[0].reshape(1, D)
    wr_hi = w_router[0].astype(bf16)
    wr_lo = (w_router[0] - wr_hi.astype(f32)).astype(bf16)
    lane_pad = ((0, 0), (0, LANES - N_EXPERTS))
    wr_cat = jnp.concatenate([jnp.pad(wr_hi, lane_pad), jnp.pad(wr_lo, lane_pad)], axis=1)
    br = jnp.pad(b_router[0].reshape(1, N_EXPERTS), lane_pad)
    x1_p, h2_p, idx_p, gate_p, rank_p, cnt_p = _post(
        m_p, x_p, mod_p, False, n2, wout, wr_cat, br, 512, per_b(512))
    x1_s, h2_s, idx_s, gate_s, rank_s, cnt_s = _post(
        m_s, x_s, mod_s, True, n2, wout, wr_cat, br, T_S, 1)

    dest_p, dest_s, zero_blk, pe, sb, nb = _routing_tables(cnt_p, cnt_s, idx_p, rank_p, idx_s, rank_s)
    xbuf = _dispatch(zero_blk, jnp.concatenate([dest_p, dest_s], axis=0), h2_p, h2_s)
    ybuf = _moe_ffn(pe, sb, nb, xbuf, w_gate_up[0], b_gate_up[0], w_down[0], b_down[0])
    y_p = _combine(dest_p, ybuf, gate_p, x1_p, mod_p, False, per_b(COMB_TB), final_norm_w)
    y_s = _combine(dest_s, ybuf, gate_s, x1_s, mod_s, True, 1, final_norm_w)

    return (y_p.reshape(BATCH, SEQ, D),
            y_s.reshape(DEC_BATCH, 1, D),
            s_p.reshape(1, BATCH, HEADS, DK, DV),
            s_s.reshape(1, DEC_BATCH, HEADS, DK, DV),
            vn_s.reshape(1, DEC_BATCH, 1, GW))
```

```python
import functools

import numpy as np
import jax
import jax.numpy as jnp
from jax import lax
from jax.experimental import pallas as pl
from jax.experimental.pallas import tpu as pltpu

f32 = jnp.float32
bf16 = jnp.bfloat16
i32 = jnp.int32

D = 2048
BATCH = 4
SEQ = 2048
DEC_BATCH = 128
PAST_LEN = 16384
HEADS = 8
DK = 128
DV = 256
CHUNK = 128
ROPE_THETA = 10000.0
GROUPS = 8
GW = 2048
GC = GW // GROUPS
N_EXPERTS = 32
TOP_K = 4
D_EXPERT = 2048
SWIGLU_LIMIT = 7.0
SWIGLU_ALPHA = 1.702
EPS = 1e-6

T_P = BATCH * SEQ
T_S = DEC_BATCH
T_ALL = T_P + T_S
D_IN = 14336
COL_Q, COL_K = 0, 1
COL_V, COL_G, COL_U, COL_GV, COL_GA, COL_GB = 1, 2, 3, 4, 5, 6
MOD_SH1, MOD_SC1, MOD_G1, MOD_SH2, MOD_SC2, MOD_G2 = range(6)

LANES = 128
SUBLANES = 8
VMEM_LIMIT = 48 * 1024 * 1024

MOE_BM = 128
MOE_RB = 12
MOE_TF = 256
MOE_J = D_EXPERT // MOE_TF
TK_ALL = T_ALL * TOP_K
MOE_NBLK = (TK_ALL + N_EXPERTS * (MOE_BM - 1)) // MOE_BM
MOE_NROWS = MOE_NBLK * MOE_BM
MOE_PMAX = N_EXPERTS + (MOE_NBLK - N_EXPERTS) // MOE_RB


def _cparams(sem, vmem=VMEM_LIMIT):
    return pltpu.CompilerParams(dimension_semantics=sem, vmem_limit_bytes=vmem)


def _ada_kernel(c_ref, w_ref, b_ref, o_ref):
    o_ref[...] = jnp.dot(c_ref[...].astype(bf16), w_ref[...].astype(bf16),
                         preferred_element_type=f32) + b_ref[...]


def _ada(c_all, w_ada, b_ada):
    m = c_all.shape[0]
    tn = 1024
    return pl.pallas_call(
        _ada_kernel,
        grid=(6 * D // tn,),
        in_specs=[pl.BlockSpec((m, D), lambda j: (0, 0)),
                  pl.BlockSpec((D, tn), lambda j: (0, j)),
                  pl.BlockSpec((1, tn), lambda j: (0, j))],
        out_specs=pl.BlockSpec((m, tn), lambda j: (0, j)),
        out_shape=jax.ShapeDtypeStruct((m, 6 * D), f32),
        compiler_params=_cparams(("arbitrary",)),
        name="ada_mod",
    )(c_all, w_ada, b_ada.reshape(1, 6 * D))


def _norm_mod_kernel(x_ref, nw_ref, sc_ref, sh_ref, o_ref):
    x = x_ref[...]
    y = x * lax.rsqrt(jnp.mean(x * x, axis=-1, keepdims=True) + EPS) * nw_ref[...]
    o_ref[...] = (y * (1.0 + sc_ref[...]) + sh_ref[...]).astype(o_ref.dtype)


NORM1_TM = T_S
NORM1_PT = T_P // NORM1_TM


def _norm1_kernel(xp_ref, xs_ref, nw_ref, scp_ref, shp_ref, scs_ref, shs_ref, o_ref):
    i = pl.program_id(0)

    @pl.when(i < NORM1_PT)
    def _():
        _norm_mod_kernel(xp_ref, nw_ref, scp_ref, shp_ref, o_ref)

    @pl.when(i == NORM1_PT)
    def _():
        _norm_mod_kernel(xs_ref, nw_ref, scs_ref, shs_ref, o_ref)


def _norm1(x_p, x_s, nw, mod_p, mod_s):
    tm = NORM1_TM
    per_b = SEQ // tm
    pt = lambda i: jnp.minimum(i, NORM1_PT - 1)
    return pl.pallas_call(
        _norm1_kernel,
        grid=(NORM1_PT + 1,),
        in_specs=[pl.BlockSpec((tm, D), lambda i: (pt(i), 0)),
                  pl.BlockSpec((tm, D), lambda i: (0, 0)),
                  pl.BlockSpec((1, D), lambda i: (0, 0)),
                  pl.BlockSpec((None, 1, D), lambda i: (pt(i) // per_b, 0, MOD_SC1)),
                  pl.BlockSpec((None, 1, D), lambda i: (pt(i) // per_b, 0, MOD_SH1)),
                  pl.BlockSpec((tm, D), lambda i: (0, MOD_SC1)),
                  pl.BlockSpec((tm, D), lambda i: (0, MOD_SH1))],
        out_specs=pl.BlockSpec((tm, D), lambda i: (i, 0)),
        out_shape=jax.ShapeDtypeStruct((T_ALL, D), bf16),
        compiler_params=_cparams(("arbitrary",)),
        name="norm1",
    )(x_p, x_s, nw, mod_p, mod_p, mod_s, mod_s)


def _proj_in_kernel(h_ref, w_ref, o_ref, wb_ref):
    @pl.when(pl.program_id(1) == 0)
    def _():
        wb_ref[...] = w_ref[...].astype(bf16)

    o_ref[...] = jnp.dot(h_ref[...], wb_ref[...], preferred_element_type=f32)


def _proj_in(h_all, w_in):
    tm, tn = 640, 1024
    return pl.pallas_call(
        _proj_in_kernel,
        grid=(D_IN // tn, T_ALL // tm),
        in_specs=[pl.BlockSpec((tm, D), lambda j, i: (i, 0)),
                  pl.BlockSpec((D, tn), lambda j, i: (0, j))],
        out_specs=pl.BlockSpec((tm, tn), lambda j, i: (i, j)),
        out_shape=jax.ShapeDtypeStruct((T_ALL, D_IN), f32),
        scratch_shapes=[pltpu.VMEM((D, tn), bf16)],
        compiler_params=_cparams(("arbitrary", "arbitrary")),
        name="proj_in",
    )(h_all, w_in)


def _retention_consts():
    h = np.arange(HEADS, dtype=np.float64)
    lg = np.log1p(-np.exp2(-5.0 - h))
    idx = np.arange(CHUNK, dtype=np.float64)
    diff = idx[:, None] - idx[None, :]
    intra = np.where(diff >= 0, np.exp(np.maximum(diff, 0.0)[None] * lg[:, None, None]), 0.0)
    q_dec = np.exp((idx + 1.0)[:, None] * lg[None, :])
    k_dec = np.exp((CHUNK - 1.0 - idx)[:, None] * lg[None, :])
    s_dec = np.exp(CHUNK * lg)
    gamma = np.exp(lg)
    return (intra.astype(np.float32), q_dec.astype(np.float32), k_dec.astype(np.float32),
            [float(v) for v in s_dec], [float(v) for v in gamma])


def _rope_tables(pos):
    half = DK // 2
    freq = ROPE_THETA ** (-np.arange(half, dtype=np.float64) / half)
    ang = np.asarray(pos, dtype=np.float64)[:, None] * freq[None, :]
    cos, sin = np.cos(ang), np.sin(ang)
    cos_t = np.concatenate([cos, cos], axis=-1).astype(np.float32)
    sin_t = np.concatenate([-sin, sin], axis=-1).astype(np.float32)
    return cos_t, sin_t


def _group_norm_gate(o, gnw, g):
    mu = jnp.mean(o, axis=-1, keepdims=True)
    var = jnp.mean(jnp.square(o - mu), axis=-1, keepdims=True)
    on = (o - mu) * lax.rsqrt(var + EPS) * gnw
    return jax.nn.silu(g) * on


def _ret_prompt_kernel(s_dec, q_ref, k_ref, v_ref, g_ref, cos_ref, sin_ref, intra_ref, qd_ref, kd_ref,
                       gnw_ref, a_ref, s_ref):
    n = pl.program_id(1)

    @pl.when(n == 0)
    def _():
        s_ref[...] = jnp.zeros_like(s_ref)

    cos = cos_ref[...]
    sin = sin_ref[...]
    qd = qd_ref[...]
    kd = kd_ref[...]
    for h in range(HEADS):
        qh = q_ref[:, h * DK:(h + 1) * DK]
        kh = k_ref[:, h * DK:(h + 1) * DK]
        qh = (qh * cos + pltpu.roll(qh, DK // 2, 1) * sin) * (DK ** -0.5)
        kh = kh * cos + pltpu.roll(kh, DK // 2, 1) * sin
        vh = v_ref[:, h * DV:(h + 1) * DV].astype(bf16)
        s_old = s_ref[h]
        scores = lax.dot_general(qh.astype(bf16), kh.astype(bf16), (((1,), (1,)), ((), ())),
                                 preferred_element_type=f32) * intra_ref[h]
        o = jnp.dot(scores.astype(bf16), vh, preferred_element_type=f32)
        o = o + jnp.dot((qh * qd[:, h:h + 1]).astype(bf16), s_old.astype(bf16), preferred_element_type=f32)
        kw_t = (kh * kd[:, h:h + 1]).T.astype(bf16)
        s_ref[h] = s_old * s_dec[h] + jnp.dot(kw_t, vh, preferred_element_type=f32)
        a_ref[:, h * DV:(h + 1) * DV] = _group_norm_gate(
            o, gnw_ref[:, h * DV:(h + 1) * DV], g_ref[:, h * DV:(h + 1) * DV]).astype(a_ref.dtype)


def _ret_prompt(z_all, gn_w):
    intra, q_dec, k_dec, s_dec, _ = _retention_consts()
    cos_t, sin_t = _rope_tables(np.arange(SEQ))
    nch = SEQ // CHUNK
    row = lambda b, n: b * nch + n
    return pl.pallas_call(
        functools.partial(_ret_prompt_kernel, s_dec),
        grid=(BATCH, nch),
        in_specs=[pl.BlockSpec((CHUNK, HEADS * DK), lambda b, n: (row(b, n), COL_Q)),
                  pl.BlockSpec((CHUNK, HEADS * DK), lambda b, n: (row(b, n), COL_K)),
                  pl.BlockSpec((CHUNK, HEADS * DV), lambda b, n: (row(b, n), COL_V)),
                  pl.BlockSpec((CHUNK, HEADS * DV), lambda b, n: (row(b, n), COL_G)),
                  pl.BlockSpec((CHUNK, DK), lambda b, n: (n, 0)),
                  pl.BlockSpec((CHUNK, DK), lambda b, n: (n, 0)),
                  pl.BlockSpec((HEADS, CHUNK, CHUNK), lambda b, n: (0, 0, 0)),
                  pl.BlockSpec((CHUNK, HEADS), lambda b, n: (0, 0)),
                  pl.BlockSpec((CHUNK, HEADS), lambda b, n: (0, 0)),
                  pl.BlockSpec((1, HEADS * DV), lambda b, n: (0, 0))],
        out_specs=[pl.BlockSpec((CHUNK, HEADS * DV), lambda b, n: (row(b, n), 0)),
                   pl.BlockSpec((None, HEADS, DK, DV), lambda b, n: (b, 0, 0, 0))],
        out_shape=[jax.ShapeDtypeStruct((T_P, HEADS * DV), bf16),
                   jax.ShapeDtypeStruct((BATCH, HEADS, DK, DV), f32)],
        compiler_params=_cparams(("arbitrary", "arbitrary")),
        name="retention_prompt",
    )(z_all, z_all, z_all, z_all, jnp.asarray(cos_t), jnp.asarray(sin_t), jnp.asarray(intra),
      jnp.asarray(q_dec), jnp.asarray(k_dec), gn_w.reshape(1, HEADS * DV))


RET_S_BT = 4


def _ret_sample_kernel(gamma, qt_ref, kt_ref, v_ref, g_ref, cos_ref, sin_ref, gnw_ref, s_in_ref,
                       a_ref, s_out_ref):
    cos = cos_ref[...]
    sin = sin_ref[...]

    def rope_t(x):
        rolled = jnp.concatenate([x[DK // 2:], x[:DK // 2]], axis=0)
        return x * cos + rolled * sin

    def body(t, carry):
        qt = rope_t(qt_ref[t]) * (DK ** -0.5)
        kt = rope_t(kt_ref[t])
        v8 = v_ref[t]
        g8 = g_ref[t]
        for h in range(HEADS):
            s_new = s_in_ref[t, h] * gamma[h] + kt[:, h:h + 1] * v8[h:h + 1, :]
            s_out_ref[t, h] = s_new
            o = jnp.sum(qt[:, h:h + 1] * s_new, axis=0, keepdims=True)
            a_ref[t, h:h + 1, :] = _group_norm_gate(o, gnw_ref[h:h + 1, :], g8[h:h + 1, :])
        return carry

    lax.fori_loop(0, RET_S_BT, body, 0)


def _ret_sample(qt, kt, v3, g3, gn_w, state):
    _, _, _, _, gamma = _retention_consts()
    cos_t, sin_t = _rope_tables(np.array([PAST_LEN]))
    cos8 = np.ascontiguousarray(np.broadcast_to(cos_t[0][:, None], (DK, HEADS)))
    sin8 = np.ascontiguousarray(np.broadcast_to(sin_t[0][:, None], (DK, HEADS)))
    bt = RET_S_BT
    return pl.pallas_call(
        functools.partial(_ret_sample_kernel, gamma),
        grid=(T_S // bt,),
        in_specs=[pl.BlockSpec((bt, DK, HEADS), lambda i: (i, 0, 0)),
                  pl.BlockSpec((bt, DK, HEADS), lambda i: (i, 0, 0)),
                  pl.BlockSpec((bt, HEADS, DV), lambda i: (i, 0, 0)),
                  pl.BlockSpec((bt, HEADS, DV), lambda i: (i, 0, 0)),
                  pl.BlockSpec((DK, HEADS), lambda i: (0, 0)),
                  pl.BlockSpec((DK, HEADS), lambda i: (0, 0)),
                  pl.BlockSpec((HEADS, DV), lambda i: (0, 0)),
                  pl.BlockSpec((bt, HEADS, DK, DV), lambda i: (i, 0, 0, 0))],
        out_specs=[pl.BlockSpec((bt, HEADS, DV), lambda i: (i, 0, 0)),
                   pl.BlockSpec((bt, HEADS, DK, DV), lambda i: (i, 0, 0, 0))],
        out_shape=[jax.ShapeDtypeStruct((T_S, HEADS, DV), f32),
                   jax.ShapeDtypeStruct((T_S, HEADS, DK, DV), f32)],
        compiler_params=_cparams(("arbitrary",)),
        name="retention_sample",
    )(qt, kt, v3, g3, jnp.asarray(cos8), jnp.asarray(sin8), gn_w.reshape(HEADS, DV), state)


def _layer_norm(x, w, b):
    mu = jnp.mean(x, axis=-1, keepdims=True)
    var = jnp.mean(jnp.square(x - mu), axis=-1, keepdims=True)
    return (x - mu) * lax.rsqrt(var + EPS) * w + b


def _gmlp_prompt_kernel(u_ref, gv_ref, ws_ref, bst_ref, lnw_ref, lnb_ref, o_ref):
    vn = _layer_norm(jax.nn.gelu(gv_ref[...]), lnw_ref[...], lnb_ref[...]).astype(bf16)
    r = lax.broadcasted_iota(i32, (CHUNK, CHUNK), 0)
    c = lax.broadcasted_iota(i32, (CHUNK, CHUNK), 1)
    causal = r >= c
    bst = bst_ref[...]
    for g in range(GROUPS):
        w = jnp.where(causal, ws_ref[g], 0.0).astype(bf16)
        mixed = jnp.dot(w, vn[:, g * GC:(g + 1) * GC], preferred_element_type=f32) + bst[:, g:g + 1]
        o_ref[:, g * GC:(g + 1) * GC] = (jax.nn.gelu(u_ref[:, g * GC:(g + 1) * GC]) * mixed).astype(o_ref.dtype)


def _gmlp_prompt(z_all, gm_ws, gm_bs, ln_w, ln_b):
    return pl.pallas_call(
        _gmlp_prompt_kernel,
        grid=(T_P // CHUNK,),
        in_specs=[pl.BlockSpec((CHUNK, GW), lambda i: (i, COL_U)),
                  pl.BlockSpec((CHUNK, GW), lambda i: (i, COL_GV)),
                  pl.BlockSpec((GROUPS, CHUNK, CHUNK), lambda i: (0, 0, 0)),
                  pl.BlockSpec((CHUNK, GROUPS), lambda i: (0, 0)),
                  pl.BlockSpec((1, GW), lambda i: (0, 0)),
                  pl.BlockSpec((1, GW), lambda i: (0, 0))],
        out_specs=pl.BlockSpec((CHUNK, GW), lambda i: (i, 0)),
        out_shape=jax.ShapeDtypeStruct((T_P, GW), bf16),
        compiler_params=_cparams(("arbitrary",)),
        name="gmlp_prompt",
    )(z_all, z_all, gm_ws, gm_bs.T, ln_w.reshape(1, GW), ln_b.reshape(1, GW))


def _gmlp_sample_kernel(u_ref, gv_ref, w0_ref, b0_ref, lnw_ref, lnb_ref, vn_ref, o_ref):
    vn = _layer_norm(jax.nn.gelu(gv_ref[...]), lnw_ref[...], lnb_ref[...])
    vn_ref[...] = vn
    o_ref[...] = (jax.nn.gelu(u_ref[...]) * (vn * w0_ref[...] + b0_ref[...])).astype(o_ref.dtype)


def _gmlp_sample(z_all, gm_ws, gm_bs, ln_w, ln_b):
    w0 = jnp.repeat(gm_ws[:, 0, 0], GC).reshape(1, GW)
    b0 = jnp.repeat(gm_bs[:, 0], GC).reshape(1, GW)
    blk = T_P // T_S
    return pl.pallas_call(
        _gmlp_sample_kernel,
        grid=(1,),
        in_specs=[pl.BlockSpec((T_S, GW), lambda i: (blk, COL_U)),
                  pl.BlockSpec((T_S, GW), lambda i: (blk, COL_GV)),
                  pl.BlockSpec((1, GW), lambda i: (0, 0)),
                  pl.BlockSpec((1, GW), lambda i: (0, 0)),
                  pl.BlockSpec((1, GW), lambda i: (0, 0)),
                  pl.BlockSpec((1, GW), lambda i: (0, 0))],
        out_specs=[pl.BlockSpec((T_S, GW), lambda i: (0, 0)),
                   pl.BlockSpec((T_S, GW), lambda i: (0, 0))],
        out_shape=[jax.ShapeDtypeStruct((T_S, GW), f32),
                   jax.ShapeDtypeStruct((T_S, GW), bf16)],
        compiler_params=_cparams(("arbitrary",)),
        name="gmlp_sample",
    )(z_all, z_all, w0, b0, ln_w.reshape(1, GW), ln_b.reshape(1, GW))


def _merge_kernel(a_ref, b_ref, ga_ref, gb_ref, woa_ref, wob_ref, o_ref):
    ya = jnp.dot(a_ref[...].astype(bf16), woa_ref[...], preferred_element_type=f32)
    yb = jnp.dot(b_ref[...].astype(bf16), wob_ref[...], preferred_element_type=f32)
    o_ref[...] = (jax.nn.sigmoid(ga_ref[...]) * ya + jax.nn.sigmoid(gb_ref[...]) * yb).astype(o_ref.dtype)


def _merge(a, bm, z_all, w_oa, w_ob, tm, z_blk0):
    m = a.shape[0]
    resident = lambda: pl.BlockSpec((D, D), lambda i: (0, 0), pipeline_mode=pl.Buffered(1))
    return pl.pallas_call(
        _merge_kernel,
        grid=(m // tm,),
        in_specs=[pl.BlockSpec((tm, D), lambda i: (i, 0)),
                  pl.BlockSpec((tm, D), lambda i: (i, 0)),
                  pl.BlockSpec((tm, D), lambda i: (z_blk0 + i, COL_GA)),
                  pl.BlockSpec((tm, D), lambda i: (z_blk0 + i, COL_GB)),
                  resident(), resident()],
        out_specs=pl.BlockSpec((tm, D), lambda i: (i, 0)),
        out_shape=jax.ShapeDtypeStruct((m, D), bf16),
        compiler_params=_cparams(("arbitrary",)),
        name="merge",
    )(a, bm, z_all, z_all, w_oa, w_ob)


def _post_kernel(m_ref, x_ref, g1_ref, sc_ref, sh_ref, nw_ref, wout_ref, wr_ref, br_ref,
                 x1_ref, h2_ref, idx_ref, gate_ref, rank_ref, cnt_ref):
    i = pl.program_id(0)

    @pl.when(i == 0)
    def _():
        cnt_ref[...] = jnp.zeros_like(cnt_ref)

    tm = m_ref.shape[0]
    y = jnp.dot(m_ref[...], wout_ref[...], preferred_element_type=f32)
    x1 = x_ref[...] + g1_ref[...] * y
    x1_ref[...] = x1
    xn = x1 * lax.rsqrt(jnp.mean(x1 * x1, axis=-1, keepdims=True) + EPS) * nw_ref[...]
    h2 = xn * (1.0 + sc_ref[...]) + sh_ref[...]
    h2_ref[...] = h2

    h_hi = h2.astype(bf16)
    h_lo = (h2 - h_hi.astype(f32)).astype(bf16)
    p_hi = jnp.dot(h_hi, wr_ref[...], preferred_element_type=f32)
    p_lo = jnp.dot(h_lo, wr_ref[...], preferred_element_type=f32)
    logits = p_hi[:, :LANES] + p_hi[:, LANES:] + p_lo[:, :LANES] + br_ref[...]

    lane = lax.broadcasted_iota(i32, (tm, LANES), 1).astype(f32)
    col = lax.broadcasted_iota(i32, (tm, TOP_K), 1)
    work = jnp.where(lane < N_EXPERTS, logits, -jnp.inf)
    member = jnp.zeros((tm, LANES), f32)
    vals, sels = [], []
    idx_out = jnp.zeros((tm, TOP_K), i32)
    for k in range(TOP_K):
        mx = jnp.max(work, axis=1, keepdims=True)
        ix = jnp.min(jnp.where(work == mx, lane, float(LANES)), axis=1, keepdims=True)
        sel = lane == ix
        vals.append(mx)
        sels.append(sel)
        idx_out = jnp.where(col == k, ix.astype(i32), idx_out)
        member = jnp.where(sel, 1.0, member)
        work = jnp.where(sel, -jnp.inf, work)
    idx_ref[...] = idx_out

    exps = [jnp.exp(v - vals[0]) for v in vals]
    den = exps[0] + exps[1] + exps[2] + exps[3]
    gate_out = jnp.zeros((tm, TOP_K), f32)
    for k in range(TOP_K):
        gate_out = jnp.where(col == k, exps[k] / den, gate_out)
    gate_ref[...] = gate_out

    r = lax.broadcasted_iota(i32, (tm, tm), 0)
    c = lax.broadcasted_iota(i32, (tm, tm), 1)
    lower = jnp.where(c < r, 1.0, 0.0).astype(bf16)
    before = jnp.dot(lower, member.astype(bf16), preferred_element_type=f32) + cnt_ref[...]
    rank_out = jnp.zeros((tm, TOP_K), i32)
    for k in range(TOP_K):
        rk = jnp.sum(jnp.where(sels[k], before, 0.0), axis=1, keepdims=True)
        rank_out = jnp.where(col == k, rk.astype(i32), rank_out)
    rank_ref[...] = rank_out
    cnt_ref[...] += jnp.sum(member, axis=0, keepdims=True)


def _post(m, x, mod, mod_is_rows, nw, w_out, w_router, b_router, tm, per_b):
    mm = m.shape[0]
    if mod_is_rows:
        mspec = lambda chunk: pl.BlockSpec((tm, D), lambda i: (i, chunk))
    else:
        mspec = lambda chunk: pl.BlockSpec((None, 1, D), lambda i: (i // per_b, 0, chunk))
    in_specs = [pl.BlockSpec((tm, D), lambda i: (i, 0)),
                pl.BlockSpec((tm, D), lambda i: (i, 0)),
                mspec(MOD_G1), mspec(MOD_SC2), mspec(MOD_SH2),
                pl.BlockSpec((1, D), lambda i: (0, 0)),
                pl.BlockSpec((D, D), lambda i: (0, 0), pipeline_mode=pl.Buffered(1)),
                pl.BlockSpec((D, 2 * LANES), lambda i: (0, 0)),
                pl.BlockSpec((1, LANES), lambda i: (0, 0))]
    args = [m, x, mod, mod, mod, nw, w_out, w_router, b_router]
    return pl.pallas_call(
        _post_kernel,
        grid=(mm // tm,),
        in_specs=in_specs,
        out_specs=[pl.BlockSpec((tm, D), lambda i: (i, 0)),
                   pl.BlockSpec((tm, D), lambda i: (i, 0)),
                   pl.BlockSpec((tm, TOP_K), lambda i: (i, 0)),
                   pl.BlockSpec((tm, TOP_K), lambda i: (i, 0)),
                   pl.BlockSpec((tm, TOP_K), lambda i: (i, 0)),
                   pl.BlockSpec((1, LANES), lambda i: (0, 0))],
        out_shape=[jax.ShapeDtypeStruct((mm, D), f32),
                   jax.ShapeDtypeStruct((mm, D), f32),
                   jax.ShapeDtypeStruct((mm, TOP_K), i32),
                   jax.ShapeDtypeStruct((mm, TOP_K), f32),
                   jax.ShapeDtypeStruct((mm, TOP_K), i32),
                   jax.ShapeDtypeStruct((1, LANES), f32)],
        compiler_params=_cparams(("arbitrary",)),
        name="post_mixer",
    )(*args)


DISP_TB = 128


DISP_PT = T_P // DISP_TB


def _dispatch_kernel(zblk_ref, dest_ref, h2p_ref, h2s_ref, xbuf_ref, zero_ref, sem):
    i = pl.program_id(0)

    def row_copy(src_ref, t, d):
        return pltpu.make_async_copy(src_ref.at[pl.ds(t, 1)], xbuf_ref.at[pl.ds(d, 1)], sem)

    def zero_copy(blk):
        return pltpu.make_async_copy(zero_ref, xbuf_ref.at[pl.ds(blk * MOE_BM, MOE_BM)], sem)

    @pl.when(i == 0)
    def _():
        zero_ref[...] = jnp.zeros_like(zero_ref)
        used = zblk_ref[N_EXPERTS]

        def zstart(e, c):
            zero_copy(zblk_ref[e]).start()
            return c

        def tstart(b, c):
            zero_copy(b).start()
            return c

        def zwait(e, c):
            zero_copy(0).wait()
            return c

        lax.fori_loop(0, N_EXPERTS, zstart, 0)
        lax.fori_loop(used, MOE_NBLK, tstart, 0)
        lax.fori_loop(0, N_EXPERTS, zwait, 0)
        lax.fori_loop(used, MOE_NBLK, zwait, 0)

    def issue(src_ref):
        for t in range(DISP_TB):
            for k in range(TOP_K):
                row_copy(src_ref, t, dest_ref[0, 0, t * TOP_K + k]).start(priority=k % 2)

    @pl.when(i < DISP_PT)
    def _():
        issue(h2p_ref)

    @pl.when(i == DISP_PT)
    def _():
        issue(h2s_ref)

    def wait_group(gi, c):
        for _ in range(SUBLANES * TOP_K):
            row_copy(h2p_ref, 0, 0).wait()
        return c

    lax.fori_loop(0, DISP_TB // SUBLANES, wait_group, 0)


def _dispatch(zero_blk, dest_all, h2_p, h2_s):
    nt = T_ALL // DISP_TB
    return pl.pallas_call(
        _dispatch_kernel,
        grid_spec=pltpu.PrefetchScalarGridSpec(
            num_scalar_prefetch=1,
            grid=(nt,),
            in_specs=[pl.BlockSpec((1, 1, DISP_TB * TOP_K), lambda i, zb: (i, 0, 0), memory_space=pltpu.SMEM),
                      pl.BlockSpec((DISP_TB, D), lambda i, zb: (jnp.minimum(i, DISP_PT - 1), 0)),
                      pl.BlockSpec((DISP_TB, D), lambda i, zb: (0, 0))],
            out_specs=pl.BlockSpec(memory_space=pl.ANY),
            scratch_shapes=[pltpu.VMEM((MOE_BM, D), f32), pltpu.SemaphoreType.DMA(())]),
        out_shape=jax.ShapeDtypeStruct((MOE_NROWS, D), f32),
        compiler_params=_cparams(("arbitrary",)),
        name="moe_dispatch",
    )(zero_blk, dest_all.reshape(nt, 1, DISP_TB * TOP_K), h2_p, h2_s)


def _moe_ffn_kernel(e_ref, sb_ref, nb_ref, xbuf_ref, wg_ref, wu_ref, bg_ref, bu_ref, wd_ref, bd_ref,
                    ybuf_ref, x_ref, acc_ref, wgb_ref, wub_ref, wdb_ref, sem_in, sem_out):
    p = pl.program_id(0)
    j = pl.program_id(1)
    nb = nb_ref[p]
    sb = sb_ref[p]

    def in_copy(r):
        return pltpu.make_async_copy(xbuf_ref.at[pl.ds((sb + r) * MOE_BM, MOE_BM)],
                                     x_ref.at[pl.ds(r * MOE_BM, MOE_BM)], sem_in.at[r])

    def out_copy(r):
        return pltpu.make_async_copy(acc_ref.at[pl.ds(r * MOE_BM, MOE_BM)],
                                     ybuf_ref.at[pl.ds((sb + r) * MOE_BM, MOE_BM)], sem_out)

    def for_blocks(fn):
        def body(r, c):
            fn(r)
            return c
        lax.fori_loop(0, nb, body, 0)

    @pl.when((p == 0) & (j == 0))
    def _():
        used = nb_ref[MOE_PMAX]
        acc_ref[0:MOE_BM, :] = jnp.zeros((MOE_BM, D), f32)

        def tail_copy(b):
            return pltpu.make_async_copy(acc_ref.at[pl.ds(0, MOE_BM)],
                                         ybuf_ref.at[pl.ds(b * MOE_BM, MOE_BM)], sem_out)

        def tstart(b, c):
            tail_copy(b).start()
            return c

        def twait(b, c):
            tail_copy(0).wait()
            return c

        lax.fori_loop(used, MOE_NBLK, tstart, 0)
        lax.fori_loop(used, MOE_NBLK, twait, 0)

    first = j == 0
    last = j == MOE_J - 1

    @pl.when(nb > 0)
    def _():
        @pl.when(first)
        def _():
            for_blocks(lambda r: in_copy(r).start())
            bd = jnp.broadcast_to(bd_ref[...], (MOE_BM, D))

            def init(r):
                acc_ref[pl.ds(pl.multiple_of(r * MOE_BM, MOE_BM), MOE_BM), :] = bd
            for_blocks(init)

        bg = bg_ref[...]
        bu = bu_ref[...]

        def cast_weights():
            wgb_ref[...] = wg_ref[...].astype(bf16)
            wub_ref[...] = wu_ref[...].astype(bf16)
            wdb_ref[...] = wd_ref[...].astype(bf16)

        def ffn_rows(b0, n_blk):
            rows = pl.ds(pl.multiple_of(b0 * MOE_BM, MOE_BM), n_blk * MOE_BM)
            x = x_ref[rows, :].astype(bf16)
            gate = jnp.dot(x, wgb_ref[...], preferred_element_type=f32) + bg
            up = jnp.dot(x, wub_ref[...], preferred_element_type=f32) + bu
            gate = jnp.minimum(gate, SWIGLU_LIMIT)
            up = jnp.clip(up, -SWIGLU_LIMIT, SWIGLU_LIMIT)
            act = (up + 1.0) * (gate * jax.nn.sigmoid(gate * SWIGLU_ALPHA))
            acc_ref[rows, :] += jnp.dot(act.astype(bf16), wdb_ref[...], preferred_element_type=f32)

        def trip(b0, groups, with_cast=False):
            n_blk = sum(groups)

            @pl.when(first)
            def _():
                for b in range(n_blk):
                    in_copy(b0 + b).wait()

            if with_cast:
                cast_weights()
            off = 0
            for g in groups:
                ffn_rows(b0 + off, g)
                off += g

            @pl.when(last)
            def _():
                for b in range(n_blk):
                    out_copy(b0 + b).start()

        full = (2, 2, 2)
        per_trip = sum(full)
        ntrip = sum((nb >= per_trip * m).astype(i32) for m in range(1, MOE_RB // per_trip + 1))
        rem = nb - ntrip * per_trip
        head = jnp.where(rem == 0, per_trip, rem)
        for n_head, groups in ((1, (1,)), (2, (2,)), (3, (2, 1)), (4, (2, 2)), (5, (2, 2, 1)), (per_trip, full)):
            @pl.when(head == n_head)
            def _(groups=groups):
                trip(0, groups, with_cast=True)

        def full_trip(q, c):
            trip(head + q * per_trip, full)
            return c
        lax.fori_loop(0, ntrip - (rem == 0).astype(i32), full_trip, 0)

        @pl.when(last)
        def _():
            for_blocks(lambda r: out_copy(r).wait())


def _moe_ffn(e_arr, sb_arr, nb_arr, xbuf, w_gate_up, b_gate_up, w_down, b_down):
    def jj(p, j, nb):
        return jnp.where(nb[p] > 0, j, MOE_J - 1)

    in_specs = [
        pl.BlockSpec(memory_space=pl.ANY),
        pl.BlockSpec((None, D, MOE_TF), lambda p, j, e, sb, nb: (e[p], 0, jj(p, j, nb))),
        pl.BlockSpec((None, D, MOE_TF), lambda p, j, e, sb, nb: (e[p], 0, MOE_J + jj(p, j, nb))),
        pl.BlockSpec((None, 1, MOE_TF), lambda p, j, e, sb, nb: (e[p], 0, jj(p, j, nb))),
        pl.BlockSpec((None, 1, MOE_TF), lambda p, j, e, sb, nb: (e[p], 0, MOE_J + jj(p, j, nb))),
        pl.BlockSpec((None, MOE_TF, D), lambda p, j, e, sb, nb: (e[p], jj(p, j, nb), 0)),
        pl.BlockSpec((None, 1, D), lambda p, j, e, sb, nb: (e[p], 0, 0)),
    ]
    return pl.pallas_call(
        _moe_ffn_kernel,
        grid_spec=pltpu.PrefetchScalarGridSpec(
            num_scalar_prefetch=3,
            grid=(MOE_PMAX, MOE_J),
            in_specs=in_specs,
            out_specs=pl.BlockSpec(memory_space=pl.ANY),
            scratch_shapes=[pltpu.VMEM((MOE_RB * MOE_BM, D), f32),
                            pltpu.VMEM((MOE_RB * MOE_BM, D), f32),
                            pltpu.VMEM((D, MOE_TF), bf16),
                            pltpu.VMEM((D, MOE_TF), bf16),
                            pltpu.VMEM((MOE_TF, D), bf16),
                            pltpu.SemaphoreType.DMA((MOE_RB,)),
                            pltpu.SemaphoreType.DMA(())]),
        out_shape=jax.ShapeDtypeStruct((MOE_NROWS, D), f32),
        compiler_params=_cparams(("arbitrary", "arbitrary"), vmem=56 * 1024 * 1024),
        name="moe_ffn",
    )(e_arr, sb_arr, nb_arr, xbuf, w_gate_up, w_gate_up,
      b_gate_up.reshape(N_EXPERTS, 1, 2 * D_EXPERT), b_gate_up.reshape(N_EXPERTS, 1, 2 * D_EXPERT),
      w_down, b_down.reshape(N_EXPERTS, 1, D))


COMB_TB = 128


def _combine_kernel(dcur_ref, dnext_ref, ybuf_ref, gate_ref, x1_ref, g2_ref, fw_ref, o_ref, rows_ref, sem):
    i = pl.program_id(0)
    slot = i & 1

    def row_copy(buf, n, d):
        return pltpu.make_async_copy(ybuf_ref.at[pl.ds(d, 1)], rows_ref.at[buf, pl.ds(n, 1)], sem.at[buf])

    def wait_tile(buf):
        def wait_group(gi, c):
            for _ in range(SUBLANES * TOP_K):
                row_copy(buf, 0, 0).wait()
            return c
        lax.fori_loop(0, COMB_TB // SUBLANES, wait_group, 0)

    @pl.when(i == 0)
    def _():
        def start_group(gi, c):
            for tt in range(SUBLANES):
                t = pl.multiple_of(gi * SUBLANES, SUBLANES) + tt
                for k in range(TOP_K):
                    d = dcur_ref[0, 0, (gi * SUBLANES + tt) * TOP_K + k]
                    row_copy(0, k * COMB_TB + t, d).start(priority=k % 2)
            return c
        lax.fori_loop(0, COMB_TB // SUBLANES, start_group, 0)

    wait_tile(slot)

    for t in range(COMB_TB):
        for k in range(TOP_K):
            row_copy(1 - slot, k * COMB_TB + t, dnext_ref[0, 0, t * TOP_K + k]).start(priority=k % 2)

    gates = gate_ref[...]
    f = gates[:, 0:1] * rows_ref[slot, 0:COMB_TB, :]
    for k in range(1, TOP_K):
        f = f + gates[:, k:k + 1] * rows_ref[slot, k * COMB_TB:(k + 1) * COMB_TB, :]
    x2 = x1_ref[...] + g2_ref[...] * f
    o_ref[...] = x2 * lax.rsqrt(jnp.mean(x2 * x2, axis=-1, keepdims=True) + EPS) * fw_ref[...]

    @pl.when(i == pl.num_programs(0) - 1)
    def _():
        wait_tile(1 - slot)


def _combine(dest, ybuf, gates, x1, mod, mod_is_rows, per_b, final_w):
    mm = x1.shape[0]
    nt = mm // COMB_TB
    dest3 = dest.reshape(nt, 1, COMB_TB * TOP_K)
    if mod_is_rows:
        g2spec = pl.BlockSpec((COMB_TB, D), lambda i: (i, MOD_G2))
    else:
        g2spec = pl.BlockSpec((None, 1, D), lambda i: (i // per_b, 0, MOD_G2))
    return pl.pallas_call(
        _combine_kernel,
        grid=(nt,),
        in_specs=[pl.BlockSpec((1, 1, COMB_TB * TOP_K), lambda i: (i, 0, 0), memory_space=pltpu.SMEM),
                  pl.BlockSpec((1, 1, COMB_TB * TOP_K), lambda i: (jnp.minimum(i + 1, nt - 1), 0, 0),
                               memory_space=pltpu.SMEM),
                  pl.BlockSpec(memory_space=pl.ANY),
                  pl.BlockSpec((COMB_TB, TOP_K), lambda i: (i, 0)),
                  pl.BlockSpec((COMB_TB, D), lambda i: (i, 0)),
                  g2spec,
                  pl.BlockSpec((1, D), lambda i: (0, 0))],
        out_specs=pl.BlockSpec((COMB_TB, D), lambda i: (i, 0)),
        out_shape=jax.ShapeDtypeStruct((mm, D), f32),
        scratch_shapes=[pltpu.VMEM((2, TOP_K * COMB_TB, D), f32), pltpu.SemaphoreType.DMA((2,))],
        compiler_params=_cparams(("arbitrary",)),
        name="moe_combine",
    )(dest3, dest3, ybuf, gates, x1, mod, final_w.reshape(1, D))


def _routing_tables(cnt_p, cnt_s, idx_p, rank_p, idx_s, rank_s):
    cp = cnt_p[0, :N_EXPERTS].astype(i32)
    cs = cnt_s[0, :N_EXPERTS].astype(i32)
    nblk = (cp + cs + MOE_BM - 1) // MOE_BM
    blk_end = jnp.cumsum(nblk)
    blk_start = blk_end - nblk
    row_start = blk_start * MOE_BM
    dest_p = row_start[idx_p.reshape(-1)] + rank_p.reshape(-1)
    dest_s = (row_start + cp)[idx_s.reshape(-1)] + rank_s.reshape(-1)
    used = blk_end[-1:]
    zero_blk = jnp.concatenate([jnp.maximum(blk_end - 1, 0), used]).astype(i32)
    npass = (nblk + MOE_RB - 1) // MOE_RB
    pass_end = jnp.cumsum(npass)
    total = pass_end[-1]
    pid = jnp.arange(MOE_PMAX, dtype=i32)
    pid_c = jnp.minimum(pid, total - 1)
    pe = jnp.minimum(jnp.sum((pass_end[None, :] <= pid_c[:, None]).astype(i32), axis=1), N_EXPERTS - 1)
    local = pid_c - (pass_end - npass)[pe]
    active = pid < total
    sb = jnp.where(active, blk_start[pe] + local * MOE_RB, 0).astype(i32)
    nb = jnp.where(active, jnp.clip(nblk[pe] - local * MOE_RB, 0, MOE_RB), 0)
    nb = jnp.concatenate([nb, used]).astype(i32)
    return dest_p.astype(i32), dest_s.astype(i32), zero_blk, pe, sb, nb


def kernel(x_prompt, x_sample, c_prompt, c_sample, state_ret, norm1_w, norm2_w, w_ada, b_ada, w_in,
           ret_gn_w, gm_ln_w, gm_ln_b, gm_ws, gm_bs, w_oa, w_ob, w_out, w_router, b_router,
           w_gate_up, b_gate_up, w_down, b_down, final_norm_w):
    x_p = x_prompt.reshape(T_P, D)
    x_s = x_sample.reshape(T_S, D)
    per_b = lambda tm: SEQ // tm

    mod = _ada(jnp.concatenate([c_prompt, c_sample], axis=0), w_ada[0], b_ada[0])
    mod_p = mod[:BATCH].reshape(BATCH, 1, 6 * D)
    mod_s = mod[BATCH:]

    n1 = norm1_w[0].reshape(1, D)
    h_all = _norm1(x_p, x_s, n1, mod_p, mod_s)
    z_all = _proj_in(h_all, w_in[0])

    a_p, s_p = _ret_prompt(z_all, ret_gn_w[0])
    z_s = z_all[T_P:]
    qt = z_s[:, :HEADS * DK].reshape(T_S, HEADS, DK).transpose(0, 2, 1)
    kt = z_s[:, HEADS * DK:2 * HEADS * DK].reshape(T_S, HEADS, DK).transpose(0, 2, 1)
    v3 = z_s[:, 2 * HEADS * DK:2 * HEADS * DK + HEADS * DV].reshape(T_S, HEADS, DV)
    g3 = z_s[:, 2 * HEADS * DK + HEADS * DV:2 * HEADS * DK + 2 * HEADS * DV].reshape(T_S, HEADS, DV)
    a_s, s_s = _ret_sample(qt, kt, v3, g3, ret_gn_w[0], state_ret[0])
    bm_p = _gmlp_prompt(z_all, gm_ws[0], gm_bs[0], gm_ln_w[0], gm_ln_b[0])
    vn_s, bm_s = _gmlp_sample(z_all, gm_ws[0], gm_bs[0], gm_ln_w[0], gm_ln_b[0])

    woa, wob, wout = w_oa[0].astype(bf16), w_ob[0].astype(bf16), w_out[0].astype(bf16)
    m_p = _merge(a_p, bm_p, z_all, woa, wob, 256, 0)
    m_s = _merge(a_s.reshape(T_S, HEADS * DV), bm_s, z_all, woa, wob, T_S, T_P // T_S)

    n2 = norm2_w[0].reshape(1, D)
    wr_hi = w_router[0].astype(bf16)
    wr_lo = (w_router[0] - wr_hi.astype(f32)).astype(bf16)
    lane_pad = ((0, 0), (0, LANES - N_EXPERTS))
    wr_cat = jnp.concatenate([jnp.pad(wr_hi, lane_pad), jnp.pad(wr_lo, lane_pad)], axis=1)
    br = jnp.pad(b_router[0].reshape(1, N_EXPERTS), lane_pad)
    x1_p, h2_p, idx_p, gate_p, rank_p, cnt_p = _post(
        m_p, x_p, mod_p, False, n2, wout, wr_cat, br, 512, per_b(512))
    x1_s, h2_s, idx_s, gate_s, rank_s, cnt_s = _post(
        m_s, x_s, mod_s, True, n2, wout, wr_cat, br, T_S, 1)

    dest_p, dest_s, zero_blk, pe, sb, nb = _routing_tables(cnt_p, cnt_s, idx_p, rank_p, idx_s, rank_s)
    xbuf = _dispatch(zero_blk, jnp.concatenate([dest_p, dest_s], axis=0), h2_p, h2_s)
    ybuf = _moe_ffn(pe, sb, nb, xbuf, w_gate_up[0], b_gate_up[0], w_down[0], b_down[0])
    y_p = _combine(dest_p, ybuf, gate_p, x1_p, mod_p, False, per_b(COMB_TB), final_norm_w)
    y_s = _combine(dest_s, ybuf, gate_s, x1_s, mod_s, True, 1, final_norm_w)

    return (y_p.reshape(BATCH, SEQ, D),
            y_s.reshape(DEC_BATCH, 1, D),
            s_p.reshape(1, BATCH, HEADS, DK, DV),
            s_s.reshape(1, DEC_BATCH, HEADS, DK, DV),
            vn_s.reshape(1, DEC_BATCH, 1, GW))
```

```python
import functools

import numpy as np
import jax
import jax.numpy as jnp
from jax import lax
from jax.experimental import pallas as pl
from jax.experimental.pallas import tpu as pltpu

f32 = jnp.float32
bf16 = jnp.bfloat16
i32 = jnp.int32

D = 2048
BATCH = 4
SEQ = 2048
DEC_BATCH = 128
PAST_LEN = 16384
HEADS = 8
DK = 128
DV = 256
CHUNK = 128
ROPE_THETA = 10000.0
GROUPS = 8
GW = 2048
GC = GW // GROUPS
N_EXPERTS = 32
TOP_K = 4
D_EXPERT = 2048
SWIGLU_LIMIT = 7.0
SWIGLU_ALPHA = 1.702
EPS = 1e-6

T_P = BATCH * SEQ
T_S = DEC_BATCH
T_ALL = T_P + T_S
D_IN = 14336
COL_Q, COL_K = 0, 1
COL_V, COL_G, COL_U, COL_GV, COL_GA, COL_GB = 1, 2, 3, 4, 5, 6
MOD_SH1, MOD_SC1, MOD_G1, MOD_SH2, MOD_SC2, MOD_G2 = range(6)

LANES = 128
SUBLANES = 8
VMEM_LIMIT = 48 * 1024 * 1024

MOE_BM = 128
MOE_RB = 12
MOE_TF = 256
MOE_J = D_EXPERT // MOE_TF
TK_ALL = T_ALL * TOP_K
MOE_NBLK = (TK_ALL + N_EXPERTS * (MOE_BM - 1)) // MOE_BM
MOE_NROWS = MOE_NBLK * MOE_BM
MOE_PMAX = N_EXPERTS + (MOE_NBLK - N_EXPERTS) // MOE_RB


def _cparams(sem, vmem=VMEM_LIMIT):
    return pltpu.CompilerParams(dimension_semantics=sem, vmem_limit_bytes=vmem)


def _ada_kernel(c_ref, w_ref, b_ref, o_ref):
    o_ref[...] = jnp.dot(c_ref[...].astype(bf16), w_ref[...].astype(bf16),
                         preferred_element_type=f32) + b_ref[...]


def _ada(c_all, w_ada, b_ada):
    m = c_all.shape[0]
    tn = 1024
    return pl.pallas_call(
        _ada_kernel,
        grid=(6 * D // tn,),
        in_specs=[pl.BlockSpec((m, D), lambda j: (0, 0)),
                  pl.BlockSpec((D, tn), lambda j: (0, j)),
                  pl.BlockSpec((1, tn), lambda j: (0, j))],
        out_specs=pl.BlockSpec((m, tn), lambda j: (0, j)),
        out_shape=jax.ShapeDtypeStruct((m, 6 * D), f32),
        compiler_params=_cparams(("arbitrary",)),
        name="ada_mod",
    )(c_all, w_ada, b_ada.reshape(1, 6 * D))


def _norm_mod_kernel(x_ref, nw_ref, sc_ref, sh_ref, o_ref):
    x = x_ref[...]
    y = x * lax.rsqrt(jnp.mean(x * x, axis=-1, keepdims=True) + EPS) * nw_ref[...]
    o_ref[...] = (y * (1.0 + sc_ref[...]) + sh_ref[...]).astype(o_ref.dtype)


NORM1_TM = T_S
NORM1_PT = T_P // NORM1_TM


def _norm1_kernel(xp_ref, xs_ref, nw_ref, scp_ref, shp_ref, scs_ref, shs_ref, o_ref):
    i = pl.program_id(0)

    @pl.when(i < NORM1_PT)
    def _():
        _norm_mod_kernel(xp_ref, nw_ref, scp_ref, shp_ref, o_ref)

    @pl.when(i == NORM1_PT)
    def _():
        _norm_mod_kernel(xs_ref, nw_ref, scs_ref, shs_ref, o_ref)


def _norm1(x_p, x_s, nw, mod_p, mod_s):
    tm = NORM1_TM
    per_b = SEQ // tm
    pt = lambda i: jnp.minimum(i, NORM1_PT - 1)
    return pl.pallas_call(
        _norm1_kernel,
        grid=(NORM1_PT + 1,),
        in_specs=[pl.BlockSpec((tm, D), lambda i: (pt(i), 0)),
                  pl.BlockSpec((tm, D), lambda i: (0, 0)),
                  pl.BlockSpec((1, D), lambda i: (0, 0)),
                  pl.BlockSpec((None, 1, D), lambda i: (pt(i) // per_b, 0, MOD_SC1)),
                  pl.BlockSpec((None, 1, D), lambda i: (pt(i) // per_b, 0, MOD_SH1)),
                  pl.BlockSpec((tm, D), lambda i: (0, MOD_SC1)),
                  pl.BlockSpec((tm, D), lambda i: (0, MOD_SH1))],
        out_specs=pl.BlockSpec((tm, D), lambda i: (i, 0)),
        out_shape=jax.ShapeDtypeStruct((T_ALL, D), bf16),
        compiler_params=_cparams(("arbitrary",)),
        name="norm1",
    )(x_p, x_s, nw, mod_p, mod_p, mod_s, mod_s)


def _proj_in_kernel(h_ref, w_ref, o_ref, wb_ref):
    @pl.when(pl.program_id(1) == 0)
    def _():
        wb_ref[...] = w_ref[...].astype(bf16)

    o_ref[...] = jnp.dot(h_ref[...], wb_ref[...], preferred_element_type=f32)


def _proj_in(h_all, w_in):
    tm, tn = 640, 1024
    return pl.pallas_call(
        _proj_in_kernel,
        grid=(D_IN // tn, T_ALL // tm),
        in_specs=[pl.BlockSpec((tm, D), lambda j, i: (i, 0)),
                  pl.BlockSpec((D, tn), lambda j, i: (0, j))],
        out_specs=pl.BlockSpec((tm, tn), lambda j, i: (i, j)),
        out_shape=jax.ShapeDtypeStruct((T_ALL, D_IN), f32),
        scratch_shapes=[pltpu.VMEM((D, tn), bf16)],
        compiler_params=_cparams(("arbitrary", "arbitrary")),
        name="proj_in",
    )(h_all, w_in)


def _retention_consts():
    h = np.arange(HEADS, dtype=np.float64)
    lg = np.log1p(-np.exp2(-5.0 - h))
    idx = np.arange(CHUNK, dtype=np.float64)
    diff = idx[:, None] - idx[None, :]
    intra = np.where(diff >= 0, np.exp(np.maximum(diff, 0.0)[None] * lg[:, None, None]), 0.0)
    q_dec = np.exp((idx + 1.0)[:, None] * lg[None, :])
    k_dec = np.exp((CHUNK - 1.0 - idx)[:, None] * lg[None, :])
    s_dec = np.exp(CHUNK * lg)
    gamma = np.exp(lg)
    return (intra.astype(np.float32), q_dec.astype(np.float32), k_dec.astype(np.float32),
            [float(v) for v in s_dec], [float(v) for v in gamma])


def _rope_tables(pos):
    half = DK // 2
    freq = ROPE_THETA ** (-np.arange(half, dtype=np.float64) / half)
    ang = np.asarray(pos, dtype=np.float64)[:, None] * freq[None, :]
    cos, sin = np.cos(ang), np.sin(ang)
    cos_t = np.concatenate([cos, cos], axis=-1).astype(np.float32)
    sin_t = np.concatenate([-sin, sin], axis=-1).astype(np.float32)
    return cos_t, sin_t


def _group_norm_gate(o, gnw, g):
    mu = jnp.mean(o, axis=-1, keepdims=True)
    var = jnp.mean(jnp.square(o - mu), axis=-1, keepdims=True)
    on = (o - mu) * lax.rsqrt(var + EPS) * gnw
    return jax.nn.silu(g) * on


def _retention_chunk(s_dec, live, q_ref, k_ref, v_ref, g_ref, cos_ref, sin_ref, intra_ref, qd_ref, kd_ref,
                     gnw_ref, s_ref, a_ref, slot):
    cos = cos_ref[...]
    sin = sin_ref[...]
    qd = qd_ref[...]
    kd = kd_ref[...]
    for h in range(HEADS):
        qh = q_ref[:, h * DK:(h + 1) * DK]
        kh = k_ref[:, h * DK:(h + 1) * DK]
        qh = (qh * cos + pltpu.roll(qh, DK // 2, 1) * sin) * (DK ** -0.5)
        kh = kh * cos + pltpu.roll(kh, DK // 2, 1) * sin
        vh = v_ref[:, h * DV:(h + 1) * DV].astype(bf16)
        s_old = s_ref[h]
        scores = lax.dot_general(qh.astype(bf16), kh.astype(bf16), (((1,), (1,)), ((), ())),
                                 preferred_element_type=f32) * intra_ref[h]
        o = jnp.dot(scores.astype(bf16), vh, preferred_element_type=f32)
        o = o + jnp.dot((qh * qd[:, h:h + 1]).astype(bf16), s_old.astype(bf16), preferred_element_type=f32)
        kw_t = (kh * kd[:, h:h + 1]).T.astype(bf16)
        s_new = s_old * s_dec[h] + jnp.dot(kw_t, vh, preferred_element_type=f32)
        s_ref[h] = jnp.where(live, s_new, s_old)
        a_ref[slot, :, h * DV:(h + 1) * DV] = _group_norm_gate(
            o, gnw_ref[:, h * DV:(h + 1) * DV], g_ref[:, h * DV:(h + 1) * DV]).astype(a_ref.dtype)


RET_S_BT = 4


def _ret_sample_kernel(gamma, qt_ref, kt_ref, v_ref, g_ref, cos_ref, sin_ref, gnw_ref, s_in_ref,
                       a_ref, s_out_ref):
    cos = cos_ref[...]
    sin = sin_ref[...]

    def rope_t(x):
        rolled = jnp.concatenate([x[DK // 2:], x[:DK // 2]], axis=0)
        return x * cos + rolled * sin

    def body(t, carry):
        qt = rope_t(qt_ref[t]) * (DK ** -0.5)
        kt = rope_t(kt_ref[t])
        v8 = v_ref[t]
        g8 = g_ref[t]
        for h in range(HEADS):
            s_new = s_in_ref[t, h] * gamma[h] + kt[:, h:h + 1] * v8[h:h + 1, :]
            s_out_ref[t, h] = s_new
            o = jnp.sum(qt[:, h:h + 1] * s_new, axis=0, keepdims=True)
            a_ref[t, h:h + 1, :] = _group_norm_gate(o, gnw_ref[h:h + 1, :], g8[h:h + 1, :])
        return carry

    lax.fori_loop(0, RET_S_BT, body, 0)


def _ret_sample(qt, kt, v3, g3, gn_w, state):
    _, _, _, _, gamma = _retention_consts()
    cos_t, sin_t = _rope_tables(np.array([PAST_LEN]))
    cos8 = np.ascontiguousarray(np.broadcast_to(cos_t[0][:, None], (DK, HEADS)))
    sin8 = np.ascontiguousarray(np.broadcast_to(sin_t[0][:, None], (DK, HEADS)))
    bt = RET_S_BT
    return pl.pallas_call(
        functools.partial(_ret_sample_kernel, gamma),
        grid=(T_S // bt,),
        in_specs=[pl.BlockSpec((bt, DK, HEADS), lambda i: (i, 0, 0)),
                  pl.BlockSpec((bt, DK, HEADS), lambda i: (i, 0, 0)),
                  pl.BlockSpec((bt, HEADS, DV), lambda i: (i, 0, 0)),
                  pl.BlockSpec((bt, HEADS, DV), lambda i: (i, 0, 0)),
                  pl.BlockSpec((DK, HEADS), lambda i: (0, 0)),
                  pl.BlockSpec((DK, HEADS), lambda i: (0, 0)),
                  pl.BlockSpec((HEADS, DV), lambda i: (0, 0)),
                  pl.BlockSpec((bt, HEADS, DK, DV), lambda i: (i, 0, 0, 0))],
        out_specs=[pl.BlockSpec((bt, HEADS, DV), lambda i: (i, 0, 0)),
                   pl.BlockSpec((bt, HEADS, DK, DV), lambda i: (i, 0, 0, 0))],
        out_shape=[jax.ShapeDtypeStruct((T_S, HEADS, DV), f32),
                   jax.ShapeDtypeStruct((T_S, HEADS, DK, DV), f32)],
        compiler_params=_cparams(("arbitrary",)),
        name="retention_sample",
    )(qt, kt, v3, g3, jnp.asarray(cos8), jnp.asarray(sin8), gn_w.reshape(HEADS, DV), state)


def _layer_norm(x, w, b):
    mu = jnp.mean(x, axis=-1, keepdims=True)
    var = jnp.mean(jnp.square(x - mu), axis=-1, keepdims=True)
    return (x - mu) * lax.rsqrt(var + EPS) * w + b


def _gmlp_chunk(u_ref, gv_ref, ws_ref, bst_ref, lnw_ref, lnb_ref, o_ref, slot):
    vn = _layer_norm(jax.nn.gelu(gv_ref[...]), lnw_ref[...], lnb_ref[...]).astype(bf16)
    r = lax.broadcasted_iota(i32, (CHUNK, CHUNK), 0)
    c = lax.broadcasted_iota(i32, (CHUNK, CHUNK), 1)
    causal = r >= c
    bst = bst_ref[...]
    for g in range(GROUPS):
        w = jnp.where(causal, ws_ref[g], 0.0).astype(bf16)
        mixed = jnp.dot(w, vn[:, g * GC:(g + 1) * GC], preferred_element_type=f32) + bst[:, g:g + 1]
        o_ref[slot, :, g * GC:(g + 1) * GC] = (
            jax.nn.gelu(u_ref[:, g * GC:(g + 1) * GC]) * mixed).astype(o_ref.dtype)


def _gmlp_sample_kernel(u_ref, gv_ref, w0_ref, b0_ref, lnw_ref, lnb_ref, vn_ref, o_ref):
    vn = _layer_norm(jax.nn.gelu(gv_ref[...]), lnw_ref[...], lnb_ref[...])
    vn_ref[...] = vn
    o_ref[...] = (jax.nn.gelu(u_ref[...]) * (vn * w0_ref[...] + b0_ref[...])).astype(o_ref.dtype)


def _gmlp_sample(z_all, gm_ws, gm_bs, ln_w, ln_b):
    w0 = jnp.repeat(gm_ws[:, 0, 0], GC).reshape(1, GW)
    b0 = jnp.repeat(gm_bs[:, 0], GC).reshape(1, GW)
    blk = T_P // T_S
    return pl.pallas_call(
        _gmlp_sample_kernel,
        grid=(1,),
        in_specs=[pl.BlockSpec((T_S, GW), lambda i: (blk, COL_U)),
                  pl.BlockSpec((T_S, GW), lambda i: (blk, COL_GV)),
                  pl.BlockSpec((1, GW), lambda i: (0, 0)),
                  pl.BlockSpec((1, GW), lambda i: (0, 0)),
                  pl.BlockSpec((1, GW), lambda i: (0, 0)),
                  pl.BlockSpec((1, GW), lambda i: (0, 0))],
        out_specs=[pl.BlockSpec((T_S, GW), lambda i: (0, 0)),
                   pl.BlockSpec((T_S, GW), lambda i: (0, 0))],
        out_shape=[jax.ShapeDtypeStruct((T_S, GW), f32),
                   jax.ShapeDtypeStruct((T_S, GW), bf16)],
        compiler_params=_cparams(("arbitrary",)),
        name="gmlp_sample",
    )(z_all, z_all, w0, b0, ln_w.reshape(1, GW), ln_b.reshape(1, GW))


def _merge_kernel(a_ref, b_ref, ga_ref, gb_ref, woa_ref, wob_ref, o_ref):
    ya = jnp.dot(a_ref[...].astype(bf16), woa_ref[...], preferred_element_type=f32)
    yb = jnp.dot(b_ref[...].astype(bf16), wob_ref[...], preferred_element_type=f32)
    o_ref[...] = (jax.nn.sigmoid(ga_ref[...]) * ya + jax.nn.sigmoid(gb_ref[...]) * yb).astype(o_ref.dtype)


def _merge(a, bm, z_all, w_oa, w_ob, tm, z_blk0):
    m = a.shape[0]
    resident = lambda: pl.BlockSpec((D, D), lambda i: (0, 0), pipeline_mode=pl.Buffered(1))
    return pl.pallas_call(
        _merge_kernel,
        grid=(m // tm,),
        in_specs=[pl.BlockSpec((tm, D), lambda i: (i, 0)),
                  pl.BlockSpec((tm, D), lambda i: (i, 0)),
                  pl.BlockSpec((tm, D), lambda i: (z_blk0 + i, COL_GA)),
                  pl.BlockSpec((tm, D), lambda i: (z_blk0 + i, COL_GB)),
                  resident(), resident()],
        out_specs=pl.BlockSpec((tm, D), lambda i: (i, 0)),
        out_shape=jax.ShapeDtypeStruct((m, D), bf16),
        compiler_params=_cparams(("arbitrary",)),
        name="merge",
    )(a, bm, z_all, z_all, w_oa, w_ob)


MIX_NCH = SEQ // CHUNK
MIX_NSTEP = BATCH * MIX_NCH


def _mixer_prompt_kernel(s_dec, q_ref, k_ref, v_ref, g_ref, u_ref, gv_ref, ga_ref, gb_ref,
                         cos_ref, sin_ref, intra_ref, qd_ref, kd_ref, gnw_ref,
                         ws_ref, bst_ref, lnw_ref, lnb_ref, woa_ref, wob_ref,
                         m_ref, s_ref, a_scr, b_scr):
    s = pl.program_id(0)
    live = s < MIX_NSTEP
    c = jnp.minimum(s, MIX_NSTEP - 1)
    cur = s & 1

    @pl.when(s == 0)
    def _():
        a_scr[...] = jnp.zeros_like(a_scr)
        b_scr[...] = jnp.zeros_like(b_scr)

    @pl.when((c & (MIX_NCH - 1)) == 0)
    def _():
        s_ref[...] = jnp.zeros_like(s_ref)

    ya = jnp.dot(a_scr[1 - cur], woa_ref[...], preferred_element_type=f32)
    yb = jnp.dot(b_scr[1 - cur], wob_ref[...], preferred_element_type=f32)
    m_ref[...] = (jax.nn.sigmoid(ga_ref[...]) * ya + jax.nn.sigmoid(gb_ref[...]) * yb).astype(m_ref.dtype)

    _retention_chunk(s_dec, live, q_ref, k_ref, v_ref, g_ref, cos_ref, sin_ref, intra_ref, qd_ref, kd_ref,
                     gnw_ref, s_ref, a_scr, cur)
    _gmlp_chunk(u_ref, gv_ref, ws_ref, bst_ref, lnw_ref, lnb_ref, b_scr, cur)


def _mixer_prompt(z_all, gn_w, gm_ws, gm_bs, ln_w, ln_b, w_oa, w_ob):
    intra, q_dec, k_dec, s_dec, _ = _retention_consts()
    cos_t, sin_t = _rope_tables(np.arange(SEQ))
    cur = lambda s: jnp.minimum(s, MIX_NSTEP - 1)
    lag = lambda s: jnp.maximum(s - 1, 0)
    zspec = lambda width, row, col: pl.BlockSpec((CHUNK, width), lambda s: (row(s), col))
    const = lambda shape: pl.BlockSpec(shape, lambda s: (0,) * len(shape))
    resident = lambda: pl.BlockSpec((D, D), lambda s: (0, 0), pipeline_mode=pl.Buffered(1))
    return pl.pallas_call(
        functools.partial(_mixer_prompt_kernel, s_dec),
        grid=(MIX_NSTEP + 1,),
        in_specs=[zspec(HEADS * DK, cur, COL_Q), zspec(HEADS * DK, cur, COL_K),
                  zspec(HEADS * DV, cur, COL_V), zspec(HEADS * DV, cur, COL_G),
                  zspec(GW, cur, COL_U), zspec(GW, cur, COL_GV),
                  zspec(D, lag, COL_GA), zspec(D, lag, COL_GB),
                  pl.BlockSpec((CHUNK, DK), lambda s: (cur(s) & (MIX_NCH - 1), 0)),
                  pl.BlockSpec((CHUNK, DK), lambda s: (cur(s) & (MIX_NCH - 1), 0)),
                  const((HEADS, CHUNK, CHUNK)), const((CHUNK, HEADS)), const((CHUNK, HEADS)),
                  const((1, HEADS * DV)),
                  const((GROUPS, CHUNK, CHUNK)), const((CHUNK, GROUPS)), const((1, GW)), const((1, GW)),
                  resident(), resident()],
        out_specs=[pl.BlockSpec((CHUNK, D), lambda s: (lag(s), 0)),
                   pl.BlockSpec((None, HEADS, DK, DV), lambda s: (cur(s) // MIX_NCH, 0, 0, 0))],
        out_shape=[jax.ShapeDtypeStruct((T_P, D), bf16),
                   jax.ShapeDtypeStruct((BATCH, HEADS, DK, DV), f32)],
        scratch_shapes=[pltpu.VMEM((2, CHUNK, HEADS * DV), bf16), pltpu.VMEM((2, CHUNK, GW), bf16)],
        compiler_params=_cparams(("arbitrary",)),
        name="mixer_prompt",
    )(z_all, z_all, z_all, z_all, z_all, z_all, z_all, z_all,
      jnp.asarray(cos_t), jnp.asarray(sin_t), jnp.asarray(intra), jnp.asarray(q_dec), jnp.asarray(k_dec),
      gn_w.reshape(1, HEADS * DV), gm_ws, gm_bs.T, ln_w.reshape(1, GW), ln_b.reshape(1, GW), w_oa, w_ob)


def _post_kernel(m_ref, x_ref, g1_ref, sc_ref, sh_ref, nw_ref, wout_ref, wr_ref, br_ref,
                 x1_ref, h2_ref, idx_ref, gate_ref, rank_ref, cnt_ref):
    i = pl.program_id(0)

    @pl.when(i == 0)
    def _():
        cnt_ref[...] = jnp.zeros_like(cnt_ref)

    tm = m_ref.shape[0]
    y = jnp.dot(m_ref[...], wout_ref[...], preferred_element_type=f32)
    x1 = x_ref[...] + g1_ref[...] * y
    x1_ref[...] = x1
    xn = x1 * lax.rsqrt(jnp.mean(x1 * x1, axis=-1, keepdims=True) + EPS) * nw_ref[...]
    h2 = xn * (1.0 + sc_ref[...]) + sh_ref[...]
    h2_ref[...] = h2

    h_hi = h2.astype(bf16)
    h_lo = (h2 - h_hi.astype(f32)).astype(bf16)
    p_hi = jnp.dot(h_hi, wr_ref[...], preferred_element_type=f32)
    p_lo = jnp.dot(h_lo, wr_ref[...], preferred_element_type=f32)
    logits = p_hi[:, :LANES] + p_hi[:, LANES:] + p_lo[:, :LANES] + br_ref[...]

    lane = lax.broadcasted_iota(i32, (tm, LANES), 1).astype(f32)
    col = lax.broadcasted_iota(i32, (tm, TOP_K), 1)
    work = jnp.where(lane < N_EXPERTS, logits, -jnp.inf)
    member = jnp.zeros((tm, LANES), f32)
    vals, sels = [], []
    idx_out = jnp.zeros((tm, TOP_K), i32)
    for k in range(TOP_K):
        mx = jnp.max(work, axis=1, keepdims=True)
        ix = jnp.min(jnp.where(work == mx, lane, float(LANES)), axis=1, keepdims=True)
        sel = lane == ix
        vals.append(mx)
        sels.append(sel)
        idx_out = jnp.where(col == k, ix.astype(i32), idx_out)
        member = jnp.where(sel, 1.0, member)
        work = jnp.where(sel, -jnp.inf, work)
    idx_ref[...] = idx_out

    exps = [jnp.exp(v - vals[0]) for v in vals]
    den = exps[0] + exps[1] + exps[2] + exps[3]
    gate_out = jnp.zeros((tm, TOP_K), f32)
    for k in range(TOP_K):
        gate_out = jnp.where(col == k, exps[k] / den, gate_out)
    gate_ref[...] = gate_out

    r = lax.broadcasted_iota(i32, (tm, tm), 0)
    c = lax.broadcasted_iota(i32, (tm, tm), 1)
    lower = jnp.where(c < r, 1.0, 0.0).astype(bf16)
    before = jnp.dot(lower, member.astype(bf16), preferred_element_type=f32) + cnt_ref[...]
    rank_out = jnp.zeros((tm, TOP_K), i32)
    for k in range(TOP_K):
        rk = jnp.sum(jnp.where(sels[k], before, 0.0), axis=1, keepdims=True)
        rank_out = jnp.where(col == k, rk.astype(i32), rank_out)
    rank_ref[...] = rank_out
    cnt_ref[...] += jnp.sum(member, axis=0, keepdims=True)


def _post(m, x, mod, mod_is_rows, nw, w_out, w_router, b_router, tm, per_b):
    mm = m.shape[0]
    if mod_is_rows:
        mspec = lambda chunk: pl.BlockSpec((tm, D), lambda i: (i, chunk))
    else:
        mspec = lambda chunk: pl.BlockSpec((None, 1, D), lambda i: (i // per_b, 0, chunk))
    in_specs = [pl.BlockSpec((tm, D), lambda i: (i, 0)),
                pl.BlockSpec((tm, D), lambda i: (i, 0)),
                mspec(MOD_G1), mspec(MOD_SC2), mspec(MOD_SH2),
                pl.BlockSpec((1, D), lambda i: (0, 0)),
                pl.BlockSpec((D, D), lambda i: (0, 0), pipeline_mode=pl.Buffered(1)),
                pl.BlockSpec((D, 2 * LANES), lambda i: (0, 0)),
                pl.BlockSpec((1, LANES), lambda i: (0, 0))]
    args = [m, x, mod, mod, mod, nw, w_out, w_router, b_router]
    return pl.pallas_call(
        _post_kernel,
        grid=(mm // tm,),
        in_specs=in_specs,
        out_specs=[pl.BlockSpec((tm, D), lambda i: (i, 0)),
                   pl.BlockSpec((tm, D), lambda i: (i, 0)),
                   pl.BlockSpec((tm, TOP_K), lambda i: (i, 0)),
                   pl.BlockSpec((tm, TOP_K), lambda i: (i, 0)),
                   pl.BlockSpec((tm, TOP_K), lambda i: (i, 0)),
                   pl.BlockSpec((1, LANES), lambda i: (0, 0))],
        out_shape=[jax.ShapeDtypeStruct((mm, D), f32),
                   jax.ShapeDtypeStruct((mm, D), f32),
                   jax.ShapeDtypeStruct((mm, TOP_K), i32),
                   jax.ShapeDtypeStruct((mm, TOP_K), f32),
                   jax.ShapeDtypeStruct((mm, TOP_K), i32),
                   jax.ShapeDtypeStruct((1, LANES), f32)],
        compiler_params=_cparams(("arbitrary",)),
        name="post_mixer",
    )(*args)


DISP_TB = 128


DISP_PT = T_P // DISP_TB


def _dispatch_kernel(zblk_ref, dest_ref, h2p_ref, h2s_ref, xbuf_ref, zero_ref, sem):
    i = pl.program_id(0)

    def row_copy(src_ref, t, d):
        return pltpu.make_async_copy(src_ref.at[pl.ds(t, 1)], xbuf_ref.at[pl.ds(d, 1)], sem)

    def zero_copy(blk):
        return pltpu.make_async_copy(zero_ref, xbuf_ref.at[pl.ds(blk * MOE_BM, MOE_BM)], sem)

    @pl.when(i == 0)
    def _():
        zero_ref[...] = jnp.zeros_like(zero_ref)
        used = zblk_ref[N_EXPERTS]

        def zstart(e, c):
            zero_copy(zblk_ref[e]).start()
            return c

        def tstart(b, c):
            zero_copy(b).start()
            return c

        def zwait(e, c):
            zero_copy(0).wait()
            return c

        lax.fori_loop(0, N_EXPERTS, zstart, 0)
        lax.fori_loop(used, MOE_NBLK, tstart, 0)
        lax.fori_loop(0, N_EXPERTS, zwait, 0)
        lax.fori_loop(used, MOE_NBLK, zwait, 0)

    def issue(src_ref):
        for t in range(DISP_TB):
            for k in range(TOP_K):
                row_copy(src_ref, t, dest_ref[0, 0, t * TOP_K + k]).start(priority=k % 2)

    @pl.when(i < DISP_PT)
    def _():
        issue(h2p_ref)

    @pl.when(i == DISP_PT)
    def _():
        issue(h2s_ref)

    def wait_group(gi, c):
        for _ in range(SUBLANES * TOP_K):
            row_copy(h2p_ref, 0, 0).wait()
        return c

    lax.fori_loop(0, DISP_TB // SUBLANES, wait_group, 0)


def _dispatch(zero_blk, dest_all, h2_p, h2_s):
    nt = T_ALL // DISP_TB
    return pl.pallas_call(
        _dispatch_kernel,
        grid_spec=pltpu.PrefetchScalarGridSpec(
            num_scalar_prefetch=1,
            grid=(nt,),
            in_specs=[pl.BlockSpec((1, 1, DISP_TB * TOP_K), lambda i, zb: (i, 0, 0), memory_space=pltpu.SMEM),
                      pl.BlockSpec((DISP_TB, D), lambda i, zb: (jnp.minimum(i, DISP_PT - 1), 0)),
                      pl.BlockSpec((DISP_TB, D), lambda i, zb: (0, 0))],
            out_specs=pl.BlockSpec(memory_space=pl.ANY),
            scratch_shapes=[pltpu.VMEM((MOE_BM, D), f32), pltpu.SemaphoreType.DMA(())]),
        out_shape=jax.ShapeDtypeStruct((MOE_NROWS, D), f32),
        compiler_params=_cparams(("arbitrary",)),
        name="moe_dispatch",
    )(zero_blk, dest_all.reshape(nt, 1, DISP_TB * TOP_K), h2_p, h2_s)


def _moe_ffn_kernel(e_ref, sb_ref, nb_ref, xbuf_ref, wg_ref, wu_ref, bg_ref, bu_ref, wd_ref, bd_ref,
                    ybuf_ref, x_ref, acc_ref, wgb_ref, wub_ref, wdb_ref, sem_in, sem_out):
    p = pl.program_id(0)
    j = pl.program_id(1)
    nb = nb_ref[p]
    sb = sb_ref[p]

    def in_copy(r):
        return pltpu.make_async_copy(xbuf_ref.at[pl.ds((sb + r) * MOE_BM, MOE_BM)],
                                     x_ref.at[pl.ds(r * MOE_BM, MOE_BM)], sem_in.at[r])

    def out_copy(r):
        return pltpu.make_async_copy(acc_ref.at[pl.ds(r * MOE_BM, MOE_BM)],
                                     ybuf_ref.at[pl.ds((sb + r) * MOE_BM, MOE_BM)], sem_out)

    def for_blocks(fn):
        def body(r, c):
            fn(r)
            return c
        lax.fori_loop(0, nb, body, 0)

    @pl.when((p == 0) & (j == 0))
    def _():
        used = nb_ref[MOE_PMAX]
        acc_ref[0:MOE_BM, :] = jnp.zeros((MOE_BM, D), f32)

        def tail_copy(b):
            return pltpu.make_async_copy(acc_ref.at[pl.ds(0, MOE_BM)],
                                         ybuf_ref.at[pl.ds(b * MOE_BM, MOE_BM)], sem_out)

        def tstart(b, c):
            tail_copy(b).start()
            return c

        def twait(b, c):
            tail_copy(0).wait()
            return c

        lax.fori_loop(used, MOE_NBLK, tstart, 0)
        lax.fori_loop(used, MOE_NBLK, twait, 0)

    first = j == 0
    last = j == MOE_J - 1

    @pl.when(nb > 0)
    def _():
        @pl.when(first)
        def _():
            for_blocks(lambda r: in_copy(r).start())
            bd = jnp.broadcast_to(bd_ref[...], (MOE_BM, D))

            def init(r):
                acc_ref[pl.ds(pl.multiple_of(r * MOE_BM, MOE_BM), MOE_BM), :] = bd
            for_blocks(init)

        bg = bg_ref[...]
        bu = bu_ref[...]

        def cast_weights():
            wgb_ref[...] = wg_ref[...].astype(bf16)
            wub_ref[...] = wu_ref[...].astype(bf16)
            wdb_ref[...] = wd_ref[...].astype(bf16)

        def ffn_rows(b0, n_blk):
            rows = pl.ds(pl.multiple_of(b0 * MOE_BM, MOE_BM), n_blk * MOE_BM)
            x = x_ref[rows, :].astype(bf16)
            gate = jnp.dot(x, wgb_ref[...], preferred_element_type=f32) + bg
            up = jnp.dot(x, wub_ref[...], preferred_element_type=f32) + bu
            gate = jnp.minimum(gate, SWIGLU_LIMIT)
            up = jnp.clip(up, -SWIGLU_LIMIT, SWIGLU_LIMIT)
            act = (up + 1.0) * (gate * jax.nn.sigmoid(gate * SWIGLU_ALPHA))
            acc_ref[rows, :] += jnp.dot(act.astype(bf16), wdb_ref[...], preferred_element_type=f32)

        def trip(b0, groups, with_cast=False):
            n_blk = sum(groups)

            @pl.when(first)
            def _():
                for b in range(n_blk):
                    in_copy(b0 + b).wait()

            if with_cast:
                cast_weights()
            off = 0
            for g in groups:
                ffn_rows(b0 + off, g)
                off += g

            @pl.when(last)
            def _():
                for b in range(n_blk):
                    out_copy(b0 + b).start()

        full = (2, 2, 2)
        per_trip = sum(full)
        ntrip = sum((nb >= per_trip * m).astype(i32) for m in range(1, MOE_RB // per_trip + 1))
        rem = nb - ntrip * per_trip
        head = jnp.where(rem == 0, per_trip, rem)
        for n_head, groups in ((1, (1,)), (2, (2,)), (3, (2, 1)), (4, (2, 2)), (5, (2, 2, 1)), (per_trip, full)):
            @pl.when(head == n_head)
            def _(groups=groups):
                trip(0, groups, with_cast=True)

        def full_trip(q, c):
            trip(head + q * per_trip, full)
            return c
        lax.fori_loop(0, ntrip - (rem == 0).astype(i32), full_trip, 0)

        @pl.when(last)
        def _():
            for_blocks(lambda r: out_copy(r).wait())


def _moe_ffn(e_arr, sb_arr, nb_arr, xbuf, w_gate_up, b_gate_up, w_down, b_down):
    def jj(p, j, nb):
        return jnp.where(nb[p] > 0, j, MOE_J - 1)

    in_specs = [
        pl.BlockSpec(memory_space=pl.ANY),
        pl.BlockSpec((None, D, MOE_TF), lambda p, j, e, sb, nb: (e[p], 0, jj(p, j, nb))),
        pl.BlockSpec((None, D, MOE_TF), lambda p, j, e, sb, nb: (e[p], 0, MOE_J + jj(p, j, nb))),
        pl.BlockSpec((None, 1, MOE_TF), lambda p, j, e, sb, nb: (e[p], 0, jj(p, j, nb))),
        pl.BlockSpec((None, 1, MOE_TF), lambda p, j, e, sb, nb: (e[p], 0, MOE_J + jj(p, j, nb))),
        pl.BlockSpec((None, MOE_TF, D), lambda p, j, e, sb, nb: (e[p], jj(p, j, nb), 0)),
        pl.BlockSpec((None, 1, D), lambda p, j, e, sb, nb: (e[p], 0, 0)),
    ]
    return pl.pallas_call(
        _moe_ffn_kernel,
        grid_spec=pltpu.PrefetchScalarGridSpec(
            num_scalar_prefetch=3,
            grid=(MOE_PMAX, MOE_J),
            in_specs=in_specs,
            out_specs=pl.BlockSpec(memory_space=pl.ANY),
            scratch_shapes=[pltpu.VMEM((MOE_RB * MOE_BM, D), f32),
                            pltpu.VMEM((MOE_RB * MOE_BM, D), f32),
                            pltpu.VMEM((D, MOE_TF), bf16),
                            pltpu.VMEM((D, MOE_TF), bf16),
                            pltpu.VMEM((MOE_TF, D), bf16),
                            pltpu.SemaphoreType.DMA((MOE_RB,)),
                            pltpu.SemaphoreType.DMA(())]),
        out_shape=jax.ShapeDtypeStruct((MOE_NROWS, D), f32),
        compiler_params=_cparams(("arbitrary", "arbitrary"), vmem=56 * 1024 * 1024),
        name="moe_ffn",
    )(e_arr, sb_arr, nb_arr, xbuf, w_gate_up, w_gate_up,
      b_gate_up.reshape(N_EXPERTS, 1, 2 * D_EXPERT), b_gate_up.reshape(N_EXPERTS, 1, 2 * D_EXPERT),
      w_down, b_down.reshape(N_EXPERTS, 1, D))


COMB_TB = 128


def _combine_kernel(dcur_ref, dnext_ref, ybuf_ref, gate_ref, x1_ref, g2_ref, fw_ref, o_ref, rows_ref, sem):
    i = pl.program_id(0)
    slot = i & 1

    def row_copy(buf, n, d):
        return pltpu.make_async_copy(ybuf_ref.at[pl.ds(d, 1)], rows_ref.at[buf, pl.ds(n, 1)], sem.at[buf])

    def wait_tile(buf):
        def wait_group(gi, c):
            for _ in range(SUBLANES * TOP_K):
                row_copy(buf, 0, 0).wait()
            return c
        lax.fori_loop(0, COMB_TB // SUBLANES, wait_group, 0)

    @pl.when(i == 0)
    def _():
        def start_group(gi, c):
            for tt in range(SUBLANES):
                t = pl.multiple_of(gi * SUBLANES, SUBLANES) + tt
                for k in range(TOP_K):
                    d = dcur_ref[0, 0, (gi * SUBLANES + tt) * TOP_K + k]
                    row_copy(0, k * COMB_TB + t, d).start(priority=k % 2)
            return c
        lax.fori_loop(0, COMB_TB // SUBLANES, start_group, 0)

    wait_tile(slot)

    for t in range(COMB_TB):
        for k in range(TOP_K):
            row_copy(1 - slot, k * COMB_TB + t, dnext_ref[0, 0, t * TOP_K + k]).start(priority=k % 2)

    gates = gate_ref[...]
    f = gates[:, 0:1] * rows_ref[slot, 0:COMB_TB, :]
    for k in range(1, TOP_K):
        f = f + gates[:, k:k + 1] * rows_ref[slot, k * COMB_TB:(k + 1) * COMB_TB, :]
    x2 = x1_ref[...] + g2_ref[...] * f
    o_ref[...] = x2 * lax.rsqrt(jnp.mean(x2 * x2, axis=-1, keepdims=True) + EPS) * fw_ref[...]

    @pl.when(i == pl.num_programs(0) - 1)
    def _():
        wait_tile(1 - slot)


def _combine(dest, ybuf, gates, x1, mod, mod_is_rows, per_b, final_w):
    mm = x1.shape[0]
    nt = mm // COMB_TB
    dest3 = dest.reshape(nt, 1, COMB_TB * TOP_K)
    if mod_is_rows:
        g2spec = pl.BlockSpec((COMB_TB, D), lambda i: (i, MOD_G2))
    else:
        g2spec = pl.BlockSpec((None, 1, D), lambda i: (i // per_b, 0, MOD_G2))
    return pl.pallas_call(
        _combine_kernel,
        grid=(nt,),
        in_specs=[pl.BlockSpec((1, 1, COMB_TB * TOP_K), lambda i: (i, 0, 0), memory_space=pltpu.SMEM),
                  pl.BlockSpec((1, 1, COMB_TB * TOP_K), lambda i: (jnp.minimum(i + 1, nt - 1), 0, 0),
                               memory_space=pltpu.SMEM),
                  pl.BlockSpec(memory_space=pl.ANY),
                  pl.BlockSpec((COMB_TB, TOP_K), lambda i: (i, 0)),
                  pl.BlockSpec((COMB_TB, D), lambda i: (i, 0)),
                  g2spec,
                  pl.BlockSpec((1, D), lambda i: (0, 0))],
        out_specs=pl.BlockSpec((COMB_TB, D), lambda i: (i, 0)),
        out_shape=jax.ShapeDtypeStruct((mm, D), f32),
        scratch_shapes=[pltpu.VMEM((2, TOP_K * COMB_TB, D), f32), pltpu.SemaphoreType.DMA((2,))],
        compiler_params=_cparams(("arbitrary",)),
        name="moe_combine",
    )(dest3, dest3, ybuf, gates, x1, mod, final_w.reshape(1, D))


def _routing_tables(cnt_p, cnt_s, idx_p, rank_p, idx_s, rank_s):
    cp = cnt_p[0, :N_EXPERTS].astype(i32)
    cs = cnt_s[0, :N_EXPERTS].astype(i32)
    nblk = (cp + cs + MOE_BM - 1) // MOE_BM
    blk_end = jnp.cumsum(nblk)
    blk_start = blk_end - nblk
    row_start = blk_start * MOE_BM
    dest_p = row_start[idx_p.reshape(-1)] + rank_p.reshape(-1)
    dest_s = (row_start + cp)[idx_s.reshape(-1)] + rank_s.reshape(-1)
    used = blk_end[-1:]
    zero_blk = jnp.concatenate([jnp.maximum(blk_end - 1, 0), used]).astype(i32)
    npass = (nblk + MOE_RB - 1) // MOE_RB
    pass_end = jnp.cumsum(npass)
    total = pass_end[-1]
    pid = jnp.arange(MOE_PMAX, dtype=i32)
    pid_c = jnp.minimum(pid, total - 1)
    pe = jnp.minimum(jnp.sum((pass_end[None, :] <= pid_c[:, None]).astype(i32), axis=1), N_EXPERTS - 1)
    local = pid_c - (pass_end - npass)[pe]
    active = pid < total
    sb = jnp.where(active, blk_start[pe] + local * MOE_RB, 0).astype(i32)
    nb = jnp.where(active, jnp.clip(nblk[pe] - local * MOE_RB, 0, MOE_RB), 0)
    nb = jnp.concatenate([nb, used]).astype(i32)
    return dest_p.astype(i32), dest_s.astype(i32), zero_blk, pe, sb, nb


def kernel(x_prompt, x_sample, c_prompt, c_sample, state_ret, norm1_w, norm2_w, w_ada, b_ada, w_in,
           ret_gn_w, gm_ln_w, gm_ln_b, gm_ws, gm_bs, w_oa, w_ob, w_out, w_router, b_router,
           w_gate_up, b_gate_up, w_down, b_down, final_norm_w):
    x_p = x_prompt.reshape(T_P, D)
    x_s = x_sample.reshape(T_S, D)
    per_b = lambda tm: SEQ // tm

    mod = _ada(jnp.concatenate([c_prompt, c_sample], axis=0), w_ada[0], b_ada[0])
    mod_p = mod[:BATCH].reshape(BATCH, 1, 6 * D)
    mod_s = mod[BATCH:]

    n1 = norm1_w[0].reshape(1, D)
    h_all = _norm1(x_p, x_s, n1, mod_p, mod_s)
    z_all = _proj_in(h_all, w_in[0])

    woa, wob, wout = w_oa[0].astype(bf16), w_ob[0].astype(bf16), w_out[0].astype(bf16)
    m_p, s_p = _mixer_prompt(z_all, ret_gn_w[0], gm_ws[0], gm_bs[0], gm_ln_w[0], gm_ln_b[0], woa, wob)
    z_s = z_all[T_P:]
    qt = z_s[:, :HEADS * DK].reshape(T_S, HEADS, DK).transpose(0, 2, 1)
    kt = z_s[:, HEADS * DK:2 * HEADS * DK].reshape(T_S, HEADS, DK).transpose(0, 2, 1)
    v3 = z_s[:, 2 * HEADS * DK:2 * HEADS * DK + HEADS * DV].reshape(T_S, HEADS, DV)
    g3 = z_s[:, 2 * HEADS * DK + HEADS * DV:2 * HEADS * DK + 2 * HEADS * DV].reshape(T_S, HEADS, DV)
    a_s, s_s = _ret_sample(qt, kt, v3, g3, ret_gn_w[0], state_ret[0])
    vn_s, bm_s = _gmlp_sample(z_all, gm_ws[0], gm_bs[0], gm_ln_w[0], gm_ln_b[0])
    m_s =_merge(a_s.reshape(T_S, HEADS * DV), bm_s, z_all, woa, wob, T_S, T_P // T_S)

    n2 = norm2_w[0].reshape(1, D)
    wr_hi = w_router[0].astype(bf16)
    wr_lo = (w_router[0] - wr_hi.astype(f32)).astype(bf16)
    lane_pad = ((0, 0), (0, LANES - N_EXPERTS))
    wr_cat = jnp.concatenate([jnp.pad(wr_hi, lane_pad), jnp.pad(wr_lo, lane_pad)], axis=1)
    br = jnp.pad(b_router[0].reshape(1, N_EXPERTS), lane_pad)
    x1_p, h2_p, idx_p, gate_p, rank_p, cnt_p = _post(
        m_p, x_p, mod_p, False, n2, wout, wr_cat, br, 512, per_b(512))
    x1_s, h2_s, idx_s, gate_s, rank_s, cnt_s = _post(
        m_s, x_s, mod_s, True, n2, wout, wr_cat, br, T_S, 1)

    dest_p, dest_s, zero_blk, pe, sb, nb = _routing_tables(cnt_p, cnt_s, idx_p, rank_p, idx_s, rank_s)
    xbuf = _dispatch(zero_blk, jnp.concatenate([dest_p, dest_s], axis=0), h2_p, h2_s)
    ybuf = _moe_ffn(pe, sb, nb, xbuf, w_gate_up[0], b_gate_up[0], w_down[0], b_down[0])
    y_p = _combine(dest_p, ybuf, gate_p, x1_p, mod_p, False, per_b(COMB_TB), final_norm_w)
    y_s = _combine(dest_s, ybuf, gate_s, x1_s, mod_s, True, 1, final_norm_w)

    return (y_p.reshape(BATCH, SEQ, D),
            y_s.reshape(DEC_BATCH, 1, D),
            s_p.reshape(1, BATCH, HEADS, DK, DV),
            s_s.reshape(1, DEC_BATCH, HEADS, DK, DV),
            vn_s.reshape(1, DEC_BATCH, 1, GW))
```

```python
import functools

import numpy as np
import jax
import jax.numpy as jnp
from jax import lax
from jax.experimental import pallas as pl
from jax.experimental.pallas import tpu as pltpu

f32 = jnp.float32
bf16 = jnp.bfloat16
i32 = jnp.int32

D = 2048
BATCH = 4
SEQ = 2048
DEC_BATCH = 128
PAST_LEN = 16384
HEADS = 8
DK = 128
DV = 256
CHUNK = 128
ROPE_THETA = 10000.0
GROUPS = 8
GW = 2048
GC = GW // GROUPS
N_EXPERTS = 32
TOP_K = 4
D_EXPERT = 2048
SWIGLU_LIMIT = 7.0
SWIGLU_ALPHA = 1.702
EPS = 1e-6

T_P = BATCH * SEQ
T_S = DEC_BATCH
T_ALL = T_P + T_S
D_IN = 14336
COL_Q, COL_K = 0, 1
COL_V, COL_G, COL_U, COL_GV, COL_GA, COL_GB = 1, 2, 3, 4, 5, 6
MOD_SH1, MOD_SC1, MOD_G1, MOD_SH2, MOD_SC2, MOD_G2 = range(6)

LANES = 128
SUBLANES = 8
VMEM_LIMIT = 48 * 1024 * 1024

MOE_BM = 128
MOE_RB = 12
MOE_TF = 256
MOE_J = D_EXPERT // MOE_TF
TK_ALL = T_ALL * TOP_K
MOE_NBLK = (TK_ALL + N_EXPERTS * (MOE_BM - 1)) // MOE_BM
MOE_NROWS = MOE_NBLK * MOE_BM
MOE_PMAX = N_EXPERTS + (MOE_NBLK - N_EXPERTS) // MOE_RB


def _cparams(sem, vmem=VMEM_LIMIT):
    return pltpu.CompilerParams(dimension_semantics=sem, vmem_limit_bytes=vmem)


def _ada_kernel(c_ref, w_ref, b_ref, o_ref):
    o_ref[...] = jnp.dot(c_ref[...].astype(bf16), w_ref[...].astype(bf16),
                         preferred_element_type=f32) + b_ref[...]


def _ada(c_all, w_ada, b_ada):
    m = c_all.shape[0]
    tn = 1024
    return pl.pallas_call(
        _ada_kernel,
        grid=(6 * D // tn,),
        in_specs=[pl.BlockSpec((m, D), lambda j: (0, 0)),
                  pl.BlockSpec((D, tn), lambda j: (0, j)),
                  pl.BlockSpec((1, tn), lambda j: (0, j))],
        out_specs=pl.BlockSpec((m, tn), lambda j: (0, j)),
        out_shape=jax.ShapeDtypeStruct((m, 6 * D), f32),
        compiler_params=_cparams(("arbitrary",)),
        name="ada_mod",
    )(c_all, w_ada, b_ada.reshape(1, 6 * D))


def _norm_mod_kernel(x_ref, nw_ref, sc_ref, sh_ref, o_ref):
    x = x_ref[...]
    y = x * lax.rsqrt(jnp.mean(x * x, axis=-1, keepdims=True) + EPS) * nw_ref[...]
    o_ref[...] = (y * (1.0 + sc_ref[...]) + sh_ref[...]).astype(o_ref.dtype)


NORM1_TM = T_S
NORM1_PT = T_P // NORM1_TM


def _norm1_kernel(xp_ref, xs_ref, nw_ref, scp_ref, shp_ref, scs_ref, shs_ref, o_ref):
    i = pl.program_id(0)

    @pl.when(i < NORM1_PT)
    def _():
        _norm_mod_kernel(xp_ref, nw_ref, scp_ref, shp_ref, o_ref)

    @pl.when(i == NORM1_PT)
    def _():
        _norm_mod_kernel(xs_ref, nw_ref, scs_ref, shs_ref, o_ref)


def _norm1(x_p, x_s, nw, mod_p, mod_s):
    tm = NORM1_TM
    per_b = SEQ // tm
    pt = lambda i: jnp.minimum(i, NORM1_PT - 1)
    return pl.pallas_call(
        _norm1_kernel,
        grid=(NORM1_PT + 1,),
        in_specs=[pl.BlockSpec((tm, D), lambda i: (pt(i), 0)),
                  pl.BlockSpec((tm, D), lambda i: (0, 0)),
                  pl.BlockSpec((1, D), lambda i: (0, 0)),
                  pl.BlockSpec((None, 1, D), lambda i: (pt(i) // per_b, 0, MOD_SC1)),
                  pl.BlockSpec((None, 1, D), lambda i: (pt(i) // per_b, 0, MOD_SH1)),
                  pl.BlockSpec((tm, D), lambda i: (0, MOD_SC1)),
                  pl.BlockSpec((tm, D), lambda i: (0, MOD_SH1))],
        out_specs=pl.BlockSpec((tm, D), lambda i: (i, 0)),
        out_shape=jax.ShapeDtypeStruct((T_ALL, D), bf16),
        compiler_params=_cparams(("arbitrary",)),
        name="norm1",
    )(x_p, x_s, nw, mod_p, mod_p, mod_s, mod_s)


def _proj_in_kernel(h_ref, w_ref, o_ref, wb_ref):
    @pl.when(pl.program_id(1) == 0)
    def _():
        wb_ref[...] = w_ref[...].astype(bf16)

    o_ref[...] = jnp.dot(h_ref[...], wb_ref[...], preferred_element_type=f32)


def _proj_in(h_all, w_in):
    tm, tn = 640, 1024
    return pl.pallas_call(
        _proj_in_kernel,
        grid=(D_IN // tn, T_ALL // tm),
        in_specs=[pl.BlockSpec((tm, D), lambda j, i: (i, 0)),
                  pl.BlockSpec((D, tn), lambda j, i: (0, j))],
        out_specs=pl.BlockSpec((tm, tn), lambda j, i: (i, j)),
        out_shape=jax.ShapeDtypeStruct((T_ALL, D_IN), f32),
        scratch_shapes=[pltpu.VMEM((D, tn), bf16)],
        compiler_params=_cparams(("arbitrary", "arbitrary")),
        name="proj_in",
    )(h_all, w_in)


def _retention_consts():
    h = np.arange(HEADS, dtype=np.float64)
    lg = np.log1p(-np.exp2(-5.0 - h))
    idx = np.arange(CHUNK, dtype=np.float64)
    diff = idx[:, None] - idx[None, :]
    intra = np.where(diff >= 0, np.exp(np.maximum(diff, 0.0)[None] * lg[:, None, None]), 0.0)
    q_dec = np.exp((idx + 1.0)[:, None] * lg[None, :])
    k_dec = np.exp((CHUNK - 1.0 - idx)[:, None] * lg[None, :])
    s_dec = np.exp(CHUNK * lg)
    gamma = np.exp(lg)
    return (intra.astype(np.float32), q_dec.astype(np.float32), k_dec.astype(np.float32),
            [float(v) for v in s_dec], [float(v) for v in gamma])


def _rope_tables(pos):
    half = DK // 2
    freq = ROPE_THETA ** (-np.arange(half, dtype=np.float64) / half)
    ang = np.asarray(pos, dtype=np.float64)[:, None] * freq[None, :]
    cos, sin = np.cos(ang), np.sin(ang)
    cos_t = np.concatenate([cos, cos], axis=-1).astype(np.float32)
    sin_t = np.concatenate([-sin, sin], axis=-1).astype(np.float32)
    return cos_t, sin_t


def _group_norm_gate(o, gnw, g):
    mu = jnp.mean(o, axis=-1, keepdims=True)
    var = jnp.mean(jnp.square(o - mu), axis=-1, keepdims=True)
    on = (o - mu) * lax.rsqrt(var + EPS) * gnw
    return jax.nn.silu(g) * on


def _retention_chunk(s_dec, live, q_ref, k_ref, v_ref, g_ref, cos_ref, sin_ref, intra_ref, qd_ref, kd_ref,
                     gnw_ref, s_ref, a_ref, slot):
    cos = cos_ref[...]
    sin = sin_ref[...]
    qd = qd_ref[...]
    kd = kd_ref[...]
    for h in range(HEADS):
        qh = q_ref[:, h * DK:(h + 1) * DK]
        kh = k_ref[:, h * DK:(h + 1) * DK]
        qh = (qh * cos + pltpu.roll(qh, DK // 2, 1) * sin) * (DK ** -0.5)
        kh = kh * cos + pltpu.roll(kh, DK // 2, 1) * sin
        vh = v_ref[:, h * DV:(h + 1) * DV].astype(bf16)
        s_old = s_ref[h]
        scores = lax.dot_general(qh.astype(bf16), kh.astype(bf16), (((1,), (1,)), ((), ())),
                                 preferred_element_type=f32) * intra_ref[h]
        o = jnp.dot(scores.astype(bf16), vh, preferred_element_type=f32)
        o = o + jnp.dot((qh * qd[:, h:h + 1]).astype(bf16), s_old.astype(bf16), preferred_element_type=f32)
        kw_t = (kh * kd[:, h:h + 1]).T.astype(bf16)
        s_new = s_old * s_dec[h] + jnp.dot(kw_t, vh, preferred_element_type=f32)
        s_ref[h] = jnp.where(live, s_new, s_old)
        a_ref[slot, :, h * DV:(h + 1) * DV] = _group_norm_gate(
            o, gnw_ref[:, h * DV:(h + 1) * DV], g_ref[:, h * DV:(h + 1) * DV]).astype(a_ref.dtype)


RET_S_BT = 4


def _ret_sample_kernel(gamma, qt_ref, kt_ref, v_ref, g_ref, cos_ref, sin_ref, spread_ref, gnw_ref, s_in_ref,
                       a_ref, s_out_ref):
    cos = cos_ref[...]
    sin = sin_ref[...]

    def rope_t(x):
        rolled = jnp.concatenate([x[DK // 2:], x[:DK // 2]], axis=0)
        return x * cos + rolled * sin

    def spread(x):
        hi = x.astype(bf16).astype(f32)
        r1 = x - hi
        mid = r1.astype(bf16).astype(f32)
        lo = r1 - mid
        pieces = hi + pltpu.roll(mid, HEADS, 1) + pltpu.roll(lo, 2 * HEADS, 1)
        return jnp.dot(pieces.astype(bf16), spread_ref[...], preferred_element_type=f32)

    def body(t, carry):
        qb = spread(rope_t(qt_ref[t]) * (DK ** -0.5))
        kb = spread(rope_t(kt_ref[t]))
        v8 = v_ref[t]
        g8 = g_ref[t]
        for h in range(HEADS):
            hs = slice(h * DV, (h + 1) * DV)
            s_new = s_in_ref[t, h] * gamma[h] + kb[:, hs] * v8[h:h + 1, :]
            s_out_ref[t, h] = s_new
            o = jnp.sum(qb[:, hs] * s_new, axis=0, keepdims=True)
            a_ref[t, h:h + 1, :] = _group_norm_gate(o, gnw_ref[h:h + 1, :], g8[h:h + 1, :])
        return carry

    lax.fori_loop(0, RET_S_BT, body, 0)


def _ret_sample(qt, kt, v3, g3, gn_w, state):
    _, _, _, _, gamma = _retention_consts()
    cos_t, sin_t = _rope_tables(np.array([PAST_LEN]))
    cos8 = np.ascontiguousarray(np.broadcast_to(cos_t[0][:, None], (DK, LANES)))
    sin8 = np.ascontiguousarray(np.broadcast_to(sin_t[0][:, None], (DK, LANES)))
    spread_m = np.zeros((LANES, HEADS * DV), np.float32)
    for piece in range(3):
        for h in range(HEADS):
            spread_m[piece * HEADS + h, h * DV:(h + 1) * DV] = 1.0
    bt = RET_S_BT
    return pl.pallas_call(
        functools.partial(_ret_sample_kernel, gamma),
        grid=(T_S // bt,),
        in_specs=[pl.BlockSpec((bt, DK, LANES), lambda i: (i, 0, 0)),
                  pl.BlockSpec((bt, DK, LANES), lambda i: (i, 0, 0)),
                  pl.BlockSpec((bt, HEADS, DV), lambda i: (i, 0, 0)),
                  pl.BlockSpec((bt, HEADS, DV), lambda i: (i, 0, 0)),
                  pl.BlockSpec((DK, LANES), lambda i: (0, 0)),
                  pl.BlockSpec((DK, LANES), lambda i: (0, 0)),
                  pl.BlockSpec((LANES, HEADS * DV), lambda i: (0, 0)),
                  pl.BlockSpec((HEADS, DV), lambda i: (0, 0)),
                  pl.BlockSpec((bt, HEADS, DK, DV), lambda i: (i, 0, 0, 0))],
        out_specs=[pl.BlockSpec((bt, HEADS, DV), lambda i: (i, 0, 0)),
                   pl.BlockSpec((bt, HEADS, DK, DV), lambda i: (i, 0, 0, 0))],
        out_shape=[jax.ShapeDtypeStruct((T_S, HEADS, DV), f32),
                   jax.ShapeDtypeStruct((T_S, HEADS, DK, DV), f32)],
        compiler_params=_cparams(("arbitrary",)),
        name="retention_sample",
    )(qt, kt, v3, g3, jnp.asarray(cos8), jnp.asarray(sin8), jnp.asarray(spread_m, dtype=bf16),
      gn_w.reshape(HEADS, DV), state)


def _layer_norm(x, w, b):
    mu = jnp.mean(x, axis=-1, keepdims=True)
    var = jnp.mean(jnp.square(x - mu), axis=-1, keepdims=True)
    return (x - mu) * lax.rsqrt(var + EPS) * w + b


def _gmlp_chunk(u_ref, gv_ref, ws_ref, bst_ref, lnw_ref, lnb_ref, o_ref, slot):
    vn = _layer_norm(jax.nn.gelu(gv_ref[...]), lnw_ref[...], lnb_ref[...]).astype(bf16)
    r = lax.broadcasted_iota(i32, (CHUNK, CHUNK), 0)
    c = lax.broadcasted_iota(i32, (CHUNK, CHUNK), 1)
    causal = r >= c
    bst = bst_ref[...]
    for g in range(GROUPS):
        w = jnp.where(causal, ws_ref[g], 0.0).astype(bf16)
        mixed = jnp.dot(w, vn[:, g * GC:(g + 1) * GC], preferred_element_type=f32) + bst[:, g:g + 1]
        o_ref[slot, :, g * GC:(g + 1) * GC] = (
            jax.nn.gelu(u_ref[:, g * GC:(g + 1) * GC]) * mixed).astype(o_ref.dtype)


def _gmlp_sample_kernel(u_ref, gv_ref, w0_ref, b0_ref, lnw_ref, lnb_ref, vn_ref, o_ref):
    vn = _layer_norm(jax.nn.gelu(gv_ref[...]), lnw_ref[...], lnb_ref[...])
    vn_ref[...] = vn
    o_ref[...] = (jax.nn.gelu(u_ref[...]) * (vn * w0_ref[...] + b0_ref[...])).astype(o_ref.dtype)


def _gmlp_sample(z_all, gm_ws, gm_bs, ln_w, ln_b):
    w0 = jnp.repeat(gm_ws[:, 0, 0], GC).reshape(1, GW)
    b0 = jnp.repeat(gm_bs[:, 0], GC).reshape(1, GW)
    blk = T_P // T_S
    return pl.pallas_call(
        _gmlp_sample_kernel,
        grid=(1,),
        in_specs=[pl.BlockSpec((T_S, GW), lambda i: (blk, COL_U)),
                  pl.BlockSpec((T_S, GW), lambda i: (blk, COL_GV)),
                  pl.BlockSpec((1, GW), lambda i: (0, 0)),
                  pl.BlockSpec((1, GW), lambda i: (0, 0)),
                  pl.BlockSpec((1, GW), lambda i: (0, 0)),
                  pl.BlockSpec((1, GW), lambda i: (0, 0))],
        out_specs=[pl.BlockSpec((T_S, GW), lambda i: (0, 0)),
                   pl.BlockSpec((T_S, GW), lambda i: (0, 0))],
        out_shape=[jax.ShapeDtypeStruct((T_S, GW), f32),
                   jax.ShapeDtypeStruct((T_S, GW), bf16)],
        compiler_params=_cparams(("arbitrary",)),
        name="gmlp_sample",
    )(z_all, z_all, w0, b0, ln_w.reshape(1, GW), ln_b.reshape(1, GW))


def _merge_kernel(a_ref, b_ref, ga_ref, gb_ref, woa_ref, wob_ref, o_ref):
    ya = jnp.dot(a_ref[...].astype(bf16), woa_ref[...], preferred_element_type=f32)
    yb = jnp.dot(b_ref[...].astype(bf16), wob_ref[...], preferred_element_type=f32)
    o_ref[...] = (jax.nn.sigmoid(ga_ref[...]) * ya + jax.nn.sigmoid(gb_ref[...]) * yb).astype(o_ref.dtype)


def _merge(a, bm, z_all, w_oa, w_ob, tm, z_blk0):
    m = a.shape[0]
    resident = lambda: pl.BlockSpec((D, D), lambda i: (0, 0), pipeline_mode=pl.Buffered(1))
    return pl.pallas_call(
        _merge_kernel,
        grid=(m // tm,),
        in_specs=[pl.BlockSpec((tm, D), lambda i: (i, 0)),
                  pl.BlockSpec((tm, D), lambda i: (i, 0)),
                  pl.BlockSpec((tm, D), lambda i: (z_blk0 + i, COL_GA)),
                  pl.BlockSpec((tm, D), lambda i: (z_blk0 + i, COL_GB)),
                  resident(), resident()],
        out_specs=pl.BlockSpec((tm, D), lambda i: (i, 0)),
        out_shape=jax.ShapeDtypeStruct((m, D), bf16),
        compiler_params=_cparams(("arbitrary",)),
        name="merge",
    )(a, bm, z_all, z_all, w_oa, w_ob)


MIX_NCH = SEQ // CHUNK
MIX_NSTEP = BATCH * MIX_NCH


def _mixer_prompt_kernel(s_dec, q_ref, k_ref, v_ref, g_ref, u_ref, gv_ref, ga_ref, gb_ref,
                         cos_ref, sin_ref, intra_ref, qd_ref, kd_ref, gnw_ref,
                         ws_ref, bst_ref, lnw_ref, lnb_ref, woa_ref, wob_ref,
                         m_ref, s_ref, a_scr, b_scr):
    s = pl.program_id(0)
    live = s < MIX_NSTEP
    c = jnp.minimum(s, MIX_NSTEP - 1)
    cur = s & 1

    @pl.when(s == 0)
    def _():
        a_scr[...] = jnp.zeros_like(a_scr)
        b_scr[...] = jnp.zeros_like(b_scr)

    @pl.when((c & (MIX_NCH - 1)) == 0)
    def _():
        s_ref[...] = jnp.zeros_like(s_ref)

    ya = jnp.dot(a_scr[1 - cur], woa_ref[...], preferred_element_type=f32)
    yb = jnp.dot(b_scr[1 - cur], wob_ref[...], preferred_element_type=f32)
    m_ref[...] = (jax.nn.sigmoid(ga_ref[...]) * ya + jax.nn.sigmoid(gb_ref[...]) * yb).astype(m_ref.dtype)

    _retention_chunk(s_dec, live, q_ref, k_ref, v_ref, g_ref, cos_ref, sin_ref, intra_ref, qd_ref, kd_ref,
                     gnw_ref, s_ref, a_scr, cur)
    _gmlp_chunk(u_ref, gv_ref, ws_ref, bst_ref, lnw_ref, lnb_ref, b_scr, cur)


def _mixer_prompt(z_all, gn_w, gm_ws, gm_bs, ln_w, ln_b, w_oa, w_ob):
    intra, q_dec, k_dec, s_dec, _ = _retention_consts()
    cos_t, sin_t = _rope_tables(np.arange(SEQ))
    cur = lambda s: jnp.minimum(s, MIX_NSTEP - 1)
    lag = lambda s: jnp.maximum(s - 1, 0)
    zspec = lambda width, row, col: pl.BlockSpec((CHUNK, width), lambda s: (row(s), col))
    const = lambda shape: pl.BlockSpec(shape, lambda s: (0,) * len(shape))
    resident = lambda: pl.BlockSpec((D, D), lambda s: (0, 0), pipeline_mode=pl.Buffered(1))
    return pl.pallas_call(
        functools.partial(_mixer_prompt_kernel, s_dec),
        grid=(MIX_NSTEP + 1,),
        in_specs=[zspec(HEADS * DK, cur, COL_Q), zspec(HEADS * DK, cur, COL_K),
                  zspec(HEADS * DV, cur, COL_V), zspec(HEADS * DV, cur, COL_G),
                  zspec(GW, cur, COL_U), zspec(GW, cur, COL_GV),
                  zspec(D, lag, COL_GA), zspec(D, lag, COL_GB),
                  pl.BlockSpec((CHUNK, DK), lambda s: (cur(s) & (MIX_NCH - 1), 0)),
                  pl.BlockSpec((CHUNK, DK), lambda s: (cur(s) & (MIX_NCH - 1), 0)),
                  const((HEADS, CHUNK, CHUNK)), const((CHUNK, HEADS)), const((CHUNK, HEADS)),
                  const((1, HEADS * DV)),
                  const((GROUPS, CHUNK, CHUNK)), const((CHUNK, GROUPS)), const((1, GW)), const((1, GW)),
                  resident(), resident()],
        out_specs=[pl.BlockSpec((CHUNK, D), lambda s: (lag(s), 0)),
                   pl.BlockSpec((None, HEADS, DK, DV), lambda s: (cur(s) // MIX_NCH, 0, 0, 0))],
        out_shape=[jax.ShapeDtypeStruct((T_P, D), bf16),
                   jax.ShapeDtypeStruct((BATCH, HEADS, DK, DV), f32)],
        scratch_shapes=[pltpu.VMEM((2, CHUNK, HEADS * DV), bf16), pltpu.VMEM((2, CHUNK, GW), bf16)],
        compiler_params=_cparams(("arbitrary",)),
        name="mixer_prompt",
    )(z_all, z_all, z_all, z_all, z_all, z_all, z_all, z_all,
      jnp.asarray(cos_t), jnp.asarray(sin_t), jnp.asarray(intra), jnp.asarray(q_dec), jnp.asarray(k_dec),
      gn_w.reshape(1, HEADS * DV), gm_ws, gm_bs.T, ln_w.reshape(1, GW), ln_b.reshape(1, GW), w_oa, w_ob)


def _post_kernel(m_ref, x_ref, g1_ref, sc_ref, sh_ref, nw_ref, wout_ref, wr_ref, br_ref,
                 x1_ref, h2_ref, idx_ref, gate_ref, rank_ref, cnt_ref):
    i = pl.program_id(0)

    @pl.when(i == 0)
    def _():
        cnt_ref[...] = jnp.zeros_like(cnt_ref)

    tm = m_ref.shape[0]
    y = jnp.dot(m_ref[...], wout_ref[...], preferred_element_type=f32)
    x1 = x_ref[...] + g1_ref[...] * y
    x1_ref[...] = x1
    xn = x1 * lax.rsqrt(jnp.mean(x1 * x1, axis=-1, keepdims=True) + EPS) * nw_ref[...]
    h2 = xn * (1.0 + sc_ref[...]) + sh_ref[...]
    h2_ref[...] = h2

    h_hi = h2.astype(bf16)
    h_lo = (h2 - h_hi.astype(f32)).astype(bf16)
    p_hi = jnp.dot(h_hi, wr_ref[...], preferred_element_type=f32)
    p_lo = jnp.dot(h_lo, wr_ref[...], preferred_element_type=f32)
    logits = p_hi[:, :LANES] + p_hi[:, LANES:] + p_lo[:, :LANES] + br_ref[...]

    lane = lax.broadcasted_iota(i32, (tm, LANES), 1).astype(f32)
    col = lax.broadcasted_iota(i32, (tm, TOP_K), 1)
    work = jnp.where(lane < N_EXPERTS, logits, -jnp.inf)
    member = jnp.zeros((tm, LANES), f32)
    vals, sels = [], []
    idx_out = jnp.zeros((tm, TOP_K), i32)
    for k in range(TOP_K):
        mx = jnp.max(work, axis=1, keepdims=True)
        ix = jnp.min(jnp.where(work == mx, lane, float(LANES)), axis=1, keepdims=True)
        sel = lane == ix
        vals.append(mx)
        sels.append(sel)
        idx_out = jnp.where(col == k, ix.astype(i32), idx_out)
        member = jnp.where(sel, 1.0, member)
        work = jnp.where(sel, -jnp.inf, work)
    idx_ref[...] = idx_out

    exps = [jnp.exp(v - vals[0]) for v in vals]
    den = exps[0] + exps[1] + exps[2] + exps[3]
    gate_out = jnp.zeros((tm, TOP_K), f32)
    for k in range(TOP_K):
        gate_out = jnp.where(col == k, exps[k] / den, gate_out)
    gate_ref[...] = gate_out

    r = lax.broadcasted_iota(i32, (tm, tm), 0)
    c = lax.broadcasted_iota(i32, (tm, tm), 1)
    lower = jnp.where(c < r, 1.0, 0.0).astype(bf16)
    before = jnp.dot(lower, member.astype(bf16), preferred_element_type=f32) + cnt_ref[...]
    rank_out = jnp.zeros((tm, TOP_K), i32)
    for k in range(TOP_K):
        rk = jnp.sum(jnp.where(sels[k], before, 0.0), axis=1, keepdims=True)
        rank_out = jnp.where(col == k, rk.astype(i32), rank_out)
    rank_ref[...] = rank_out
    cnt_ref[...] += jnp.sum(member, axis=0, keepdims=True)


def _post(m, x, mod, mod_is_rows, nw, w_out, w_router, b_router, tm, per_b):
    mm = m.shape[0]
    if mod_is_rows:
        mspec = lambda chunk: pl.BlockSpec((tm, D), lambda i: (i, chunk))
    else:
        mspec = lambda chunk: pl.BlockSpec((None, 1, D), lambda i: (i // per_b, 0, chunk))
    in_specs = [pl.BlockSpec((tm, D), lambda i: (i, 0)),
                pl.BlockSpec((tm, D), lambda i: (i, 0)),
                mspec(MOD_G1), mspec(MOD_SC2), mspec(MOD_SH2),
                pl.BlockSpec((1, D), lambda i: (0, 0)),
                pl.BlockSpec((D, D), lambda i: (0, 0), pipeline_mode=pl.Buffered(1)),
                pl.BlockSpec((D, 2 * LANES), lambda i: (0, 0)),
                pl.BlockSpec((1, LANES), lambda i: (0, 0))]
    args = [m, x, mod, mod, mod, nw, w_out, w_router, b_router]
    return pl.pallas_call(
        _post_kernel,
        grid=(mm // tm,),
        in_specs=in_specs,
        out_specs=[pl.BlockSpec((tm, D), lambda i: (i, 0)),
                   pl.BlockSpec((tm, D), lambda i: (i, 0)),
                   pl.BlockSpec((tm, TOP_K), lambda i: (i, 0)),
                   pl.BlockSpec((tm, TOP_K), lambda i: (i, 0)),
                   pl.BlockSpec((tm, TOP_K), lambda i: (i, 0)),
                   pl.BlockSpec((1, LANES), lambda i: (0, 0))],
        out_shape=[jax.ShapeDtypeStruct((mm, D), f32),
                   jax.ShapeDtypeStruct((mm, D), f32),
                   jax.ShapeDtypeStruct((mm, TOP_K), i32),
                   jax.ShapeDtypeStruct((mm, TOP_K), f32),
                   jax.ShapeDtypeStruct((mm, TOP_K), i32),
                   jax.ShapeDtypeStruct((1, LANES), f32)],
        compiler_params=_cparams(("arbitrary",)),
        name="post_mixer",
    )(*args)


DISP_TB = 128


DISP_PT = T_P // DISP_TB


def _dispatch_kernel(zblk_ref, dest_ref, h2p_ref, h2s_ref, xbuf_ref, zero_ref, sem):
    i = pl.program_id(0)

    def row_copy(src_ref, t, d):
        return pltpu.make_async_copy(src_ref.at[pl.ds(t, 1)], xbuf_ref.at[pl.ds(d, 1)], sem)

    def zero_copy(blk):
        return pltpu.make_async_copy(zero_ref, xbuf_ref.at[pl.ds(blk * MOE_BM, MOE_BM)], sem)

    @pl.when(i == 0)
    def _():
        zero_ref[...] = jnp.zeros_like(zero_ref)
        used = zblk_ref[N_EXPERTS]

        def zstart(e, c):
            zero_copy(zblk_ref[e]).start()
            return c

        def tstart(b, c):
            zero_copy(b).start()
            return c

        def zwait(e, c):
            zero_copy(0).wait()
            return c

        lax.fori_loop(0, N_EXPERTS, zstart, 0)
        lax.fori_loop(used, MOE_NBLK, tstart, 0)
        lax.fori_loop(0, N_EXPERTS, zwait, 0)
        lax.fori_loop(used, MOE_NBLK, zwait, 0)

    def issue(src_ref):
        for t in range(DISP_TB):
            for k in range(TOP_K):
                row_copy(src_ref, t, dest_ref[0, 0, t * TOP_K + k]).start(priority=k % 2)

    @pl.when(i < DISP_PT)
    def _():
        issue(h2p_ref)

    @pl.when(i == DISP_PT)
    def _():
        issue(h2s_ref)

    def wait_group(gi, c):
        for _ in range(SUBLANES * TOP_K):
            row_copy(h2p_ref, 0, 0).wait()
        return c

    lax.fori_loop(0, DISP_TB // SUBLANES, wait_group, 0)


def _dispatch(zero_blk, dest_all, h2_p, h2_s):
    nt = T_ALL // DISP_TB
    return pl.pallas_call(
        _dispatch_kernel,
        grid_spec=pltpu.PrefetchScalarGridSpec(
            num_scalar_prefetch=1,
            grid=(nt,),
            in_specs=[pl.BlockSpec((1, 1, DISP_TB * TOP_K), lambda i, zb: (i, 0, 0), memory_space=pltpu.SMEM),
                      pl.BlockSpec((DISP_TB, D), lambda i, zb: (jnp.minimum(i, DISP_PT - 1), 0)),
                      pl.BlockSpec((DISP_TB, D), lambda i, zb: (0, 0))],
            out_specs=pl.BlockSpec(memory_space=pl.ANY),
            scratch_shapes=[pltpu.VMEM((MOE_BM, D), f32), pltpu.SemaphoreType.DMA(())]),
        out_shape=jax.ShapeDtypeStruct((MOE_NROWS, D), f32),
        compiler_params=_cparams(("arbitrary",)),
        name="moe_dispatch",
    )(zero_blk, dest_all.reshape(nt, 1, DISP_TB * TOP_K), h2_p, h2_s)


def _moe_ffn_kernel(e_ref, sb_ref, nb_ref, xbuf_ref, wg_ref, wu_ref, bg_ref, bu_ref, wd_ref, bd_ref,
                    ybuf_ref, x_ref, acc_ref, wgb_ref, wub_ref, wdb_ref, sem_in, sem_out):
    p = pl.program_id(0)
    j = pl.program_id(1)
    nb = nb_ref[p]
    sb = sb_ref[p]

    def in_copy(r):
        return pltpu.make_async_copy(xbuf_ref.at[pl.ds((sb + r) * MOE_BM, MOE_BM)],
                                     x_ref.at[pl.ds(r * MOE_BM, MOE_BM)], sem_in.at[r])

    def out_copy(r):
        return pltpu.make_async_copy(acc_ref.at[pl.ds(r * MOE_BM, MOE_BM)],
                                     ybuf_ref.at[pl.ds((sb + r) * MOE_BM, MOE_BM)], sem_out)

    def for_blocks(fn):
        def body(r, c):
            fn(r)
            return c
        lax.fori_loop(0, nb, body, 0)

    @pl.when((p == 0) & (j == 0))
    def _():
        used = nb_ref[MOE_PMAX]
        acc_ref[0:MOE_BM, :] = jnp.zeros((MOE_BM, D), f32)

        def tail_copy(b):
            return pltpu.make_async_copy(acc_ref.at[pl.ds(0, MOE_BM)],
                                         ybuf_ref.at[pl.ds(b * MOE_BM, MOE_BM)], sem_out)

        def tstart(b, c):
            tail_copy(b).start()
            return c

        def twait(b, c):
            tail_copy(0).wait()
            return c

        lax.fori_loop(used, MOE_NBLK, tstart, 0)
        lax.fori_loop(used, MOE_NBLK, twait, 0)

    first = j == 0
    last = j == MOE_J - 1

    @pl.when(nb > 0)
    def _():
        @pl.when(first)
        def _():
            for_blocks(lambda r: in_copy(r).start())
            bd = jnp.broadcast_to(bd_ref[...], (MOE_BM, D))

            def init(r):
                acc_ref[pl.ds(pl.multiple_of(r * MOE_BM, MOE_BM), MOE_BM), :] = bd
            for_blocks(init)

        bg = bg_ref[...]
        bu = bu_ref[...]

        def cast_weights():
            wgb_ref[...] = wg_ref[...].astype(bf16)
            wub_ref[...] = wu_ref[...].astype(bf16)
            wdb_ref[...] = wd_ref[...].astype(bf16)

        def ffn_rows(b0, n_blk):
            rows = pl.ds(pl.multiple_of(b0 * MOE_BM, MOE_BM), n_blk * MOE_BM)
            x = x_ref[rows, :].astype(bf16)
            gate = jnp.dot(x, wgb_ref[...], preferred_element_type=f32) + bg
            up = jnp.dot(x, wub_ref[...], preferred_element_type=f32) + bu
            gate = jnp.minimum(gate, SWIGLU_LIMIT)
            up = jnp.clip(up, -SWIGLU_LIMIT, SWIGLU_LIMIT)
            act = (up + 1.0) * (gate * jax.nn.sigmoid(gate * SWIGLU_ALPHA))
            acc_ref[rows, :] += jnp.dot(act.astype(bf16), wdb_ref[...], preferred_element_type=f32)

        def trip(b0, groups, with_cast=False):
            n_blk = sum(groups)

            @pl.when(first)
            def _():
                for b in range(n_blk):
                    in_copy(b0 + b).wait()

            if with_cast:
                cast_weights()
            off = 0
            for g in groups:
                ffn_rows(b0 + off, g)
                off += g

            @pl.when(last)
            def _():
                for b in range(n_blk):
                    out_copy(b0 + b).start()

        full = (3, 3)
        per_trip = sum(full)
        ntrip = sum((nb >= per_trip * m).astype(i32) for m in range(1, MOE_RB // per_trip + 1))
        rem = nb - ntrip * per_trip
        head = jnp.where(rem == 0, per_trip, rem)
        for n_head, groups in ((1, (1,)), (2, (2,)), (3, (3,)), (4, (2, 2)), (5, (3, 2)), (per_trip, full)):
            @pl.when(head == n_head)
            def _(groups=groups):
                trip(0, groups, with_cast=True)

        def full_trip(q, c):
            trip(head + q * per_trip, full)
            return c
        lax.fori_loop(0, ntrip - (rem == 0).astype(i32), full_trip, 0)

        @pl.when(last)
        def _():
            for_blocks(lambda r: out_copy(r).wait())


def _moe_ffn(e_arr, sb_arr, nb_arr, xbuf, w_gate_up, b_gate_up, w_down, b_down):
    def jj(p, j, nb):
        return jnp.where(nb[p] > 0, j, MOE_J - 1)

    in_specs = [
        pl.BlockSpec(memory_space=pl.ANY),
        pl.BlockSpec((None, D, MOE_TF), lambda p, j, e, sb, nb: (e[p], 0, jj(p, j, nb))),
        pl.BlockSpec((None, D, MOE_TF), lambda p, j, e, sb, nb: (e[p], 0, MOE_J + jj(p, j, nb))),
        pl.BlockSpec((None, 1, MOE_TF), lambda p, j, e, sb, nb: (e[p], 0, jj(p, j, nb))),
        pl.BlockSpec((None, 1, MOE_TF), lambda p, j, e, sb, nb: (e[p], 0, MOE_J + jj(p, j, nb))),
        pl.BlockSpec((None, MOE_TF, D), lambda p, j, e, sb, nb: (e[p], jj(p, j, nb), 0)),
        pl.BlockSpec((None, 1, D), lambda p, j, e, sb, nb: (e[p], 0, 0)),
    ]
    return pl.pallas_call(
        _moe_ffn_kernel,
        grid_spec=pltpu.PrefetchScalarGridSpec(
            num_scalar_prefetch=3,
            grid=(MOE_PMAX, MOE_J),
            in_specs=in_specs,
            out_specs=pl.BlockSpec(memory_space=pl.ANY),
            scratch_shapes=[pltpu.VMEM((MOE_RB * MOE_BM, D), f32),
                            pltpu.VMEM((MOE_RB * MOE_BM, D), f32),
                            pltpu.VMEM((D, MOE_TF), bf16),
                            pltpu.VMEM((D, MOE_TF), bf16),
                            pltpu.VMEM((MOE_TF, D), bf16),
                            pltpu.SemaphoreType.DMA((MOE_RB,)),
                            pltpu.SemaphoreType.DMA(())]),
        out_shape=jax.ShapeDtypeStruct((MOE_NROWS, D), f32),
        compiler_params=_cparams(("arbitrary", "arbitrary"), vmem=56 * 1024 * 1024),
        name="moe_ffn",
    )(e_arr, sb_arr, nb_arr, xbuf, w_gate_up, w_gate_up,
      b_gate_up.reshape(N_EXPERTS, 1, 2 * D_EXPERT), b_gate_up.reshape(N_EXPERTS, 1, 2 * D_EXPERT),
      w_down, b_down.reshape(N_EXPERTS, 1, D))


COMB_TB = 128


def _combine_kernel(dcur_ref, dnext_ref, ybuf_ref, gate_ref, x1_ref, g2_ref, fw_ref, o_ref, rows_ref, sem):
    i = pl.program_id(0)
    slot = i & 1

    def row_copy(buf, n, d):
        return pltpu.make_async_copy(ybuf_ref.at[pl.ds(d, 1)], rows_ref.at[buf, pl.ds(n, 1)], sem.at[buf])

    def wait_tile(buf):
        def wait_group(gi, c):
            for _ in range(SUBLANES * TOP_K):
                row_copy(buf, 0, 0).wait()
            return c
        lax.fori_loop(0, COMB_TB // SUBLANES, wait_group, 0)

    @pl.when(i == 0)
    def _():
        def start_group(gi, c):
            for tt in range(SUBLANES):
                t = pl.multiple_of(gi * SUBLANES, SUBLANES) + tt
                for k in range(TOP_K):
                    d = dcur_ref[0, 0, (gi * SUBLANES + tt) * TOP_K + k]
                    row_copy(0, k * COMB_TB + t, d).start(priority=k % 2)
            return c
        lax.fori_loop(0, COMB_TB // SUBLANES, start_group, 0)

    wait_tile(slot)

    for t in range(COMB_TB):
        for k in range(TOP_K):
            row_copy(1 - slot, k * COMB_TB + t, dnext_ref[0, 0, t * TOP_K + k]).start(priority=k % 2)

    gates = gate_ref[...]
    f = gates[:, 0:1] * rows_ref[slot, 0:COMB_TB, :]
    for k in range(1, TOP_K):
        f = f + gates[:, k:k + 1] * rows_ref[slot, k * COMB_TB:(k + 1) * COMB_TB, :]
    x2 = x1_ref[...] + g2_ref[...] * f
    o_ref[...] = x2 * lax.rsqrt(jnp.mean(x2 * x2, axis=-1, keepdims=True) + EPS) * fw_ref[...]

    @pl.when(i == pl.num_programs(0) - 1)
    def _():
        wait_tile(1 - slot)


def _combine(dest, ybuf, gates, x1, mod, mod_is_rows, per_b, final_w):
    mm = x1.shape[0]
    nt = mm // COMB_TB
    dest3 = dest.reshape(nt, 1, COMB_TB * TOP_K)
    if mod_is_rows:
        g2spec = pl.BlockSpec((COMB_TB, D), lambda i: (i, MOD_G2))
    else:
        g2spec = pl.BlockSpec((None, 1, D), lambda i: (i // per_b, 0, MOD_G2))
    return pl.pallas_call(
        _combine_kernel,
        grid=(nt,),
        in_specs=[pl.BlockSpec((1, 1, COMB_TB * TOP_K), lambda i: (i, 0, 0), memory_space=pltpu.SMEM),
                  pl.BlockSpec((1, 1, COMB_TB * TOP_K), lambda i: (jnp.minimum(i + 1, nt - 1), 0, 0),
                               memory_space=pltpu.SMEM),
                  pl.BlockSpec(memory_space=pl.ANY),
                  pl.BlockSpec((COMB_TB, TOP_K), lambda i: (i, 0)),
                  pl.BlockSpec((COMB_TB, D), lambda i: (i, 0)),
                  g2spec,
                  pl.BlockSpec((1, D), lambda i: (0, 0))],
        out_specs=pl.BlockSpec((COMB_TB, D), lambda i: (i, 0)),
        out_shape=jax.ShapeDtypeStruct((mm, D), f32),
        scratch_shapes=[pltpu.VMEM((2, TOP_K * COMB_TB, D), f32), pltpu.SemaphoreType.DMA((2,))],
        compiler_params=_cparams(("arbitrary",)),
        name="moe_combine",
    )(dest3, dest3, ybuf, gates, x1, mod, final_w.reshape(1, D))


def _routing_tables(cnt_p, cnt_s, idx_p, rank_p, idx_s, rank_s):
    cp = cnt_p[0, :N_EXPERTS].astype(i32)
    cs = cnt_s[0, :N_EXPERTS].astype(i32)
    nblk = (cp + cs + MOE_BM - 1) // MOE_BM
    blk_end = jnp.cumsum(nblk)
    blk_start = blk_end - nblk
    row_start = blk_start * MOE_BM
    dest_p = row_start[idx_p.reshape(-1)] + rank_p.reshape(-1)
    dest_s = (row_start + cp)[idx_s.reshape(-1)] + rank_s.reshape(-1)
    used = blk_end[-1:]
    zero_blk = jnp.concatenate([jnp.maximum(blk_end - 1, 0), used]).astype(i32)
    npass = (nblk + MOE_RB - 1) // MOE_RB
    pass_end = jnp.cumsum(npass)
    total = pass_end[-1]
    pid = jnp.arange(MOE_PMAX, dtype=i32)
    pid_c = jnp.minimum(pid, total - 1)
    pe = jnp.minimum(jnp.sum((pass_end[None, :] <= pid_c[:, None]).astype(i32), axis=1), N_EXPERTS - 1)
    local = pid_c - (pass_end - npass)[pe]
    active = pid < total
    sb = jnp.where(active, blk_start[pe] + local * MOE_RB, 0).astype(i32)
    nb = jnp.where(active, jnp.clip(nblk[pe] - local * MOE_RB, 0, MOE_RB), 0)
    nb = jnp.concatenate([nb, used]).astype(i32)
    return dest_p.astype(i32), dest_s.astype(i32), zero_blk, pe, sb, nb


def kernel(x_prompt, x_sample, c_prompt, c_sample, state_ret, norm1_w, norm2_w, w_ada, b_ada, w_in,
           ret_gn_w, gm_ln_w, gm_ln_b, gm_ws, gm_bs, w_oa, w_ob, w_out, w_router, b_router,
           w_gate_up, b_gate_up, w_down, b_down, final_norm_w):
    x_p = x_prompt.reshape(T_P, D)
    x_s = x_sample.reshape(T_S, D)
    per_b = lambda tm: SEQ // tm

    mod = _ada(jnp.concatenate([c_prompt, c_sample], axis=0), w_ada[0], b_ada[0])
    mod_p = mod[:BATCH].reshape(BATCH, 1, 6 * D)
    mod_s = mod[BATCH:]

    n1 = norm1_w[0].reshape(1, D)
    h_all = _norm1(x_p, x_s, n1, mod_p, mod_s)
    z_all = _proj_in(h_all, w_in[0])

    woa, wob, wout = w_oa[0].astype(bf16), w_ob[0].astype(bf16), w_out[0].astype(bf16)
    m_p, s_p = _mixer_prompt(z_all, ret_gn_w[0], gm_ws[0], gm_bs[0], gm_ln_w[0], gm_ln_b[0], woa, wob)
    z_s = z_all[T_P:]
    head_lanes = lambda t: jnp.pad(t.reshape(T_S, HEADS, DK).transpose(0, 2, 1),
                                   ((0, 0), (0, 0), (0, LANES - HEADS)))
    qt = head_lanes(z_s[:, :HEADS * DK])
    kt = head_lanes(z_s[:, HEADS * DK:2 * HEADS * DK])
    v3 = z_s[:, 2 * HEADS * DK:2 * HEADS * DK + HEADS * DV].reshape(T_S, HEADS, DV)
    g3 = z_s[:, 2 * HEADS * DK + HEADS * DV:2 * HEADS * DK + 2 * HEADS * DV].reshape(T_S, HEADS, DV)
    a_s, s_s = _ret_sample(qt, kt, v3, g3, ret_gn_w[0], state_ret[0])
    vn_s, bm_s = _gmlp_sample(z_all, gm_ws[0], gm_bs[0], gm_ln_w[0], gm_ln_b[0])
    m_s =_merge(a_s.reshape(T_S, HEADS * DV), bm_s, z_all, woa, wob, T_S, T_P // T_S)

    n2 = norm2_w[0].reshape(1, D)
    wr_hi = w_router[0].astype(bf16)
    wr_lo = (w_router[0] - wr_hi.astype(f32)).astype(bf16)
    lane_pad = ((0, 0), (0, LANES - N_EXPERTS))
    wr_cat = jnp.concatenate([jnp.pad(wr_hi, lane_pad), jnp.pad(wr_lo, lane_pad)], axis=1)
    br = jnp.pad(b_router[0].reshape(1, N_EXPERTS), lane_pad)
    x1_p, h2_p, idx_p, gate_p, rank_p, cnt_p = _post(
        m_p, x_p, mod_p, False, n2, wout, wr_cat, br, 512, per_b(512))
    x1_s, h2_s, idx_s, gate_s, rank_s, cnt_s = _post(
        m_s, x_s, mod_s, True, n2, wout, wr_cat, br, T_S, 1)

    dest_p, dest_s, zero_blk, pe, sb, nb = _routing_tables(cnt_p, cnt_s, idx_p, rank_p, idx_s, rank_s)
    xbuf = _dispatch(zero_blk, jnp.concatenate([dest_p, dest_s], axis=0), h2_p, h2_s)
    ybuf = _moe_ffn(pe, sb, nb, xbuf, w_gate_up[0], b_gate_up[0], w_down[0], b_down[0])
    y_p = _combine(dest_p, ybuf, gate_p, x1_p, mod_p, False, per_b(COMB_TB), final_norm_w)
    y_s = _combine(dest_s, ybuf, gate_s, x1_s, mod_s, True, 1, final_norm_w)

    return (y_p.reshape(BATCH, SEQ, D),
            y_s.reshape(DEC_BATCH, 1, D),
            s_p.reshape(1, BATCH, HEADS, DK, DV),
            s_s.reshape(1, DEC_BATCH, HEADS, DK, DV),
            vn_s.reshape(1, DEC_BATCH, 1, GW))
```

```python
import functools

import numpy as np
import jax
import jax.numpy as jnp
from jax import lax
from jax.experimental import pallas as pl
from jax.experimental.pallas import tpu as pltpu

f32 = jnp.float32
bf16 = jnp.bfloat16
i32 = jnp.int32

D = 2048
BATCH = 4
SEQ = 2048
DEC_BATCH = 128
PAST_LEN = 16384
HEADS = 8
DK = 128
DV = 256
CHUNK = 128
ROPE_THETA = 10000.0
GROUPS = 8
GW = 2048
GC = GW // GROUPS
N_EXPERTS = 32
TOP_K = 4
D_EXPERT = 2048
SWIGLU_LIMIT = 7.0
SWIGLU_ALPHA = 1.702
EPS = 1e-6

T_P = BATCH * SEQ
T_S = DEC_BATCH
T_ALL = T_P + T_S
D_IN = 14336
COL_Q, COL_K = 0, 1
COL_V, COL_G, COL_U, COL_GV, COL_GA, COL_GB = 1, 2, 3, 4, 5, 6
MOD_SH1, MOD_SC1, MOD_G1, MOD_SH2, MOD_SC2, MOD_G2 = range(6)

LANES = 128
SUBLANES = 8
VMEM_LIMIT = 48 * 1024 * 1024

MOE_BM = 128
MOE_RB = 12
MOE_TF = 256
MOE_J = D_EXPERT // MOE_TF
TK_ALL = T_ALL * TOP_K
MOE_NBLK = (TK_ALL + N_EXPERTS * (MOE_BM - 1)) // MOE_BM
MOE_NROWS = MOE_NBLK * MOE_BM
MOE_PMAX = N_EXPERTS + (MOE_NBLK - N_EXPERTS) // MOE_RB


def _cparams(sem, vmem=VMEM_LIMIT):
    return pltpu.CompilerParams(dimension_semantics=sem, vmem_limit_bytes=vmem)


def _ada_kernel(c_ref, w_ref, b_ref, o_ref):
    o_ref[...] = jnp.dot(c_ref[...].astype(bf16), w_ref[...].astype(bf16),
                         preferred_element_type=f32) + b_ref[...]


def _ada(c_all, w_ada, b_ada):
    m = c_all.shape[0]
    tn = 1024
    return pl.pallas_call(
        _ada_kernel,
        grid=(6 * D // tn,),
        in_specs=[pl.BlockSpec((m, D), lambda j: (0, 0)),
                  pl.BlockSpec((D, tn), lambda j: (0, j)),
                  pl.BlockSpec((1, tn), lambda j: (0, j))],
        out_specs=pl.BlockSpec((m, tn), lambda j: (0, j)),
        out_shape=jax.ShapeDtypeStruct((m, 6 * D), f32),
        compiler_params=_cparams(("arbitrary",)),
        name="ada_mod",
    )(c_all, w_ada, b_ada.reshape(1, 6 * D))


def _norm_mod_kernel(x_ref, nw_ref, sc_ref, sh_ref, o_ref):
    x = x_ref[...]
    y = x * lax.rsqrt(jnp.mean(x * x, axis=-1, keepdims=True) + EPS) * nw_ref[...]
    o_ref[...] = (y * (1.0 + sc_ref[...]) + sh_ref[...]).astype(o_ref.dtype)


NORM1_TM = T_S
NORM1_PT = T_P // NORM1_TM


def _norm1_kernel(xp_ref, xs_ref, nw_ref, scp_ref, shp_ref, scs_ref, shs_ref, o_ref):
    i = pl.program_id(0)

    @pl.when(i < NORM1_PT)
    def _():
        _norm_mod_kernel(xp_ref, nw_ref, scp_ref, shp_ref, o_ref)

    @pl.when(i == NORM1_PT)
    def _():
        _norm_mod_kernel(xs_ref, nw_ref, scs_ref, shs_ref, o_ref)


def _norm1(x_p, x_s, nw, mod_p, mod_s):
    tm = NORM1_TM
    per_b = SEQ // tm
    pt = lambda i: jnp.minimum(i, NORM1_PT - 1)
    return pl.pallas_call(
        _norm1_kernel,
        grid=(NORM1_PT + 1,),
        in_specs=[pl.BlockSpec((tm, D), lambda i: (pt(i), 0)),
                  pl.BlockSpec((tm, D), lambda i: (0, 0)),
                  pl.BlockSpec((1, D), lambda i: (0, 0)),
                  pl.BlockSpec((None, 1, D), lambda i: (pt(i) // per_b, 0, MOD_SC1)),
                  pl.BlockSpec((None, 1, D), lambda i: (pt(i) // per_b, 0, MOD_SH1)),
                  pl.BlockSpec((tm, D), lambda i: (0, MOD_SC1)),
                  pl.BlockSpec((tm, D), lambda i: (0, MOD_SH1))],
        out_specs=pl.BlockSpec((tm, D), lambda i: (i, 0)),
        out_shape=jax.ShapeDtypeStruct((T_ALL, D), bf16),
        compiler_params=_cparams(("arbitrary",)),
        name="norm1",
    )(x_p, x_s, nw, mod_p, mod_p, mod_s, mod_s)


def _proj_in_kernel(h_ref, w_ref, o_ref, wb_ref):
    @pl.when(pl.program_id(1) == 0)
    def _():
        wb_ref[...] = w_ref[...].astype(bf16)

    o_ref[...] = jnp.dot(h_ref[...], wb_ref[...], preferred_element_type=f32)


def _proj_in(h_all, w_in):
    tm, tn = 640, 1024
    return pl.pallas_call(
        _proj_in_kernel,
        grid=(D_IN // tn, T_ALL // tm),
        in_specs=[pl.BlockSpec((tm, D), lambda j, i: (i, 0)),
                  pl.BlockSpec((D, tn), lambda j, i: (0, j))],
        out_specs=pl.BlockSpec((tm, tn), lambda j, i: (i, j)),
        out_shape=jax.ShapeDtypeStruct((T_ALL, D_IN), f32),
        scratch_shapes=[pltpu.VMEM((D, tn), bf16)],
        compiler_params=_cparams(("arbitrary", "arbitrary")),
        name="proj_in",
    )(h_all, w_in)


def _retention_consts():
    h = np.arange(HEADS, dtype=np.float64)
    lg = np.log1p(-np.exp2(-5.0 - h))
    idx = np.arange(CHUNK, dtype=np.float64)
    diff = idx[:, None] - idx[None, :]
    intra = np.where(diff >= 0, np.exp(np.maximum(diff, 0.0)[None] * lg[:, None, None]), 0.0)
    q_dec = np.exp((idx + 1.0)[:, None] * lg[None, :])
    k_dec = np.exp((CHUNK - 1.0 - idx)[:, None] * lg[None, :])
    s_dec = np.exp(CHUNK * lg)
    gamma = np.exp(lg)
    return (intra.astype(np.float32), q_dec.astype(np.float32), k_dec.astype(np.float32),
            [float(v) for v in s_dec], [float(v) for v in gamma])


def _rope_tables(pos):
    half = DK // 2
    freq = ROPE_THETA ** (-np.arange(half, dtype=np.float64) / half)
    ang = np.asarray(pos, dtype=np.float64)[:, None] * freq[None, :]
    cos, sin = np.cos(ang), np.sin(ang)
    cos_t = np.concatenate([cos, cos], axis=-1).astype(np.float32)
    sin_t = np.concatenate([-sin, sin], axis=-1).astype(np.float32)
    return cos_t, sin_t


def _group_norm_gate(o, gnw, g):
    mu = jnp.mean(o, axis=-1, keepdims=True)
    var = jnp.mean(jnp.square(o - mu), axis=-1, keepdims=True)
    on = (o - mu) * lax.rsqrt(var + EPS) * gnw
    return jax.nn.silu(g) * on


def _retention_chunk(s_dec, live, q_ref, k_ref, v_ref, g_ref, cos_ref, sin_ref, intra_ref, qd_ref, kd_ref,
                     gnw_ref, s_ref, a_ref, slot):
    cos = cos_ref[...]
    sin = sin_ref[...]
    qd = qd_ref[...]
    kd = kd_ref[...]
    for h in range(HEADS):
        qh = q_ref[:, h * DK:(h + 1) * DK]
        kh = k_ref[:, h * DK:(h + 1) * DK]
        qh = (qh * cos + pltpu.roll(qh, DK // 2, 1) * sin) * (DK ** -0.5)
        kh = kh * cos + pltpu.roll(kh, DK // 2, 1) * sin
        vh = v_ref[:, h * DV:(h + 1) * DV].astype(bf16)
        s_old = s_ref[h]
        scores = lax.dot_general(qh.astype(bf16), kh.astype(bf16), (((1,), (1,)), ((), ())),
                                 preferred_element_type=f32) * intra_ref[h]
        o = jnp.dot(scores.astype(bf16), vh, preferred_element_type=f32)
        o = o + jnp.dot((qh * qd[:, h:h + 1]).astype(bf16), s_old.astype(bf16), preferred_element_type=f32)
        kw_t = (kh * kd[:, h:h + 1]).T.astype(bf16)
        s_new = s_old * s_dec[h] + jnp.dot(kw_t, vh, preferred_element_type=f32)
        s_ref[h] = jnp.where(live, s_new, s_old)
        a_ref[slot, :, h * DV:(h + 1) * DV] = _group_norm_gate(
            o, gnw_ref[:, h * DV:(h + 1) * DV], g_ref[:, h * DV:(h + 1) * DV]).astype(a_ref.dtype)


RET_S_BT = SUBLANES


def _ret_sample_kernel(gamma, qt_ref, kt_ref, v_ref, g_ref, cos_ref, sin_ref, spread_ref, gnw_ref, s_in_ref,
                       a_ref, s_out_ref):
    cos = cos_ref[...]
    sin = sin_ref[...]

    def rope_t(x):
        rolled = jnp.concatenate([x[DK // 2:], x[:DK // 2]], axis=0)
        return x * cos + rolled * sin

    def spread(x):
        hi = x.astype(bf16).astype(f32)
        r1 = x - hi
        mid = r1.astype(bf16).astype(f32)
        lo = r1 - mid
        pieces = hi + pltpu.roll(mid, HEADS, 1) + pltpu.roll(lo, 2 * HEADS, 1)
        return jnp.dot(pieces.astype(bf16), spread_ref[...], preferred_element_type=f32)

    def head_cols(row_ref, t):
        rows = [row_ref[pl.ds(t, 1), h * DK:(h + 1) * DK] for h in range(HEADS)]
        return jnp.concatenate(rows + [jnp.zeros((LANES - HEADS, DK), f32)], axis=0).T

    for t in range(RET_S_BT):
        qb = spread(rope_t(head_cols(qt_ref, t)) * (DK ** -0.5))
        kb = spread(rope_t(head_cols(kt_ref, t)))
        row = pl.ds(t, 1)
        for h in range(HEADS):
            hs = slice(h * DV, (h + 1) * DV)
            s_new = s_in_ref[t, h] * gamma[h] + kb[:, hs] * v_ref[row, hs]
            s_out_ref[t, h] = s_new
            o = jnp.sum(qb[:, hs] * s_new, axis=0, keepdims=True)
            a_ref[row, hs] = _group_norm_gate(o, gnw_ref[:, hs], g_ref[row, hs])


def _ret_sample(z_all, gn_w, state):
    _, _, _, _, gamma = _retention_consts()
    cos_t, sin_t = _rope_tables(np.array([PAST_LEN]))
    cos8 = np.ascontiguousarray(np.broadcast_to(cos_t[0][:, None], (DK, LANES)))
    sin8 = np.ascontiguousarray(np.broadcast_to(sin_t[0][:, None], (DK, LANES)))
    spread_m = np.zeros((LANES, HEADS * DV), np.float32)
    for piece in range(3):
        for h in range(HEADS):
            spread_m[piece * HEADS + h, h * DV:(h + 1) * DV] = 1.0
    bt = RET_S_BT
    blk0 = T_P // bt
    return pl.pallas_call(
        functools.partial(_ret_sample_kernel, gamma),
        grid=(T_S // bt,),
        in_specs=[pl.BlockSpec((bt, HEADS * DK), lambda i: (blk0 + i, COL_Q)),
                  pl.BlockSpec((bt, HEADS * DK), lambda i: (blk0 + i, COL_K)),
                  pl.BlockSpec((bt, HEADS * DV), lambda i: (blk0 + i, COL_V)),
                  pl.BlockSpec((bt, HEADS * DV), lambda i: (blk0 + i, COL_G)),
                  pl.BlockSpec((DK, LANES), lambda i: (0, 0)),
                  pl.BlockSpec((DK, LANES), lambda i: (0, 0)),
                  pl.BlockSpec((LANES, HEADS * DV), lambda i: (0, 0)),
                  pl.BlockSpec((1, HEADS * DV), lambda i: (0, 0)),
                  pl.BlockSpec((bt, HEADS, DK, DV), lambda i: (i, 0, 0, 0))],
        out_specs=[pl.BlockSpec((bt, HEADS * DV), lambda i: (i, 0)),
                   pl.BlockSpec((bt, HEADS, DK, DV), lambda i: (i, 0, 0, 0))],
        out_shape=[jax.ShapeDtypeStruct((T_S, HEADS * DV), f32),
                   jax.ShapeDtypeStruct((T_S, HEADS, DK, DV), f32)],
        compiler_params=_cparams(("arbitrary",)),
        name="retention_sample",
    )(z_all, z_all, z_all, z_all, jnp.asarray(cos8), jnp.asarray(sin8), jnp.asarray(spread_m, dtype=bf16),
      gn_w.reshape(1, HEADS * DV), state)


def _layer_norm(x, w, b):
    mu = jnp.mean(x, axis=-1, keepdims=True)
    var = jnp.mean(jnp.square(x - mu), axis=-1, keepdims=True)
    return (x - mu) * lax.rsqrt(var + EPS) * w + b


def _gmlp_chunk(u_ref, gv_ref, ws_ref, bst_ref, lnw_ref, lnb_ref, o_ref, slot):
    vn = _layer_norm(jax.nn.gelu(gv_ref[...]), lnw_ref[...], lnb_ref[...]).astype(bf16)
    r = lax.broadcasted_iota(i32, (CHUNK, CHUNK), 0)
    c = lax.broadcasted_iota(i32, (CHUNK, CHUNK), 1)
    causal = r >= c
    bst = bst_ref[...]
    for g in range(GROUPS):
        w = jnp.where(causal, ws_ref[g], 0.0).astype(bf16)
        mixed = jnp.dot(w, vn[:, g * GC:(g + 1) * GC], preferred_element_type=f32) + bst[:, g:g + 1]
        o_ref[slot, :, g * GC:(g + 1) * GC] = (
            jax.nn.gelu(u_ref[:, g * GC:(g + 1) * GC]) * mixed).astype(o_ref.dtype)


def _gmlp_sample_kernel(u_ref, gv_ref, w0_ref, b0_ref, lnw_ref, lnb_ref, vn_ref, o_ref):
    vn = _layer_norm(jax.nn.gelu(gv_ref[...]), lnw_ref[...], lnb_ref[...])
    vn_ref[...] = vn
    o_ref[...] = (jax.nn.gelu(u_ref[...]) * (vn * w0_ref[...] + b0_ref[...])).astype(o_ref.dtype)


def _gmlp_sample(z_all, gm_ws, gm_bs, ln_w, ln_b):
    w0 = jnp.repeat(gm_ws[:, 0, 0], GC).reshape(1, GW)
    b0 = jnp.repeat(gm_bs[:, 0], GC).reshape(1, GW)
    blk = T_P // T_S
    return pl.pallas_call(
        _gmlp_sample_kernel,
        grid=(1,),
        in_specs=[pl.BlockSpec((T_S, GW), lambda i: (blk, COL_U)),
                  pl.BlockSpec((T_S, GW), lambda i: (blk, COL_GV)),
                  pl.BlockSpec((1, GW), lambda i: (0, 0)),
                  pl.BlockSpec((1, GW), lambda i: (0, 0)),
                  pl.BlockSpec((1, GW), lambda i: (0, 0)),
                  pl.BlockSpec((1, GW), lambda i: (0, 0))],
        out_specs=[pl.BlockSpec((T_S, GW), lambda i: (0, 0)),
                   pl.BlockSpec((T_S, GW), lambda i: (0, 0))],
        out_shape=[jax.ShapeDtypeStruct((T_S, GW), f32),
                   jax.ShapeDtypeStruct((T_S, GW), bf16)],
        compiler_params=_cparams(("arbitrary",)),
        name="gmlp_sample",
    )(z_all, z_all, w0, b0, ln_w.reshape(1, GW), ln_b.reshape(1, GW))


def _merge_kernel(a_ref, b_ref, ga_ref, gb_ref, woa_ref, wob_ref, o_ref):
    ya = jnp.dot(a_ref[...].astype(bf16), woa_ref[...], preferred_element_type=f32)
    yb = jnp.dot(b_ref[...].astype(bf16), wob_ref[...], preferred_element_type=f32)
    o_ref[...] = (jax.nn.sigmoid(ga_ref[...]) * ya + jax.nn.sigmoid(gb_ref[...]) * yb).astype(o_ref.dtype)


def _merge(a, bm, z_all, w_oa, w_ob, tm, z_blk0):
    m = a.shape[0]
    resident = lambda: pl.BlockSpec((D, D), lambda i: (0, 0), pipeline_mode=pl.Buffered(1))
    return pl.pallas_call(
        _merge_kernel,
        grid=(m // tm,),
        in_specs=[pl.BlockSpec((tm, D), lambda i: (i, 0)),
                  pl.BlockSpec((tm, D), lambda i: (i, 0)),
                  pl.BlockSpec((tm, D), lambda i: (z_blk0 + i, COL_GA)),
                  pl.BlockSpec((tm, D), lambda i: (z_blk0 + i, COL_GB)),
                  resident(), resident()],
        out_specs=pl.BlockSpec((tm, D), lambda i: (i, 0)),
        out_shape=jax.ShapeDtypeStruct((m, D), bf16),
        compiler_params=_cparams(("arbitrary",)),
        name="merge",
    )(a, bm, z_all, z_all, w_oa, w_ob)


MIX_NCH = SEQ // CHUNK
MIX_NSTEP = BATCH * MIX_NCH


def _mixer_prompt_kernel(s_dec, q_ref, k_ref, v_ref, g_ref, u_ref, gv_ref, ga_ref, gb_ref,
                         cos_ref, sin_ref, intra_ref, qd_ref, kd_ref, gnw_ref,
                         ws_ref, bst_ref, lnw_ref, lnb_ref, woa_ref, wob_ref,
                         m_ref, s_ref, a_scr, b_scr):
    s = pl.program_id(0)
    live = s < MIX_NSTEP
    c = jnp.minimum(s, MIX_NSTEP - 1)
    cur = s & 1

    @pl.when(s == 0)
    def _():
        a_scr[...] = jnp.zeros_like(a_scr)
        b_scr[...] = jnp.zeros_like(b_scr)

    @pl.when((c & (MIX_NCH - 1)) == 0)
    def _():
        s_ref[...] = jnp.zeros_like(s_ref)

    ya = jnp.dot(a_scr[1 - cur], woa_ref[...], preferred_element_type=f32)
    yb = jnp.dot(b_scr[1 - cur], wob_ref[...], preferred_element_type=f32)
    m_ref[...] = (jax.nn.sigmoid(ga_ref[...]) * ya + jax.nn.sigmoid(gb_ref[...]) * yb).astype(m_ref.dtype)

    _retention_chunk(s_dec, live, q_ref, k_ref, v_ref, g_ref, cos_ref, sin_ref, intra_ref, qd_ref, kd_ref,
                     gnw_ref, s_ref, a_scr, cur)
    _gmlp_chunk(u_ref, gv_ref, ws_ref, bst_ref, lnw_ref, lnb_ref, b_scr, cur)


def _mixer_prompt(z_all, gn_w, gm_ws, gm_bs, ln_w, ln_b, w_oa, w_ob):
    intra, q_dec, k_dec, s_dec, _ = _retention_consts()
    cos_t, sin_t = _rope_tables(np.arange(SEQ))
    cur = lambda s: jnp.minimum(s, MIX_NSTEP - 1)
    lag = lambda s: jnp.maximum(s - 1, 0)
    zspec = lambda width, row, col: pl.BlockSpec((CHUNK, width), lambda s: (row(s), col))
    const = lambda shape: pl.BlockSpec(shape, lambda s: (0,) * len(shape))
    resident = lambda: pl.BlockSpec((D, D), lambda s: (0, 0), pipeline_mode=pl.Buffered(1))
    return pl.pallas_call(
        functools.partial(_mixer_prompt_kernel, s_dec),
        grid=(MIX_NSTEP + 1,),
        in_specs=[zspec(HEADS * DK, cur, COL_Q), zspec(HEADS * DK, cur, COL_K),
                  zspec(HEADS * DV, cur, COL_V), zspec(HEADS * DV, cur, COL_G),
                  zspec(GW, cur, COL_U), zspec(GW, cur, COL_GV),
                  zspec(D, lag, COL_GA), zspec(D, lag, COL_GB),
                  pl.BlockSpec((CHUNK, DK), lambda s: (cur(s) & (MIX_NCH - 1), 0)),
                  pl.BlockSpec((CHUNK, DK), lambda s: (cur(s) & (MIX_NCH - 1), 0)),
                  const((HEADS, CHUNK, CHUNK)), const((CHUNK, HEADS)), const((CHUNK, HEADS)),
                  const((1, HEADS * DV)),
                  const((GROUPS, CHUNK, CHUNK)), const((CHUNK, GROUPS)), const((1, GW)), const((1, GW)),
                  resident(), resident()],
        out_specs=[pl.BlockSpec((CHUNK, D), lambda s: (lag(s), 0)),
                   pl.BlockSpec((None, HEADS, DK, DV), lambda s: (cur(s) // MIX_NCH, 0, 0, 0))],
        out_shape=[jax.ShapeDtypeStruct((T_P, D), bf16),
                   jax.ShapeDtypeStruct((BATCH, HEADS, DK, DV), f32)],
        scratch_shapes=[pltpu.VMEM((2, CHUNK, HEADS * DV), bf16), pltpu.VMEM((2, CHUNK, GW), bf16)],
        compiler_params=_cparams(("arbitrary",)),
        name="mixer_prompt",
    )(z_all, z_all, z_all, z_all, z_all, z_all, z_all, z_all,
      jnp.asarray(cos_t), jnp.asarray(sin_t), jnp.asarray(intra), jnp.asarray(q_dec), jnp.asarray(k_dec),
      gn_w.reshape(1, HEADS * DV), gm_ws, gm_bs.T, ln_w.reshape(1, GW), ln_b.reshape(1, GW), w_oa, w_ob)


def _post_kernel(m_ref, x_ref, g1_ref, sc_ref, sh_ref, nw_ref, wout_ref, wr_ref, br_ref,
                 x1_ref, h2_ref, idx_ref, gate_ref, rank_ref, cnt_ref):
    i = pl.program_id(0)

    @pl.when(i == 0)
    def _():
        cnt_ref[...] = jnp.zeros_like(cnt_ref)

    tm = m_ref.shape[0]
    y = jnp.dot(m_ref[...], wout_ref[...], preferred_element_type=f32)
    x1 = x_ref[...] + g1_ref[...] * y
    x1_ref[...] = x1
    xn = x1 * lax.rsqrt(jnp.mean(x1 * x1, axis=-1, keepdims=True) + EPS) * nw_ref[...]
    h2 = xn * (1.0 + sc_ref[...]) + sh_ref[...]
    h2_ref[...] = h2

    h_hi = h2.astype(bf16)
    h_lo = (h2 - h_hi.astype(f32)).astype(bf16)
    p_hi = jnp.dot(h_hi, wr_ref[...], preferred_element_type=f32)
    p_lo = jnp.dot(h_lo, wr_ref[...], preferred_element_type=f32)
    logits = p_hi[:, :LANES] + p_hi[:, LANES:] + p_lo[:, :LANES] + br_ref[...]

    lane = lax.broadcasted_iota(i32, (tm, LANES), 1).astype(f32)
    col = lax.broadcasted_iota(i32, (tm, TOP_K), 1)
    work = jnp.where(lane < N_EXPERTS, logits, -jnp.inf)
    member = jnp.zeros((tm, LANES), f32)
    vals, sels = [], []
    idx_out = jnp.zeros((tm, TOP_K), i32)
    for k in range(TOP_K):
        mx = jnp.max(work, axis=1, keepdims=True)
        ix = jnp.min(jnp.where(work == mx, lane, float(LANES)), axis=1, keepdims=True)
        sel = lane == ix
        vals.append(mx)
        sels.append(sel)
        idx_out = jnp.where(col == k, ix.astype(i32), idx_out)
        member = jnp.where(sel, 1.0, member)
        work = jnp.where(sel, -jnp.inf, work)
    idx_ref[...] = idx_out

    exps = [jnp.exp(v - vals[0]) for v in vals]
    den = exps[0] + exps[1] + exps[2] + exps[3]
    gate_out = jnp.zeros((tm, TOP_K), f32)
    for k in range(TOP_K):
        gate_out = jnp.where(col == k, exps[k] / den, gate_out)
    gate_ref[...] = gate_out

    r = lax.broadcasted_iota(i32, (tm, tm), 0)
    c = lax.broadcasted_iota(i32, (tm, tm), 1)
    lower = jnp.where(c < r, 1.0, 0.0).astype(bf16)
    before = jnp.dot(lower, member.astype(bf16), preferred_element_type=f32) + cnt_ref[...]
    rank_out = jnp.zeros((tm, TOP_K), i32)
    for k in range(TOP_K):
        rk = jnp.sum(jnp.where(sels[k], before, 0.0), axis=1, keepdims=True)
        rank_out = jnp.where(col == k, rk.astype(i32), rank_out)
    rank_ref[...] = rank_out
    cnt_ref[...] += jnp.sum(member, axis=0, keepdims=True)


def _post(m, x, mod, mod_is_rows, nw, w_out, w_router, b_router, tm, per_b):
    mm = m.shape[0]
    if mod_is_rows:
        mspec = lambda chunk: pl.BlockSpec((tm, D), lambda i: (i, chunk))
    else:
        mspec = lambda chunk: pl.BlockSpec((None, 1, D), lambda i: (i // per_b, 0, chunk))
    in_specs = [pl.BlockSpec((tm, D), lambda i: (i, 0)),
                pl.BlockSpec((tm, D), lambda i: (i, 0)),
                mspec(MOD_G1), mspec(MOD_SC2), mspec(MOD_SH2),
                pl.BlockSpec((1, D), lambda i: (0, 0)),
                pl.BlockSpec((D, D), lambda i: (0, 0), pipeline_mode=pl.Buffered(1)),
                pl.BlockSpec((D, 2 * LANES), lambda i: (0, 0)),
                pl.BlockSpec((1, LANES), lambda i: (0, 0))]
    args = [m, x, mod, mod, mod, nw, w_out, w_router, b_router]
    return pl.pallas_call(
        _post_kernel,
        grid=(mm // tm,),
        in_specs=in_specs,
        out_specs=[pl.BlockSpec((tm, D), lambda i: (i, 0)),
                   pl.BlockSpec((tm, D), lambda i: (i, 0)),
                   pl.BlockSpec((tm, TOP_K), lambda i: (i, 0)),
                   pl.BlockSpec((tm, TOP_K), lambda i: (i, 0)),
                   pl.BlockSpec((tm, TOP_K), lambda i: (i, 0)),
                   pl.BlockSpec((1, LANES), lambda i: (0, 0))],
        out_shape=[jax.ShapeDtypeStruct((mm, D), f32),
                   jax.ShapeDtypeStruct((mm, D), f32),
                   jax.ShapeDtypeStruct((mm, TOP_K), i32),
                   jax.ShapeDtypeStruct((mm, TOP_K), f32),
                   jax.ShapeDtypeStruct((mm, TOP_K), i32),
                   jax.ShapeDtypeStruct((1, LANES), f32)],
        compiler_params=_cparams(("arbitrary",)),
        name="post_mixer",
    )(*args)


DISP_TB = 128


DISP_PT = T_P // DISP_TB


def _dispatch_kernel(zblk_ref, dest_ref, h2p_ref, h2s_ref, xbuf_ref, zero_ref, sem):
    i = pl.program_id(0)

    def row_copy(src_ref, t, d):
        return pltpu.make_async_copy(src_ref.at[pl.ds(t, 1)], xbuf_ref.at[pl.ds(d, 1)], sem)

    def zero_copy(blk):
        return pltpu.make_async_copy(zero_ref, xbuf_ref.at[pl.ds(blk * MOE_BM, MOE_BM)], sem)

    @pl.when(i == 0)
    def _():
        zero_ref[...] = jnp.zeros_like(zero_ref)
        used = zblk_ref[N_EXPERTS]

        def zstart(e, c):
            zero_copy(zblk_ref[e]).start()
            return c

        def tstart(b, c):
            zero_copy(b).start()
            return c

        def zwait(e, c):
            zero_copy(0).wait()
            return c

        lax.fori_loop(0, N_EXPERTS, zstart, 0)
        lax.fori_loop(used, MOE_NBLK, tstart, 0)
        lax.fori_loop(0, N_EXPERTS, zwait, 0)
        lax.fori_loop(used, MOE_NBLK, zwait, 0)

    def issue(src_ref):
        for t in range(DISP_TB):
            for k in range(TOP_K):
                row_copy(src_ref, t, dest_ref[0, 0, t * TOP_K + k]).start(priority=k % 2)

    @pl.when(i < DISP_PT)
    def _():
        issue(h2p_ref)

    @pl.when(i == DISP_PT)
    def _():
        issue(h2s_ref)

    def wait_group(gi, c):
        for _ in range(SUBLANES * TOP_K):
            row_copy(h2p_ref, 0, 0).wait()
        return c

    lax.fori_loop(0, DISP_TB // SUBLANES, wait_group, 0)


def _dispatch(zero_blk, dest_all, h2_p, h2_s):
    nt = T_ALL // DISP_TB
    return pl.pallas_call(
        _dispatch_kernel,
        grid_spec=pltpu.PrefetchScalarGridSpec(
            num_scalar_prefetch=1,
            grid=(nt,),
            in_specs=[pl.BlockSpec((1, 1, DISP_TB * TOP_K), lambda i, zb: (i, 0, 0), memory_space=pltpu.SMEM),
                      pl.BlockSpec((DISP_TB, D), lambda i, zb: (jnp.minimum(i, DISP_PT - 1), 0)),
                      pl.BlockSpec((DISP_TB, D), lambda i, zb: (0, 0))],
            out_specs=pl.BlockSpec(memory_space=pl.ANY),
            scratch_shapes=[pltpu.VMEM((MOE_BM, D), f32), pltpu.SemaphoreType.DMA(())]),
        out_shape=jax.ShapeDtypeStruct((MOE_NROWS, D), f32),
        compiler_params=_cparams(("arbitrary",)),
        name="moe_dispatch",
    )(zero_blk, dest_all.reshape(nt, 1, DISP_TB * TOP_K), h2_p, h2_s)


def _moe_ffn_kernel(e_ref, sb_ref, nb_ref, xbuf_ref, wg_ref, wu_ref, bg_ref, bu_ref, wd_ref, bd_ref,
                    ybuf_ref, x_ref, acc_ref, wgb_ref, wub_ref, wdb_ref, sem_in, sem_out):
    p = pl.program_id(0)
    j = pl.program_id(1)
    nb = nb_ref[p]
    sb = sb_ref[p]

    def in_copy(r):
        return pltpu.make_async_copy(xbuf_ref.at[pl.ds((sb + r) * MOE_BM, MOE_BM)],
                                     x_ref.at[pl.ds(r * MOE_BM, MOE_BM)], sem_in.at[r])

    def out_copy(r):
        return pltpu.make_async_copy(acc_ref.at[pl.ds(r * MOE_BM, MOE_BM)],
                                     ybuf_ref.at[pl.ds((sb + r) * MOE_BM, MOE_BM)], sem_out)

    def for_blocks(fn):
        def body(r, c):
            fn(r)
            return c
        lax.fori_loop(0, nb, body, 0)

    @pl.when((p == 0) & (j == 0))
    def _():
        used = nb_ref[MOE_PMAX]
        acc_ref[0:MOE_BM, :] = jnp.zeros((MOE_BM, D), f32)

        def tail_copy(b):
            return pltpu.make_async_copy(acc_ref.at[pl.ds(0, MOE_BM)],
                                         ybuf_ref.at[pl.ds(b * MOE_BM, MOE_BM)], sem_out)

        def tstart(b, c):
            tail_copy(b).start()
            return c

        def twait(b, c):
            tail_copy(0).wait()
            return c

        lax.fori_loop(used, MOE_NBLK, tstart, 0)
        lax.fori_loop(used, MOE_NBLK, twait, 0)

    first = j == 0
    last = j == MOE_J - 1

    @pl.when(nb > 0)
    def _():
        @pl.when(first)
        def _():
            for_blocks(lambda r: in_copy(r).start())
            bd = jnp.broadcast_to(bd_ref[...], (MOE_BM, D))

            def init(r):
                acc_ref[pl.ds(pl.multiple_of(r * MOE_BM, MOE_BM), MOE_BM), :] = bd
            for_blocks(init)

        bg = bg_ref[...]
        bu = bu_ref[...]

        def cast_weights():
            wgb_ref[...] = wg_ref[...].astype(bf16)
            wub_ref[...] = wu_ref[...].astype(bf16)
            wdb_ref[...] = wd_ref[...].astype(bf16)

        def ffn_rows(b0, n_blk):
            rows = pl.ds(pl.multiple_of(b0 * MOE_BM, MOE_BM), n_blk * MOE_BM)
            x = x_ref[rows, :].astype(bf16)
            gate = jnp.dot(x, wgb_ref[...], preferred_element_type=f32) + bg
            up = jnp.dot(x, wub_ref[...], preferred_element_type=f32) + bu
            gate = jnp.minimum(gate, SWIGLU_LIMIT)
            up = jnp.clip(up, -SWIGLU_LIMIT, SWIGLU_LIMIT)
            act = (up + 1.0) * (gate * jax.nn.sigmoid(gate * SWIGLU_ALPHA))
            acc_ref[rows, :] += jnp.dot(act.astype(bf16), wdb_ref[...], preferred_element_type=f32)

        def trip(b0, groups, with_cast=False):
            n_blk = sum(groups)

            @pl.when(first)
            def _():
                for b in range(n_blk):
                    in_copy(b0 + b).wait()

            if with_cast:
                cast_weights()
            off = 0
            for g in groups:
                ffn_rows(b0 + off, g)
                off += g

            @pl.when(last)
            def _():
                for b in range(n_blk):
                    out_copy(b0 + b).start()

        full = (3, 3)
        per_trip = sum(full)
        ntrip = sum((nb >= per_trip * m).astype(i32) for m in range(1, MOE_RB // per_trip + 1))
        rem = nb - ntrip * per_trip
        head = jnp.where(rem == 0, per_trip, rem)
        for n_head, groups in ((1, (1,)), (2, (2,)), (3, (3,)), (4, (2, 2)), (5, (3, 2)), (per_trip, full)):
            @pl.when(head == n_head)
            def _(groups=groups):
                trip(0, groups, with_cast=True)

        def full_trip(q, c):
            trip(head + q * per_trip, full)
            return c
        lax.fori_loop(0, ntrip - (rem == 0).astype(i32), full_trip, 0)

        @pl.when(last)
        def _():
            for_blocks(lambda r: out_copy(r).wait())


def _moe_ffn(e_arr, sb_arr, nb_arr, xbuf, w_gate_up, b_gate_up, w_down, b_down):
    def jj(p, j, nb):
        return jnp.where(nb[p] > 0, j, MOE_J - 1)

    in_specs = [
        pl.BlockSpec(memory_space=pl.ANY),
        pl.BlockSpec((None, D, MOE_TF), lambda p, j, e, sb, nb: (e[p], 0, jj(p, j, nb))),
        pl.BlockSpec((None, D, MOE_TF), lambda p, j, e, sb, nb: (e[p], 0, MOE_J + jj(p, j, nb))),
        pl.BlockSpec((None, 1, MOE_TF), lambda p, j, e, sb, nb: (e[p], 0, jj(p, j, nb))),
        pl.BlockSpec((None, 1, MOE_TF), lambda p, j, e, sb, nb: (e[p], 0, MOE_J + jj(p, j, nb))),
        pl.BlockSpec((None, MOE_TF, D), lambda p, j, e, sb, nb: (e[p], jj(p, j, nb), 0)),
        pl.BlockSpec((None, 1, D), lambda p, j, e, sb, nb: (e[p], 0, 0)),
    ]
    return pl.pallas_call(
        _moe_ffn_kernel,
        grid_spec=pltpu.PrefetchScalarGridSpec(
            num_scalar_prefetch=3,
            grid=(MOE_PMAX, MOE_J),
            in_specs=in_specs,
            out_specs=pl.BlockSpec(memory_space=pl.ANY),
            scratch_shapes=[pltpu.VMEM((MOE_RB * MOE_BM, D), f32),
                            pltpu.VMEM((MOE_RB * MOE_BM, D), f32),
                            pltpu.VMEM((D, MOE_TF), bf16),
                            pltpu.VMEM((D, MOE_TF), bf16),
                            pltpu.VMEM((MOE_TF, D), bf16),
                            pltpu.SemaphoreType.DMA((MOE_RB,)),
                            pltpu.SemaphoreType.DMA(())]),
        out_shape=jax.ShapeDtypeStruct((MOE_NROWS, D), f32),
        compiler_params=_cparams(("arbitrary", "arbitrary"), vmem=56 * 1024 * 1024),
        name="moe_ffn",
    )(e_arr, sb_arr, nb_arr, xbuf, w_gate_up, w_gate_up,
      b_gate_up.reshape(N_EXPERTS, 1, 2 * D_EXPERT), b_gate_up.reshape(N_EXPERTS, 1, 2 * D_EXPERT),
      w_down, b_down.reshape(N_EXPERTS, 1, D))


COMB_TB = 128


def _combine_kernel(dcur_ref, dnext_ref, ybuf_ref, gate_ref, x1_ref, g2_ref, fw_ref, o_ref, rows_ref, sem):
    i = pl.program_id(0)
    slot = i & 1

    def row_copy(buf, n, d):
        return pltpu.make_async_copy(ybuf_ref.at[pl.ds(d, 1)], rows_ref.at[buf, pl.ds(n, 1)], sem.at[buf])

    def wait_tile(buf):
        def wait_group(gi, c):
            for _ in range(SUBLANES * TOP_K):
                row_copy(buf, 0, 0).wait()
            return c
        lax.fori_loop(0, COMB_TB // SUBLANES, wait_group, 0)

    @pl.when(i == 0)
    def _():
        def start_group(gi, c):
            for tt in range(SUBLANES):
                t = pl.multiple_of(gi * SUBLANES, SUBLANES) + tt
                for k in range(TOP_K):
                    d = dcur_ref[0, 0, (gi * SUBLANES + tt) * TOP_K + k]
                    row_copy(0, k * COMB_TB + t, d).start(priority=k % 2)
            return c
        lax.fori_loop(0, COMB_TB // SUBLANES, start_group, 0)

    wait_tile(slot)

    for t in range(COMB_TB):
        for k in range(TOP_K):
            row_copy(1 - slot, k * COMB_TB + t, dnext_ref[0, 0, t * TOP_K + k]).start(priority=k % 2)

    gates = gate_ref[...]
    f = gates[:, 0:1] * rows_ref[slot, 0:COMB_TB, :]
    for k in range(1, TOP_K):
        f = f + gates[:, k:k + 1] * rows_ref[slot, k * COMB_TB:(k + 1) * COMB_TB, :]
    x2 = x1_ref[...] + g2_ref[...] * f
    o_ref[...] = x2 * lax.rsqrt(jnp.mean(x2 * x2, axis=-1, keepdims=True) + EPS) * fw_ref[...]

    @pl.when(i == pl.num_programs(0) - 1)
    def _():
        wait_tile(1 - slot)


def _combine(dest, ybuf, gates, x1, mod, mod_is_rows, per_b, final_w):
    mm = x1.shape[0]
    nt = mm // COMB_TB
    dest3 = dest.reshape(nt, 1, COMB_TB * TOP_K)
    if mod_is_rows:
        g2spec = pl.BlockSpec((COMB_TB, D), lambda i: (i, MOD_G2))
    else:
        g2spec = pl.BlockSpec((None, 1, D), lambda i: (i // per_b, 0, MOD_G2))
    return pl.pallas_call(
        _combine_kernel,
        grid=(nt,),
        in_specs=[pl.BlockSpec((1, 1, COMB_TB * TOP_K), lambda i: (i, 0, 0), memory_space=pltpu.SMEM),
                  pl.BlockSpec((1, 1, COMB_TB * TOP_K), lambda i: (jnp.minimum(i + 1, nt - 1), 0, 0),
                               memory_space=pltpu.SMEM),
                  pl.BlockSpec(memory_space=pl.ANY),
                  pl.BlockSpec((COMB_TB, TOP_K), lambda i: (i, 0)),
                  pl.BlockSpec((COMB_TB, D), lambda i: (i, 0)),
                  g2spec,
                  pl.BlockSpec((1, D), lambda i: (0, 0))],
        out_specs=pl.BlockSpec((COMB_TB, D), lambda i: (i, 0)),
        out_shape=jax.ShapeDtypeStruct((mm, D), f32),
        scratch_shapes=[pltpu.VMEM((2, TOP_K * COMB_TB, D), f32), pltpu.SemaphoreType.DMA((2,))],
        compiler_params=_cparams(("arbitrary",)),
        name="moe_combine",
    )(dest3, dest3, ybuf, gates, x1, mod, final_w.reshape(1, D))


def _routing_tables(cnt_p, cnt_s, idx_p, rank_p, idx_s, rank_s):
    cp = cnt_p[0, :N_EXPERTS].astype(i32)
    cs = cnt_s[0, :N_EXPERTS].astype(i32)
    nblk = (cp + cs + MOE_BM - 1) // MOE_BM
    blk_end = jnp.cumsum(nblk)
    blk_start = blk_end - nblk
    row_start = blk_start * MOE_BM
    experts = jnp.arange(N_EXPERTS, dtype=i32)

    def seg_row(idx, rank, base):
        flat = idx.reshape(-1)
        return jnp.sum(jnp.where(flat[:, None] == experts[None, :], base[None, :], 0), axis=1) + rank.reshape(-1)

    dest_p = seg_row(idx_p, rank_p, row_start)
    dest_s = seg_row(idx_s, rank_s, row_start + cp)
    used = blk_end[-1:]
    zero_blk = jnp.concatenate([jnp.maximum(blk_end - 1, 0), used]).astype(i32)
    npass = (nblk + MOE_RB - 1) // MOE_RB
    pass_end = jnp.cumsum(npass)
    total = pass_end[-1]
    pid = jnp.arange(MOE_PMAX, dtype=i32)
    pid_c = jnp.minimum(pid, total - 1)
    pe = jnp.minimum(jnp.sum((pass_end[None, :] <= pid_c[:, None]).astype(i32), axis=1), N_EXPERTS - 1)
    local = pid_c - (pass_end - npass)[pe]
    active = pid < total
    sb = jnp.where(active, blk_start[pe] + local * MOE_RB, 0).astype(i32)
    nb = jnp.where(active, jnp.clip(nblk[pe] - local * MOE_RB, 0, MOE_RB), 0)
    nb = jnp.concatenate([nb, used]).astype(i32)
    return dest_p.astype(i32), dest_s.astype(i32), zero_blk, pe, sb, nb


def kernel(x_prompt, x_sample, c_prompt, c_sample, state_ret, norm1_w, norm2_w, w_ada, b_ada, w_in,
           ret_gn_w, gm_ln_w, gm_ln_b, gm_ws, gm_bs, w_oa, w_ob, w_out, w_router, b_router,
           w_gate_up, b_gate_up, w_down, b_down, final_norm_w):
    x_p = x_prompt.reshape(T_P, D)
    x_s = x_sample.reshape(T_S, D)
    per_b = lambda tm: SEQ // tm

    mod = _ada(jnp.concatenate([c_prompt, c_sample], axis=0), w_ada[0], b_ada[0])
    mod_p = mod[:BATCH].reshape(BATCH, 1, 6 * D)
    mod_s = mod[BATCH:]

    n1 = norm1_w[0].reshape(1, D)
    h_all = _norm1(x_p, x_s, n1, mod_p, mod_s)
    z_all = _proj_in(h_all, w_in[0])

    woa, wob, wout = w_oa[0].astype(bf16), w_ob[0].astype(bf16), w_out[0].astype(bf16)
    m_p, s_p = _mixer_prompt(z_all, ret_gn_w[0], gm_ws[0], gm_bs[0], gm_ln_w[0], gm_ln_b[0], woa, wob)
    a_s, s_s = _ret_sample(z_all, ret_gn_w[0], state_ret[0])
    vn_s, bm_s = _gmlp_sample(z_all, gm_ws[0], gm_bs[0], gm_ln_w[0], gm_ln_b[0])
    m_s = _merge(a_s, bm_s, z_all, woa, wob, T_S, T_P // T_S)

    n2 = norm2_w[0].reshape(1, D)
    wr_hi = w_router[0].astype(bf16)
    wr_lo = (w_router[0] - wr_hi.astype(f32)).astype(bf16)
    lane_pad = ((0, 0), (0, LANES - N_EXPERTS))
    wr_cat = jnp.concatenate([jnp.pad(wr_hi, lane_pad), jnp.pad(wr_lo, lane_pad)], axis=1)
    br = jnp.pad(b_router[0].reshape(1, N_EXPERTS), lane_pad)
    x1_p, h2_p, idx_p, gate_p, rank_p, cnt_p = _post(
        m_p, x_p, mod_p, False, n2, wout, wr_cat, br, 512, per_b(512))
    x1_s, h2_s, idx_s, gate_s, rank_s, cnt_s = _post(
        m_s, x_s, mod_s, True, n2, wout, wr_cat, br, T_S, 1)

    dest_p, dest_s, zero_blk, pe, sb, nb = _routing_tables(cnt_p, cnt_s, idx_p, rank_p, idx_s, rank_s)
    xbuf = _dispatch(zero_blk, jnp.concatenate([dest_p, dest_s], axis=0), h2_p, h2_s)
    ybuf = _moe_ffn(pe, sb, nb, xbuf, w_gate_up[0], b_gate_up[0], w_down[0], b_down[0])
    y_p = _combine(dest_p, ybuf, gate_p, x1_p, mod_p, False, per_b(COMB_TB), final_norm_w)
    y_s = _combine(dest_s, ybuf, gate_s, x1_s, mod_s, True, 1, final_norm_w)

    return (y_p.reshape(BATCH, SEQ, D),
            y_s.reshape(DEC_BATCH, 1, D),
            s_p.reshape(1, BATCH, HEADS, DK, DV),
            s_s.reshape(1, DEC_BATCH, HEADS, DK, DV),
            vn_s.reshape(1, DEC_BATCH, 1, GW))
```

```python
import functools

import numpy as np
import jax
import jax.numpy as jnp
from jax import lax
from jax.experimental import pallas as pl
from jax.experimental.pallas import tpu as pltpu

f32 = jnp.float32
bf16 = jnp.bfloat16
i32 = jnp.int32

D = 2048
BATCH = 4
SEQ = 2048
DEC_BATCH = 128
PAST_LEN = 16384
HEADS = 8
DK = 128
DV = 256
CHUNK = 128
ROPE_THETA = 10000.0
GROUPS = 8
GW = 2048
GC = GW // GROUPS
N_EXPERTS = 32
TOP_K = 4
D_EXPERT = 2048
SWIGLU_LIMIT = 7.0
SWIGLU_ALPHA = 1.702
EPS = 1e-6

T_P = BATCH * SEQ
T_S = DEC_BATCH
T_ALL = T_P + T_S
D_IN = 14336
COL_Q, COL_K = 0, 1
COL_V, COL_G, COL_U, COL_GV, COL_GA, COL_GB = 1, 2, 3, 4, 5, 6
MOD_SH1, MOD_SC1, MOD_G1, MOD_SH2, MOD_SC2, MOD_G2 = range(6)

LANES = 128
SUBLANES = 8
VMEM_LIMIT = 48 * 1024 * 1024

MOE_BM = 128
MOE_RB = 12
MOE_TF = 256
MOE_J = D_EXPERT // MOE_TF
TK_ALL = T_ALL * TOP_K
MOE_NBLK = (TK_ALL + N_EXPERTS * (MOE_BM - 1)) // MOE_BM
MOE_NROWS = MOE_NBLK * MOE_BM
MOE_PMAX = N_EXPERTS + (MOE_NBLK - N_EXPERTS) // MOE_RB


def _cparams(sem, vmem=VMEM_LIMIT):
    return pltpu.CompilerParams(dimension_semantics=sem, vmem_limit_bytes=vmem)


def _ada_kernel(c_ref, w_ref, b_ref, o_ref):
    o_ref[...] = jnp.dot(c_ref[...].astype(bf16), w_ref[...].astype(bf16),
                         preferred_element_type=f32) + b_ref[...]


def _ada(c_all, w_ada, b_ada):
    m = c_all.shape[0]
    tn = 1024
    return pl.pallas_call(
        _ada_kernel,
        grid=(6 * D // tn,),
        in_specs=[pl.BlockSpec((m, D), lambda j: (0, 0)),
                  pl.BlockSpec((D, tn), lambda j: (0, j)),
                  pl.BlockSpec((1, tn), lambda j: (0, j))],
        out_specs=pl.BlockSpec((m, tn), lambda j: (0, j)),
        out_shape=jax.ShapeDtypeStruct((m, 6 * D), f32),
        compiler_params=_cparams(("arbitrary",)),
        name="ada_mod",
    )(c_all, w_ada, b_ada.reshape(1, 6 * D))


def _norm_mod_kernel(x_ref, nw_ref, sc_ref, sh_ref, o_ref):
    x = x_ref[...]
    y = x * lax.rsqrt(jnp.mean(x * x, axis=-1, keepdims=True) + EPS) * nw_ref[...]
    o_ref[...] = (y * (1.0 + sc_ref[...]) + sh_ref[...]).astype(o_ref.dtype)


def _norm1(x, nw, mod, mod_is_rows, tm, per_b):
    mm = x.shape[0]
    if mod_is_rows:
        mspec = lambda chunk: pl.BlockSpec((tm, D), lambda i: (i, chunk))
    else:
        mspec = lambda chunk: pl.BlockSpec((None, 1, D), lambda i: (i // per_b, 0, chunk))
    return pl.pallas_call(
        _norm_mod_kernel,
        grid=(mm // tm,),
        in_specs=[pl.BlockSpec((tm, D), lambda i: (i, 0)),
                  pl.BlockSpec((1, D), lambda i: (0, 0)),
                  mspec(MOD_SC1), mspec(MOD_SH1)],
        out_specs=pl.BlockSpec((tm, D), lambda i: (i, 0)),
        out_shape=jax.ShapeDtypeStruct((mm, D), bf16),
        compiler_params=_cparams(("arbitrary",)),
        name="norm1",
    )(x, nw, mod, mod)


PROJ_TM = 1024
PROJ_TN = 1024
PROJ_PT = T_P // PROJ_TM


def _proj_in_kernel(hp_ref, hs_ref, w_ref, zp_ref, zs_ref, wb_ref):
    i = pl.program_id(1)

    @pl.when(i == 0)
    def _():
        wb_ref[...] = w_ref[...].astype(bf16)

    @pl.when(i < PROJ_PT)
    def _():
        zp_ref[...] = jnp.dot(hp_ref[...], wb_ref[...], preferred_element_type=f32)

    @pl.when(i == PROJ_PT)
    def _():
        zs_ref[...] = jnp.dot(hs_ref[...], wb_ref[...], preferred_element_type=f32)


def _proj_in(h_p, h_s, w_in):
    tm, tn = PROJ_TM, PROJ_TN
    pt = lambda i: jnp.minimum(i, PROJ_PT - 1)
    return pl.pallas_call(
        _proj_in_kernel,
        grid=(D_IN // tn, PROJ_PT + 1),
        in_specs=[pl.BlockSpec((tm, D), lambda j, i: (pt(i), 0)),
                  pl.BlockSpec((T_S, D), lambda j, i: (0, 0)),
                  pl.BlockSpec((D, tn), lambda j, i: (0, j))],
        out_specs=[pl.BlockSpec((tm, tn), lambda j, i: (pt(i), j)),
                   pl.BlockSpec((T_S, tn), lambda j, i: (0, j))],
        out_shape=[jax.ShapeDtypeStruct((T_P, D_IN), f32),
                   jax.ShapeDtypeStruct((T_S, D_IN), f32)],
        scratch_shapes=[pltpu.VMEM((D, tn), bf16)],
        compiler_params=_cparams(("arbitrary", "arbitrary")),
        name="proj_in",
    )(h_p, h_s, w_in)


def _retention_consts():
    h = np.arange(HEADS, dtype=np.float64)
    lg = np.log1p(-np.exp2(-5.0 - h))
    idx = np.arange(CHUNK, dtype=np.float64)
    diff = idx[:, None] - idx[None, :]
    intra = np.where(diff >= 0, np.exp(np.maximum(diff, 0.0)[None] * lg[:, None, None]), 0.0)
    q_dec = np.exp((idx + 1.0)[:, None] * lg[None, :])
    k_dec = np.exp((CHUNK - 1.0 - idx)[:, None] * lg[None, :])
    s_dec = np.exp(CHUNK * lg)
    gamma = np.exp(lg)
    return (intra.astype(np.float32), q_dec.astype(np.float32), k_dec.astype(np.float32),
            [float(v) for v in s_dec], [float(v) for v in gamma])


def _rope_tables(pos):
    half = DK // 2
    freq = ROPE_THETA ** (-np.arange(half, dtype=np.float64) / half)
    ang = np.asarray(pos, dtype=np.float64)[:, None] * freq[None, :]
    cos, sin = np.cos(ang), np.sin(ang)
    cos_t = np.concatenate([cos, cos], axis=-1).astype(np.float32)
    sin_t = np.concatenate([-sin, sin], axis=-1).astype(np.float32)
    return cos_t, sin_t


def _group_norm_gate(o, gnw, g):
    mu = jnp.mean(o, axis=-1, keepdims=True)
    var = jnp.mean(jnp.square(o - mu), axis=-1, keepdims=True)
    on = (o - mu) * lax.rsqrt(var + EPS) * gnw
    return jax.nn.silu(g) * on


def _retention_chunk(s_dec, live, q_ref, k_ref, v_ref, g_ref, cos_ref, sin_ref, intra_ref, qd_ref, kd_ref,
                     gnw_ref, s_ref, a_ref, slot):
    cos = cos_ref[...]
    sin = sin_ref[...]
    qd = qd_ref[...]
    kd = kd_ref[...]
    for h in range(HEADS):
        qh = q_ref[:, h * DK:(h + 1) * DK]
        kh = k_ref[:, h * DK:(h + 1) * DK]
        qh = (qh * cos + pltpu.roll(qh, DK // 2, 1) * sin) * (DK ** -0.5)
        kh = kh * cos + pltpu.roll(kh, DK // 2, 1) * sin
        vh = v_ref[:, h * DV:(h + 1) * DV].astype(bf16)
        s_old = s_ref[h]
        scores = lax.dot_general(qh.astype(bf16), kh.astype(bf16), (((1,), (1,)), ((), ())),
                                 preferred_element_type=f32) * intra_ref[h]
        o = jnp.dot(scores.astype(bf16), vh, preferred_element_type=f32)
        o = o + jnp.dot((qh * qd[:, h:h + 1]).astype(bf16), s_old.astype(bf16), preferred_element_type=f32)
        kw_t = (kh * kd[:, h:h + 1]).T.astype(bf16)
        s_new = s_old * s_dec[h] + jnp.dot(kw_t, vh, preferred_element_type=f32)
        s_ref[h] = jnp.where(live, s_new, s_old)
        a_ref[slot, :, h * DV:(h + 1) * DV] = _group_norm_gate(
            o, gnw_ref[:, h * DV:(h + 1) * DV], g_ref[:, h * DV:(h + 1) * DV]).astype(a_ref.dtype)


RET_S_BT = SUBLANES


def _ret_sample_kernel(gamma, qt_ref, kt_ref, v_ref, g_ref, cos_ref, sin_ref, spread_ref, gnw_ref, s_in_ref,
                       a_ref, s_out_ref):
    cos = cos_ref[...]
    sin = sin_ref[...]

    def rope_t(x):
        rolled = jnp.concatenate([x[DK // 2:], x[:DK // 2]], axis=0)
        return x * cos + rolled * sin

    def spread(x):
        hi = x.astype(bf16).astype(f32)
        r1 = x - hi
        mid = r1.astype(bf16).astype(f32)
        lo = r1 - mid
        pieces = hi + pltpu.roll(mid, HEADS, 1) + pltpu.roll(lo, 2 * HEADS, 1)
        return jnp.dot(pieces.astype(bf16), spread_ref[...], preferred_element_type=f32)

    def head_cols(row_ref, t):
        rows = [row_ref[pl.ds(t, 1), h * DK:(h + 1) * DK] for h in range(HEADS)]
        return jnp.concatenate(rows + [jnp.zeros((LANES - HEADS, DK), f32)], axis=0).T

    for t in range(RET_S_BT):
        qb = spread(rope_t(head_cols(qt_ref, t)) * (DK ** -0.5))
        kb = spread(rope_t(head_cols(kt_ref, t)))
        row = pl.ds(t, 1)
        for h in range(HEADS):
            hs = slice(h * DV, (h + 1) * DV)
            s_new = s_in_ref[t, h] * gamma[h] + kb[:, hs] * v_ref[row, hs]
            s_out_ref[t, h] = s_new
            o = jnp.sum(qb[:, hs] * s_new, axis=0, keepdims=True)
            a_ref[row, hs] = _group_norm_gate(o, gnw_ref[:, hs], g_ref[row, hs])


def _ret_sample(z_all, gn_w, state):
    _, _, _, _, gamma = _retention_consts()
    cos_t, sin_t = _rope_tables(np.array([PAST_LEN]))
    cos8 = np.ascontiguousarray(np.broadcast_to(cos_t[0][:, None], (DK, LANES)))
    sin8 = np.ascontiguousarray(np.broadcast_to(sin_t[0][:, None], (DK, LANES)))
    spread_m = np.zeros((LANES, HEADS * DV), np.float32)
    for piece in range(3):
        for h in range(HEADS):
            spread_m[piece * HEADS + h, h * DV:(h + 1) * DV] = 1.0
    bt = RET_S_BT
    blk0 = 0
    return pl.pallas_call(
        functools.partial(_ret_sample_kernel, gamma),
        grid=(T_S // bt,),
        in_specs=[pl.BlockSpec((bt, HEADS * DK), lambda i: (blk0 + i, COL_Q)),
                  pl.BlockSpec((bt, HEADS * DK), lambda i: (blk0 + i, COL_K)),
                  pl.BlockSpec((bt, HEADS * DV), lambda i: (blk0 + i, COL_V)),
                  pl.BlockSpec((bt, HEADS * DV), lambda i: (blk0 + i, COL_G)),
                  pl.BlockSpec((DK, LANES), lambda i: (0, 0)),
                  pl.BlockSpec((DK, LANES), lambda i: (0, 0)),
                  pl.BlockSpec((LANES, HEADS * DV), lambda i: (0, 0)),
                  pl.BlockSpec((1, HEADS * DV), lambda i: (0, 0)),
                  pl.BlockSpec((bt, HEADS, DK, DV), lambda i: (i, 0, 0, 0))],
        out_specs=[pl.BlockSpec((bt, HEADS * DV), lambda i: (i, 0)),
                   pl.BlockSpec((bt, HEADS, DK, DV), lambda i: (i, 0, 0, 0))],
        out_shape=[jax.ShapeDtypeStruct((T_S, HEADS * DV), f32),
                   jax.ShapeDtypeStruct((T_S, HEADS, DK, DV), f32)],
        compiler_params=_cparams(("arbitrary",)),
        name="retention_sample",
    )(z_all, z_all, z_all, z_all, jnp.asarray(cos8), jnp.asarray(sin8), jnp.asarray(spread_m, dtype=bf16),
      gn_w.reshape(1, HEADS * DV), state)


def _layer_norm(x, w, b):
    mu = jnp.mean(x, axis=-1, keepdims=True)
    var = jnp.mean(jnp.square(x - mu), axis=-1, keepdims=True)
    return (x - mu) * lax.rsqrt(var + EPS) * w + b


def _gmlp_chunk(u_ref, gv_ref, ws_ref, bst_ref, lnw_ref, lnb_ref, o_ref, slot):
    vn = _layer_norm(jax.nn.gelu(gv_ref[...]), lnw_ref[...], lnb_ref[...]).astype(bf16)
    r = lax.broadcasted_iota(i32, (CHUNK, CHUNK), 0)
    c = lax.broadcasted_iota(i32, (CHUNK, CHUNK), 1)
    causal = r >= c
    bst = bst_ref[...]
    for g in range(GROUPS):
        w = jnp.where(causal, ws_ref[g], 0.0).astype(bf16)
        mixed = jnp.dot(w, vn[:, g * GC:(g + 1) * GC], preferred_element_type=f32) + bst[:, g:g + 1]
        o_ref[slot, :, g * GC:(g + 1) * GC] = (
            jax.nn.gelu(u_ref[:, g * GC:(g + 1) * GC]) * mixed).astype(o_ref.dtype)


def _gmlp_sample_kernel(u_ref, gv_ref, w0_ref, b0_ref, lnw_ref, lnb_ref, vn_ref, o_ref):
    vn = _layer_norm(jax.nn.gelu(gv_ref[...]), lnw_ref[...], lnb_ref[...])
    vn_ref[...] = vn
    o_ref[...] = (jax.nn.gelu(u_ref[...]) * (vn * w0_ref[...] + b0_ref[...])).astype(o_ref.dtype)


def _gmlp_sample(z_all, gm_ws, gm_bs, ln_w, ln_b):
    w0 = jnp.repeat(gm_ws[:, 0, 0], GC).reshape(1, GW)
    b0 = jnp.repeat(gm_bs[:, 0], GC).reshape(1, GW)
    blk = 0
    return pl.pallas_call(
        _gmlp_sample_kernel,
        grid=(1,),
        in_specs=[pl.BlockSpec((T_S, GW), lambda i: (blk, COL_U)),
                  pl.BlockSpec((T_S, GW), lambda i: (blk, COL_GV)),
                  pl.BlockSpec((1, GW), lambda i: (0, 0)),
                  pl.BlockSpec((1, GW), lambda i: (0, 0)),
                  pl.BlockSpec((1, GW), lambda i: (0, 0)),
                  pl.BlockSpec((1, GW), lambda i: (0, 0))],
        out_specs=[pl.BlockSpec((T_S, GW), lambda i: (0, 0)),
                   pl.BlockSpec((T_S, GW), lambda i: (0, 0))],
        out_shape=[jax.ShapeDtypeStruct((T_S, GW), f32),
                   jax.ShapeDtypeStruct((T_S, GW), bf16)],
        compiler_params=_cparams(("arbitrary",)),
        name="gmlp_sample",
    )(z_all, z_all, w0, b0, ln_w.reshape(1, GW), ln_b.reshape(1, GW))


def _merge_kernel(a_ref, b_ref, ga_ref, gb_ref, woa_ref, wob_ref, o_ref):
    ya = jnp.dot(a_ref[...].astype(bf16), woa_ref[...], preferred_element_type=f32)
    yb = jnp.dot(b_ref[...].astype(bf16), wob_ref[...], preferred_element_type=f32)
    o_ref[...] = (jax.nn.sigmoid(ga_ref[...]) * ya + jax.nn.sigmoid(gb_ref[...]) * yb).astype(o_ref.dtype)


def _merge(a, bm, z_all, w_oa, w_ob, tm, z_blk0):
    m = a.shape[0]
    resident = lambda: pl.BlockSpec((D, D), lambda i: (0, 0), pipeline_mode=pl.Buffered(1))
    return pl.pallas_call(
        _merge_kernel,
        grid=(m // tm,),
        in_specs=[pl.BlockSpec((tm, D), lambda i: (i, 0)),
                  pl.BlockSpec((tm, D), lambda i: (i, 0)),
                  pl.BlockSpec((tm, D), lambda i: (z_blk0 + i, COL_GA)),
                  pl.BlockSpec((tm, D), lambda i: (z_blk0 + i, COL_GB)),
                  resident(), resident()],
        out_specs=pl.BlockSpec((tm, D), lambda i: (i, 0)),
        out_shape=jax.ShapeDtypeStruct((m, D), bf16),
        compiler_params=_cparams(("arbitrary",)),
        name="merge",
    )(a, bm, z_all, z_all, w_oa, w_ob)


MIX_NCH = SEQ // CHUNK
MIX_NSTEP = BATCH * MIX_NCH


def _mixer_prompt_kernel(s_dec, q_ref, k_ref, v_ref, g_ref, u_ref, gv_ref, ga_ref, gb_ref,
                         cos_ref, sin_ref, intra_ref, qd_ref, kd_ref, gnw_ref,
                         ws_ref, bst_ref, lnw_ref, lnb_ref, woa_ref, wob_ref,
                         m_ref, s_ref, a_scr, b_scr):
    s = pl.program_id(0)
    live = s < MIX_NSTEP
    c = jnp.minimum(s, MIX_NSTEP - 1)
    cur = s & 1

    @pl.when(s == 0)
    def _():
        a_scr[...] = jnp.zeros_like(a_scr)
        b_scr[...] = jnp.zeros_like(b_scr)

    @pl.when((c & (MIX_NCH - 1)) == 0)
    def _():
        s_ref[...] = jnp.zeros_like(s_ref)

    ya = jnp.dot(a_scr[1 - cur], woa_ref[...], preferred_element_type=f32)
    yb = jnp.dot(b_scr[1 - cur], wob_ref[...], preferred_element_type=f32)
    m_ref[...] = (jax.nn.sigmoid(ga_ref[...]) * ya + jax.nn.sigmoid(gb_ref[...]) * yb).astype(m_ref.dtype)

    _retention_chunk(s_dec, live, q_ref, k_ref, v_ref, g_ref, cos_ref, sin_ref, intra_ref, qd_ref, kd_ref,
                     gnw_ref, s_ref, a_scr, cur)
    _gmlp_chunk(u_ref, gv_ref, ws_ref, bst_ref, lnw_ref, lnb_ref, b_scr, cur)


def _mixer_prompt(z_all, gn_w, gm_ws, gm_bs, ln_w, ln_b, w_oa, w_ob):
    intra, q_dec, k_dec, s_dec, _ = _retention_consts()
    cos_t, sin_t = _rope_tables(np.arange(SEQ))
    cur = lambda s: jnp.minimum(s, MIX_NSTEP - 1)
    lag = lambda s: jnp.maximum(s - 1, 0)
    zspec = lambda width, row, col: pl.BlockSpec((CHUNK, width), lambda s: (row(s), col))
    const = lambda shape: pl.BlockSpec(shape, lambda s: (0,) * len(shape))
    resident = lambda: pl.BlockSpec((D, D), lambda s: (0, 0), pipeline_mode=pl.Buffered(1))
    return pl.pallas_call(
        functools.partial(_mixer_prompt_kernel, s_dec),
        grid=(MIX_NSTEP + 1,),
        in_specs=[zspec(HEADS * DK, cur, COL_Q), zspec(HEADS * DK, cur, COL_K),
                  zspec(HEADS * DV, cur, COL_V), zspec(HEADS * DV, cur, COL_G),
                  zspec(GW, cur, COL_U), zspec(GW, cur, COL_GV),
                  zspec(D, lag, COL_GA), zspec(D, lag, COL_GB),
                  pl.BlockSpec((CHUNK, DK), lambda s: (cur(s) & (MIX_NCH - 1), 0)),
                  pl.BlockSpec((CHUNK, DK), lambda s: (cur(s) & (MIX_NCH - 1), 0)),
                  const((HEADS, CHUNK, CHUNK)), const((CHUNK, HEADS)), const((CHUNK, HEADS)),
                  const((1, HEADS * DV)),
                  const((GROUPS, CHUNK, CHUNK)), const((CHUNK, GROUPS)), const((1, GW)), const((1, GW)),
                  resident(), resident()],
        out_specs=[pl.BlockSpec((CHUNK, D), lambda s: (lag(s), 0)),
                   pl.BlockSpec((None, HEADS, DK, DV), lambda s: (cur(s) // MIX_NCH, 0, 0, 0))],
        out_shape=[jax.ShapeDtypeStruct((T_P, D), bf16),
                   jax.ShapeDtypeStruct((BATCH, HEADS, DK, DV), f32)],
        scratch_shapes=[pltpu.VMEM((2, CHUNK, HEADS * DV), bf16), pltpu.VMEM((2, CHUNK, GW), bf16)],
        compiler_params=_cparams(("arbitrary",)),
        name="mixer_prompt",
    )(z_all, z_all, z_all, z_all, z_all, z_all, z_all, z_all,
      jnp.asarray(cos_t), jnp.asarray(sin_t), jnp.asarray(intra), jnp.asarray(q_dec), jnp.asarray(k_dec),
      gn_w.reshape(1, HEADS * DV), gm_ws, gm_bs.T, ln_w.reshape(1, GW), ln_b.reshape(1, GW), w_oa, w_ob)


def _post_kernel(m_ref, x_ref, g1_ref, sc_ref, sh_ref, nw_ref, wout_ref, wr_ref, br_ref,
                 x1_ref, h2_ref, idx_ref, gate_ref, rank_ref, cnt_ref):
    i = pl.program_id(0)

    @pl.when(i == 0)
    def _():
        cnt_ref[...] = jnp.zeros_like(cnt_ref)

    tm = m_ref.shape[0]
    y = jnp.dot(m_ref[...], wout_ref[...], preferred_element_type=f32)
    x1 = x_ref[...] + g1_ref[...] * y
    x1_ref[...] = x1
    xn = x1 * lax.rsqrt(jnp.mean(x1 * x1, axis=-1, keepdims=True) + EPS) * nw_ref[...]
    h2 = xn * (1.0 + sc_ref[...]) + sh_ref[...]
    h2_ref[...] = h2

    h_hi = h2.astype(bf16)
    h_lo = (h2 - h_hi.astype(f32)).astype(bf16)
    p_hi = jnp.dot(h_hi, wr_ref[...], preferred_element_type=f32)
    p_lo = jnp.dot(h_lo, wr_ref[...], preferred_element_type=f32)
    logits = p_hi[:, :LANES] + p_hi[:, LANES:] + p_lo[:, :LANES] + br_ref[...]

    lane = lax.broadcasted_iota(i32, (tm, LANES), 1).astype(f32)
    col = lax.broadcasted_iota(i32, (tm, TOP_K), 1)
    work = jnp.where(lane < N_EXPERTS, logits, -jnp.inf)
    member = jnp.zeros((tm, LANES), f32)
    vals, sels = [], []
    idx_out = jnp.zeros((tm, TOP_K), i32)
    for k in range(TOP_K):
        mx = jnp.max(work, axis=1, keepdims=True)
        ix = jnp.min(jnp.where(work == mx, lane, float(LANES)), axis=1, keepdims=True)
        sel = lane == ix
        vals.append(mx)
        sels.append(sel)
        idx_out = jnp.where(col == k, ix.astype(i32), idx_out)
        member = jnp.where(sel, 1.0, member)
        work = jnp.where(sel, -jnp.inf, work)
    idx_ref[...] = idx_out

    exps = [jnp.exp(v - vals[0]) for v in vals]
    den = exps[0] + exps[1] + exps[2] + exps[3]
    gate_out = jnp.zeros((tm, TOP_K), f32)
    for k in range(TOP_K):
        gate_out = jnp.where(col == k, exps[k] / den, gate_out)
    gate_ref[...] = gate_out

    r = lax.broadcasted_iota(i32, (tm, tm), 0)
    c = lax.broadcasted_iota(i32, (tm, tm), 1)
    lower = jnp.where(c < r, 1.0, 0.0).astype(bf16)
    before = jnp.dot(lower, member.astype(bf16), preferred_element_type=f32) + cnt_ref[...]
    rank_out = jnp.zeros((tm, TOP_K), i32)
    for k in range(TOP_K):
        rk = jnp.sum(jnp.where(sels[k], before, 0.0), axis=1, keepdims=True)
        rank_out = jnp.where(col == k, rk.astype(i32), rank_out)
    rank_ref[...] = rank_out
    cnt_ref[...] += jnp.sum(member, axis=0, keepdims=True)


def _post(m, x, mod, mod_is_rows, nw, w_out, w_router, b_router, tm, per_b):
    mm = m.shape[0]
    if mod_is_rows:
        mspec = lambda chunk: pl.BlockSpec((tm, D), lambda i: (i, chunk))
    else:
        mspec = lambda chunk: pl.BlockSpec((None, 1, D), lambda i: (i // per_b, 0, chunk))
    in_specs = [pl.BlockSpec((tm, D), lambda i: (i, 0)),
                pl.BlockSpec((tm, D), lambda i: (i, 0)),
                mspec(MOD_G1), mspec(MOD_SC2), mspec(MOD_SH2),
                pl.BlockSpec((1, D), lambda i: (0, 0)),
                pl.BlockSpec((D, D), lambda i: (0, 0), pipeline_mode=pl.Buffered(1)),
                pl.BlockSpec((D, 2 * LANES), lambda i: (0, 0)),
                pl.BlockSpec((1, LANES), lambda i: (0, 0))]
    args = [m, x, mod, mod, mod, nw, w_out, w_router, b_router]
    return pl.pallas_call(
        _post_kernel,
        grid=(mm // tm,),
        in_specs=in_specs,
        out_specs=[pl.BlockSpec((tm, D), lambda i: (i, 0)),
                   pl.BlockSpec((tm, D), lambda i: (i, 0)),
                   pl.BlockSpec((tm, TOP_K), lambda i: (i, 0)),
                   pl.BlockSpec((tm, TOP_K), lambda i: (i, 0)),
                   pl.BlockSpec((tm, TOP_K), lambda i: (i, 0)),
                   pl.BlockSpec((1, LANES), lambda i: (0, 0))],
        out_shape=[jax.ShapeDtypeStruct((mm, D), f32),
                   jax.ShapeDtypeStruct((mm, D), f32),
                   jax.ShapeDtypeStruct((mm, TOP_K), i32),
                   jax.ShapeDtypeStruct((mm, TOP_K), f32),
                   jax.ShapeDtypeStruct((mm, TOP_K), i32),
                   jax.ShapeDtypeStruct((1, LANES), f32)],
        compiler_params=_cparams(("arbitrary",)),
        name="post_mixer",
    )(*args)


DISP_TB = 128


DISP_PT = T_P // DISP_TB


def _dispatch_kernel(zblk_ref, dest_ref, h2p_ref, h2s_ref, xbuf_ref, zero_ref, sem):
    i = pl.program_id(0)

    def row_copy(src_ref, t, d):
        return pltpu.make_async_copy(src_ref.at[pl.ds(t, 1)], xbuf_ref.at[pl.ds(d, 1)], sem)

    def zero_copy(blk):
        return pltpu.make_async_copy(zero_ref, xbuf_ref.at[pl.ds(blk * MOE_BM, MOE_BM)], sem)

    @pl.when(i == 0)
    def _():
        zero_ref[...] = jnp.zeros_like(zero_ref)
        used = zblk_ref[N_EXPERTS]

        def zstart(e, c):
            zero_copy(zblk_ref[e]).start()
            return c

        def tstart(b, c):
            zero_copy(b).start()
            return c

        def zwait(e, c):
            zero_copy(0).wait()
            return c

        lax.fori_loop(0, N_EXPERTS, zstart, 0)
        lax.fori_loop(used, MOE_NBLK, tstart, 0)
        lax.fori_loop(0, N_EXPERTS, zwait, 0)
        lax.fori_loop(used, MOE_NBLK, zwait, 0)

    def issue(src_ref):
        for t in range(DISP_TB):
            for k in range(TOP_K):
                row_copy(src_ref, t, dest_ref[0, 0, t * TOP_K + k]).start(priority=k % 2)

    @pl.when(i < DISP_PT)
    def _():
        issue(h2p_ref)

    @pl.when(i == DISP_PT)
    def _():
        issue(h2s_ref)

    def wait_group(gi, c):
        for _ in range(SUBLANES * TOP_K):
            row_copy(h2p_ref, 0, 0).wait()
        return c

    lax.fori_loop(0, DISP_TB // SUBLANES, wait_group, 0)


def _dispatch(zero_blk, dest_all, h2_p, h2_s):
    nt = T_ALL // DISP_TB
    return pl.pallas_call(
        _dispatch_kernel,
        grid_spec=pltpu.PrefetchScalarGridSpec(
            num_scalar_prefetch=1,
            grid=(nt,),
            in_specs=[pl.BlockSpec((1, 1, DISP_TB * TOP_K), lambda i, zb: (i, 0, 0), memory_space=pltpu.SMEM),
                      pl.BlockSpec((DISP_TB, D), lambda i, zb: (jnp.minimum(i, DISP_PT - 1), 0)),
                      pl.BlockSpec((DISP_TB, D), lambda i, zb: (0, 0))],
            out_specs=pl.BlockSpec(memory_space=pl.ANY),
            scratch_shapes=[pltpu.VMEM((MOE_BM, D), f32), pltpu.SemaphoreType.DMA(())]),
        out_shape=jax.ShapeDtypeStruct((MOE_NROWS, D), f32),
        compiler_params=_cparams(("arbitrary",)),
        name="moe_dispatch",
    )(zero_blk, dest_all.reshape(nt, 1, DISP_TB * TOP_K), h2_p, h2_s)


def _moe_ffn_kernel(e_ref, sb_ref, nb_ref, xbuf_ref, wg_ref, wu_ref, bg_ref, bu_ref, wd_ref, bd_ref,
                    ybuf_ref, x_ref, acc_ref, wgb_ref, wub_ref, wdb_ref, sem_in, sem_out):
    p = pl.program_id(0)
    j = pl.program_id(1)
    nb = nb_ref[p]
    sb = sb_ref[p]

    def in_copy(r):
        return pltpu.make_async_copy(xbuf_ref.at[pl.ds((sb + r) * MOE_BM, MOE_BM)],
                                     x_ref.at[pl.ds(r * MOE_BM, MOE_BM)], sem_in.at[r])

    def out_copy(r):
        return pltpu.make_async_copy(acc_ref.at[pl.ds(r * MOE_BM, MOE_BM)],
                                     ybuf_ref.at[pl.ds((sb + r) * MOE_BM, MOE_BM)], sem_out)

    def for_blocks(fn):
        def body(r, c):
            fn(r)
            return c
        lax.fori_loop(0, nb, body, 0)

    @pl.when((p == 0) & (j == 0))
    def _():
        used = nb_ref[MOE_PMAX]
        acc_ref[0:MOE_BM, :] = jnp.zeros((MOE_BM, D), f32)

        def tail_copy(b):
            return pltpu.make_async_copy(acc_ref.at[pl.ds(0, MOE_BM)],
                                         ybuf_ref.at[pl.ds(b * MOE_BM, MOE_BM)], sem_out)

        def tstart(b, c):
            tail_copy(b).start()
            return c

        def twait(b, c):
            tail_copy(0).wait()
            return c

        lax.fori_loop(used, MOE_NBLK, tstart, 0)
        lax.fori_loop(used, MOE_NBLK, twait, 0)

    first = j == 0
    last = j == MOE_J - 1

    @pl.when(nb > 0)
    def _():
        @pl.when(first)
        def _():
            for_blocks(lambda r: in_copy(r).start())
            bd = jnp.broadcast_to(bd_ref[...], (MOE_BM, D))

            def init(r):
                acc_ref[pl.ds(pl.multiple_of(r * MOE_BM, MOE_BM), MOE_BM), :] = bd
            for_blocks(init)

        bg = bg_ref[...]
        bu = bu_ref[...]

        def cast_weights():
            wgb_ref[...] = wg_ref[...].astype(bf16)
            wub_ref[...] = wu_ref[...].astype(bf16)
            wdb_ref[...] = wd_ref[...].astype(bf16)

        def ffn_rows(b0, n_blk):
            rows = pl.ds(pl.multiple_of(b0 * MOE_BM, MOE_BM), n_blk * MOE_BM)
            x = x_ref[rows, :].astype(bf16)
            gate = jnp.dot(x, wgb_ref[...], preferred_element_type=f32) + bg
            up = jnp.dot(x, wub_ref[...], preferred_element_type=f32) + bu
            gate = jnp.minimum(gate, SWIGLU_LIMIT)
            up = jnp.clip(up, -SWIGLU_LIMIT, SWIGLU_LIMIT)
            act = (up + 1.0) * (gate * jax.nn.sigmoid(gate * SWIGLU_ALPHA))
            acc_ref[rows, :] += jnp.dot(act.astype(bf16), wdb_ref[...], preferred_element_type=f32)

        def trip(b0, groups, with_cast=False):
            n_blk = sum(groups)

            @pl.when(first)
            def _():
                for b in range(n_blk):
                    in_copy(b0 + b).wait()

            if with_cast:
                cast_weights()
            off = 0
            for g in groups:
                ffn_rows(b0 + off, g)
                off += g

            @pl.when(last)
            def _():
                for b in range(n_blk):
                    out_copy(b0 + b).start()

        full = (3, 3)
        per_trip = sum(full)
        ntrip = sum((nb >= per_trip * m).astype(i32) for m in range(1, MOE_RB // per_trip + 1))
        rem = nb - ntrip * per_trip
        head = jnp.where(rem == 0, per_trip, rem)
        for n_head, groups in ((1, (1,)), (2, (2,)), (3, (3,)), (4, (2, 2)), (5, (3, 2)), (per_trip, full)):
            @pl.when(head == n_head)
            def _(groups=groups):
                trip(0, groups, with_cast=True)

        def full_trip(q, c):
            trip(head + q * per_trip, full)
            return c
        lax.fori_loop(0, ntrip - (rem == 0).astype(i32), full_trip, 0)

        @pl.when(last)
        def _():
            for_blocks(lambda r: out_copy(r).wait())


def _moe_ffn(e_arr, sb_arr, nb_arr, xbuf, w_gate_up, b_gate_up, w_down, b_down):
    def jj(p, j, nb):
        return jnp.where(nb[p] > 0, j, MOE_J - 1)

    in_specs = [
        pl.BlockSpec(memory_space=pl.ANY),
        pl.BlockSpec((None, D, MOE_TF), lambda p, j, e, sb, nb: (e[p], 0, jj(p, j, nb))),
        pl.BlockSpec((None, D, MOE_TF), lambda p, j, e, sb, nb: (e[p], 0, MOE_J + jj(p, j, nb))),
        pl.BlockSpec((None, 1, MOE_TF), lambda p, j, e, sb, nb: (e[p], 0, jj(p, j, nb))),
        pl.BlockSpec((None, 1, MOE_TF), lambda p, j, e, sb, nb: (e[p], 0, MOE_J + jj(p, j, nb))),
        pl.BlockSpec((None, MOE_TF, D), lambda p, j, e, sb, nb: (e[p], jj(p, j, nb), 0)),
        pl.BlockSpec((None, 1, D), lambda p, j, e, sb, nb: (e[p], 0, 0)),
    ]
    return pl.pallas_call(
        _moe_ffn_kernel,
        grid_spec=pltpu.PrefetchScalarGridSpec(
            num_scalar_prefetch=3,
            grid=(MOE_PMAX, MOE_J),
            in_specs=in_specs,
            out_specs=pl.BlockSpec(memory_space=pl.ANY),
            scratch_shapes=[pltpu.VMEM((MOE_RB * MOE_BM, D), f32),
                            pltpu.VMEM((MOE_RB * MOE_BM, D), f32),
                            pltpu.VMEM((D, MOE_TF), bf16),
                            pltpu.VMEM((D, MOE_TF), bf16),
                            pltpu.VMEM((MOE_TF, D), bf16),
                            pltpu.SemaphoreType.DMA((MOE_RB,)),
                            pltpu.SemaphoreType.DMA(())]),
        out_shape=jax.ShapeDtypeStruct((MOE_NROWS, D), f32),
        compiler_params=_cparams(("arbitrary", "arbitrary"), vmem=56 * 1024 * 1024),
        name="moe_ffn",
    )(e_arr, sb_arr, nb_arr, xbuf, w_gate_up, w_gate_up,
      b_gate_up.reshape(N_EXPERTS, 1, 2 * D_EXPERT), b_gate_up.reshape(N_EXPERTS, 1, 2 * D_EXPERT),
      w_down, b_down.reshape(N_EXPERTS, 1, D))


COMB_TB = 128


def _combine_kernel(dcur_ref, dnext_ref, ybuf_ref, gate_ref, x1_ref, g2_ref, fw_ref, o_ref, rows_ref, sem):
    i = pl.program_id(0)
    slot = i & 1

    def row_copy(buf, n, d):
        return pltpu.make_async_copy(ybuf_ref.at[pl.ds(d, 1)], rows_ref.at[buf, pl.ds(n, 1)], sem.at[buf])

    def wait_tile(buf):
        def wait_group(gi, c):
            for _ in range(SUBLANES * TOP_K):
                row_copy(buf, 0, 0).wait()
            return c
        lax.fori_loop(0, COMB_TB // SUBLANES, wait_group, 0)

    @pl.when(i == 0)
    def _():
        def start_group(gi, c):
            for tt in range(SUBLANES):
                t = pl.multiple_of(gi * SUBLANES, SUBLANES) + tt
                for k in range(TOP_K):
                    d = dcur_ref[0, 0, (gi * SUBLANES + tt) * TOP_K + k]
                    row_copy(0, k * COMB_TB + t, d).start(priority=k % 2)
            return c
        lax.fori_loop(0, COMB_TB // SUBLANES, start_group, 0)

    wait_tile(slot)

    def step(buf):
        for t in range(COMB_TB):
            for k in range(TOP_K):
                row_copy(1 - buf, k * COMB_TB + t, dnext_ref[0, 0, t * TOP_K + k]).start(priority=k % 2)

        gates = gate_ref[...]
        f = gates[:, 0:1] * rows_ref[buf, 0:COMB_TB, :]
        for k in range(1, TOP_K):
            f = f + gates[:, k:k + 1] * rows_ref[buf, k * COMB_TB:(k + 1) * COMB_TB, :]
        x2 = x1_ref[...] + g2_ref[...] * f
        o_ref[...] = x2 * lax.rsqrt(jnp.mean(x2 * x2, axis=-1, keepdims=True) + EPS) * fw_ref[...]

    for buf in range(2):
        @pl.when(slot == buf)
        def _(buf=buf):
            step(buf)

    @pl.when(i == pl.num_programs(0) - 1)
    def _():
        wait_tile(1 - slot)


def _combine(dest, ybuf, gates, x1, mod, mod_is_rows, per_b, final_w):
    mm = x1.shape[0]
    nt = mm // COMB_TB
    dest3 = dest.reshape(nt, 1, COMB_TB * TOP_K)
    if mod_is_rows:
        g2spec = pl.BlockSpec((COMB_TB, D), lambda i: (i, MOD_G2))
    else:
        g2spec = pl.BlockSpec((None, 1, D), lambda i: (i // per_b, 0, MOD_G2))
    return pl.pallas_call(
        _combine_kernel,
        grid=(nt,),
        in_specs=[pl.BlockSpec((1, 1, COMB_TB * TOP_K), lambda i: (i, 0, 0), memory_space=pltpu.SMEM),
                  pl.BlockSpec((1, 1, COMB_TB * TOP_K), lambda i: (jnp.minimum(i + 1, nt - 1), 0, 0),
                               memory_space=pltpu.SMEM),
                  pl.BlockSpec(memory_space=pl.ANY),
                  pl.BlockSpec((COMB_TB, TOP_K), lambda i: (i, 0)),
                  pl.BlockSpec((COMB_TB, D), lambda i: (i, 0)),
                  g2spec,
                  pl.BlockSpec((1, D), lambda i: (0, 0))],
        out_specs=pl.BlockSpec((COMB_TB, D), lambda i: (i, 0)),
        out_shape=jax.ShapeDtypeStruct((mm, D), f32),
        scratch_shapes=[pltpu.VMEM((2, TOP_K * COMB_TB, D), f32), pltpu.SemaphoreType.DMA((2,))],
        compiler_params=_cparams(("arbitrary",)),
        name="moe_combine",
    )(dest3, dest3, ybuf, gates, x1, mod, final_w.reshape(1, D))


def _routing_tables(cnt_p, cnt_s, idx_p, rank_p, idx_s, rank_s):
    cp = cnt_p[0, :N_EXPERTS].astype(i32)
    cs = cnt_s[0, :N_EXPERTS].astype(i32)
    nblk = (cp + cs + MOE_BM - 1) // MOE_BM
    blk_end = jnp.cumsum(nblk)
    blk_start = blk_end - nblk
    row_start = blk_start * MOE_BM
    experts = jnp.arange(N_EXPERTS, dtype=i32)

    def seg_row(idx, rank, base):
        flat = idx.reshape(-1)
        return jnp.sum(jnp.where(flat[:, None] == experts[None, :], base[None, :], 0), axis=1) + rank.reshape(-1)

    dest_p = seg_row(idx_p, rank_p, row_start)
    dest_s = seg_row(idx_s, rank_s, row_start + cp)
    used = blk_end[-1:]
    zero_blk = jnp.concatenate([jnp.maximum(blk_end - 1, 0), used]).astype(i32)
    npass = (nblk + MOE_RB - 1) // MOE_RB
    pass_end = jnp.cumsum(npass)
    total = pass_end[-1]
    pid = jnp.arange(MOE_PMAX, dtype=i32)
    pid_c = jnp.minimum(pid, total - 1)
    pe = jnp.minimum(jnp.sum((pass_end[None, :] <= pid_c[:, None]).astype(i32), axis=1), N_EXPERTS - 1)
    local = pid_c - (pass_end - npass)[pe]
    active = pid < total
    sb = jnp.where(active, blk_start[pe] + local * MOE_RB, 0).astype(i32)
    nb = jnp.where(active, jnp.clip(nblk[pe] - local * MOE_RB, 0, MOE_RB), 0)
    nb = jnp.concatenate([nb, used]).astype(i32)
    return dest_p.astype(i32), dest_s.astype(i32), zero_blk, pe, sb, nb


def kernel(x_prompt, x_sample, c_prompt, c_sample, state_ret, norm1_w, norm2_w, w_ada, b_ada, w_in,
           ret_gn_w, gm_ln_w, gm_ln_b, gm_ws, gm_bs, w_oa, w_ob, w_out, w_router, b_router,
           w_gate_up, b_gate_up, w_down, b_down, final_norm_w):
    x_p = x_prompt.reshape(T_P, D)
    x_s = x_sample.reshape(T_S, D)
    per_b = lambda tm: SEQ // tm

    mod = _ada(jnp.concatenate([c_prompt, c_sample], axis=0), w_ada[0], b_ada[0])
    mod_p = mod[:BATCH].reshape(BATCH, 1, 6 * D)
    mod_s = mod[BATCH:]

    n1 = norm1_w[0].reshape(1, D)
    h_p = _norm1(x_p, n1, mod_p, False, 512, per_b(512))
    h_s = _norm1(x_s, n1, mod_s, True, T_S, 1)
    z_p, z_s = _proj_in(h_p, h_s, w_in[0])

    woa, wob, wout = w_oa[0].astype(bf16), w_ob[0].astype(bf16), w_out[0].astype(bf16)
    m_p, s_p = _mixer_prompt(z_p, ret_gn_w[0], gm_ws[0], gm_bs[0], gm_ln_w[0], gm_ln_b[0], woa, wob)
    a_s, s_s = _ret_sample(z_s, ret_gn_w[0], state_ret[0])
    vn_s, bm_s = _gmlp_sample(z_s, gm_ws[0], gm_bs[0], gm_ln_w[0], gm_ln_b[0])
    m_s = _merge(a_s, bm_s, z_s, woa, wob, T_S, 0)

    n2 = norm2_w[0].reshape(1, D)
    wr_hi = w_router[0].astype(bf16)
    wr_lo = (w_router[0] - wr_hi.astype(f32)).astype(bf16)
    lane_pad = ((0, 0), (0, LANES - N_EXPERTS))
    wr_cat = jnp.concatenate([jnp.pad(wr_hi, lane_pad), jnp.pad(wr_lo, lane_pad)], axis=1)
    br = jnp.pad(b_router[0].reshape(1, N_EXPERTS), lane_pad)
    x1_p, h2_p, idx_p, gate_p, rank_p, cnt_p = _post(
        m_p, x_p, mod_p, False, n2, wout, wr_cat, br, 512, per_b(512))
    x1_s, h2_s, idx_s, gate_s, rank_s, cnt_s = _post(
        m_s, x_s, mod_s, True, n2, wout, wr_cat, br, T_S, 1)

    dest_p, dest_s, zero_blk, pe, sb, nb = _routing_tables(cnt_p, cnt_s, idx_p, rank_p, idx_s, rank_s)
    xbuf = _dispatch(zero_blk, jnp.concatenate([dest_p, dest_s], axis=0), h2_p, h2_s)
    ybuf = _moe_ffn(pe, sb, nb, xbuf, w_gate_up[0], b_gate_up[0], w_down[0], b_down[0])
    y_p = _combine(dest_p, ybuf, gate_p, x1_p, mod_p, False, per_b(COMB_TB), final_norm_w)
    y_s = _combine(dest_s, ybuf, gate_s, x1_s, mod_s, True, 1, final_norm_w)

    return (y_p.reshape(BATCH, SEQ, D),
            y_s.reshape(DEC_BATCH, 1, D),
            s_p.reshape(1, BATCH, HEADS, DK, DV),
            s_s.reshape(1, DEC_BATCH, HEADS, DK, DV),
            vn_s.reshape(1, DEC_BATCH, 1, GW))
```

```python
import functools

import numpy as np
import jax
import jax.numpy as jnp
from jax import lax
from jax.experimental import pallas as pl
from jax.experimental.pallas import tpu as pltpu

f32 = jnp.float32
bf16 = jnp.bfloat16
i32 = jnp.int32

D = 2048
BATCH = 4
SEQ = 2048
DEC_BATCH = 128
PAST_LEN = 16384
HEADS = 8
DK = 128
DV = 256
CHUNK = 128
ROPE_THETA = 10000.0
GROUPS = 8
GW = 2048
GC = GW // GROUPS
N_EXPERTS = 32
TOP_K = 4
D_EXPERT = 2048
SWIGLU_LIMIT = 7.0
SWIGLU_ALPHA = 1.702
EPS = 1e-6

T_P = BATCH * SEQ
T_S = DEC_BATCH
T_ALL = T_P + T_S
D_IN = 14336
COL_Q, COL_K = 0, 1
COL_V, COL_G, COL_U, COL_GV, COL_GA, COL_GB = 1, 2, 3, 4, 5, 6
MOD_SH1, MOD_SC1, MOD_G1, MOD_SH2, MOD_SC2, MOD_G2 = range(6)

LANES = 128
SUBLANES = 8
VMEM_LIMIT = 48 * 1024 * 1024

MOE_BM = 128
MOE_RB = 12
MOE_TF = 256
MOE_J = D_EXPERT // MOE_TF
TK_ALL = T_ALL * TOP_K
MOE_NBLK = (TK_ALL + N_EXPERTS * (MOE_BM - 1)) // MOE_BM
MOE_NROWS = MOE_NBLK * MOE_BM
MOE_PMAX = N_EXPERTS + (MOE_NBLK - N_EXPERTS) // MOE_RB


def _cparams(sem, vmem=VMEM_LIMIT):
    return pltpu.CompilerParams(dimension_semantics=sem, vmem_limit_bytes=vmem)


def _ada_kernel(c_ref, w_ref, b_ref, o_ref):
    o_ref[...] = jnp.dot(c_ref[...].astype(bf16), w_ref[...].astype(bf16),
                         preferred_element_type=f32) + b_ref[...]


def _ada(c_all, w_ada, b_ada):
    m = c_all.shape[0]
    tn = 1024
    return pl.pallas_call(
        _ada_kernel,
        grid=(6 * D // tn,),
        in_specs=[pl.BlockSpec((m, D), lambda j: (0, 0)),
                  pl.BlockSpec((D, tn), lambda j: (0, j)),
                  pl.BlockSpec((1, tn), lambda j: (0, j))],
        out_specs=pl.BlockSpec((m, tn), lambda j: (0, j)),
        out_shape=jax.ShapeDtypeStruct((m, 6 * D), f32),
        compiler_params=_cparams(("arbitrary",)),
        name="ada_mod",
    )(c_all, w_ada, b_ada.reshape(1, 6 * D))


def _norm_mod_kernel(x_ref, nw_ref, sc_ref, sh_ref, o_ref):
    x = x_ref[...]
    y = x * lax.rsqrt(jnp.mean(x * x, axis=-1, keepdims=True) + EPS) * nw_ref[...]
    o_ref[...] = (y * (1.0 + sc_ref[...]) + sh_ref[...]).astype(o_ref.dtype)


def _norm1(x, nw, mod, mod_is_rows, tm, per_b):
    mm = x.shape[0]
    if mod_is_rows:
        mspec = lambda chunk: pl.BlockSpec((tm, D), lambda i: (i, chunk))
    else:
        mspec = lambda chunk: pl.BlockSpec((None, 1, D), lambda i: (i // per_b, 0, chunk))
    return pl.pallas_call(
        _norm_mod_kernel,
        grid=(mm // tm,),
        in_specs=[pl.BlockSpec((tm, D), lambda i: (i, 0)),
                  pl.BlockSpec((1, D), lambda i: (0, 0)),
                  mspec(MOD_SC1), mspec(MOD_SH1)],
        out_specs=pl.BlockSpec((tm, D), lambda i: (i, 0)),
        out_shape=jax.ShapeDtypeStruct((mm, D), bf16),
        compiler_params=_cparams(("arbitrary",)),
        name="norm1",
    )(x, nw, mod, mod)


PROJ_TM = 1024
PROJ_TN = 1024
PROJ_PT = T_P // PROJ_TM


def _proj_in_kernel(hp_ref, hs_ref, w_ref, zp_ref, zs_ref, wb_ref):
    i = pl.program_id(1)

    @pl.when(i == 0)
    def _():
        wb_ref[...] = w_ref[...].astype(bf16)
        zs_ref[...] = jnp.dot(hs_ref[...], wb_ref[...], preferred_element_type=f32)

    @pl.when(i > 0)
    def _():
        zp_ref[...] = jnp.dot(hp_ref[...], wb_ref[...], preferred_element_type=f32)


def _proj_in(h_p, h_s, w_in):
    tm, tn = PROJ_TM, PROJ_TN
    pt = lambda i: jnp.maximum(i - 1, 0)
    return pl.pallas_call(
        _proj_in_kernel,
        grid=(D_IN // tn, PROJ_PT + 1),
        in_specs=[pl.BlockSpec((tm, D), lambda j, i: (pt(i), 0)),
                  pl.BlockSpec((T_S, D), lambda j, i: (0, 0)),
                  pl.BlockSpec((D, tn), lambda j, i: (0, j))],
        out_specs=[pl.BlockSpec((tm, tn), lambda j, i: (pt(i), j)),
                   pl.BlockSpec((T_S, tn), lambda j, i: (0, j))],
        out_shape=[jax.ShapeDtypeStruct((T_P, D_IN), f32),
                   jax.ShapeDtypeStruct((T_S, D_IN), f32)],
        scratch_shapes=[pltpu.VMEM((D, tn), bf16)],
        compiler_params=_cparams(("arbitrary", "arbitrary")),
        name="proj_in",
    )(h_p, h_s, w_in)


def _retention_consts():
    h = np.arange(HEADS, dtype=np.float64)
    lg = np.log1p(-np.exp2(-5.0 - h))
    idx = np.arange(CHUNK, dtype=np.float64)
    diff = idx[:, None] - idx[None, :]
    intra = np.where(diff >= 0, np.exp(np.maximum(diff, 0.0)[None] * lg[:, None, None]), 0.0)
    q_dec = np.exp((idx + 1.0)[:, None] * lg[None, :])
    k_dec = np.exp((CHUNK - 1.0 - idx)[:, None] * lg[None, :])
    s_dec = np.exp(CHUNK * lg)
    gamma = np.exp(lg)
    return (intra.astype(np.float32), q_dec.astype(np.float32), k_dec.astype(np.float32),
            [float(v) for v in s_dec], [float(v) for v in gamma])


def _rope_tables(pos):
    half = DK // 2
    freq = ROPE_THETA ** (-np.arange(half, dtype=np.float64) / half)
    ang = np.asarray(pos, dtype=np.float64)[:, None] * freq[None, :]
    cos, sin = np.cos(ang), np.sin(ang)
    cos_t = np.concatenate([cos, cos], axis=-1).astype(np.float32)
    sin_t = np.concatenate([-sin, sin], axis=-1).astype(np.float32)
    return cos_t, sin_t


def _group_norm_gate(o, gnw, g):
    mu = jnp.mean(o, axis=-1, keepdims=True)
    var = jnp.mean(jnp.square(o - mu), axis=-1, keepdims=True)
    on = (o - mu) * lax.rsqrt(var + EPS) * gnw
    return jax.nn.silu(g) * on


def _retention_chunk(s_dec, live, q_ref, k_ref, v_ref, g_ref, cos_ref, sin_ref, intra_ref, qd_ref, kd_ref,
                     gnw_ref, s_ref, a_ref, slot):
    cos = cos_ref[...]
    sin = sin_ref[...]
    qd = qd_ref[...]
    kd = kd_ref[...]
    for h in range(HEADS):
        qh = q_ref[:, h * DK:(h + 1) * DK]
        kh = k_ref[:, h * DK:(h + 1) * DK]
        qh = (qh * cos + pltpu.roll(qh, DK // 2, 1) * sin) * (DK ** -0.5)
        kh = kh * cos + pltpu.roll(kh, DK // 2, 1) * sin
        vh = v_ref[:, h * DV:(h + 1) * DV].astype(bf16)
        s_old = s_ref[h]
        scores = lax.dot_general(qh.astype(bf16), kh.astype(bf16), (((1,), (1,)), ((), ())),
                                 preferred_element_type=f32) * intra_ref[h]
        o = jnp.dot(scores.astype(bf16), vh, preferred_element_type=f32)
        o = o + jnp.dot((qh * qd[:, h:h + 1]).astype(bf16), s_old.astype(bf16), preferred_element_type=f32)
        kw_t = (kh * kd[:, h:h + 1]).T.astype(bf16)
        s_new = s_old * s_dec[h] + jnp.dot(kw_t, vh, preferred_element_type=f32)
        s_ref[h] = jnp.where(live, s_new, s_old)
        a_ref[slot, :, h * DV:(h + 1) * DV] = _group_norm_gate(
            o, gnw_ref[:, h * DV:(h + 1) * DV], g_ref[:, h * DV:(h + 1) * DV]).astype(a_ref.dtype)


RET_S_BT = SUBLANES


def _ret_sample_kernel(gamma, qt_ref, kt_ref, v_ref, g_ref, cos_ref, sin_ref, spread_ref, gnw_ref, s_in_ref,
                       a_ref, s_out_ref):
    cos = cos_ref[...]
    sin = sin_ref[...]

    def rope_t(x):
        rolled = jnp.concatenate([x[DK // 2:], x[:DK // 2]], axis=0)
        return x * cos + rolled * sin

    def spread(x):
        hi = x.astype(bf16).astype(f32)
        r1 = x - hi
        mid = r1.astype(bf16).astype(f32)
        lo = r1 - mid
        pieces = hi + pltpu.roll(mid, HEADS, 1) + pltpu.roll(lo, 2 * HEADS, 1)
        return jnp.dot(pieces.astype(bf16), spread_ref[...], preferred_element_type=f32)

    def head_cols(row_ref, t):
        rows = [row_ref[pl.ds(t, 1), h * DK:(h + 1) * DK] for h in range(HEADS)]
        return jnp.concatenate(rows + [jnp.zeros((LANES - HEADS, DK), f32)], axis=0).T

    for t in range(RET_S_BT):
        qb = spread(rope_t(head_cols(qt_ref, t)) * (DK ** -0.5))
        kb = spread(rope_t(head_cols(kt_ref, t)))
        row = pl.ds(t, 1)
        for h in range(HEADS):
            hs = slice(h * DV, (h + 1) * DV)
            s_new = s_in_ref[t, h] * gamma[h] + kb[:, hs] * v_ref[row, hs]
            s_out_ref[t, h] = s_new
            o = jnp.sum(qb[:, hs] * s_new, axis=0, keepdims=True)
            a_ref[row, hs] = _group_norm_gate(o, gnw_ref[:, hs], g_ref[row, hs])


def _ret_sample(z_all, gn_w, state):
    _, _, _, _, gamma = _retention_consts()
    cos_t, sin_t = _rope_tables(np.array([PAST_LEN]))
    cos8 = np.ascontiguousarray(np.broadcast_to(cos_t[0][:, None], (DK, LANES)))
    sin8 = np.ascontiguousarray(np.broadcast_to(sin_t[0][:, None], (DK, LANES)))
    spread_m = np.zeros((LANES, HEADS * DV), np.float32)
    for piece in range(3):
        for h in range(HEADS):
            spread_m[piece * HEADS + h, h * DV:(h + 1) * DV] = 1.0
    bt = RET_S_BT
    blk0 = 0
    return pl.pallas_call(
        functools.partial(_ret_sample_kernel, gamma),
        grid=(T_S // bt,),
        in_specs=[pl.BlockSpec((bt, HEADS * DK), lambda i: (blk0 + i, COL_Q)),
                  pl.BlockSpec((bt, HEADS * DK), lambda i: (blk0 + i, COL_K)),
                  pl.BlockSpec((bt, HEADS * DV), lambda i: (blk0 + i, COL_V)),
                  pl.BlockSpec((bt, HEADS * DV), lambda i: (blk0 + i, COL_G)),
                  pl.BlockSpec((DK, LANES), lambda i: (0, 0)),
                  pl.BlockSpec((DK, LANES), lambda i: (0, 0)),
                  pl.BlockSpec((LANES, HEADS * DV), lambda i: (0, 0)),
                  pl.BlockSpec((1, HEADS * DV), lambda i: (0, 0)),
                  pl.BlockSpec((bt, HEADS, DK, DV), lambda i: (i, 0, 0, 0))],
        out_specs=[pl.BlockSpec((bt, HEADS * DV), lambda i: (i, 0)),
                   pl.BlockSpec((bt, HEADS, DK, DV), lambda i: (i, 0, 0, 0))],
        out_shape=[jax.ShapeDtypeStruct((T_S, HEADS * DV), f32),
                   jax.ShapeDtypeStruct((T_S, HEADS, DK, DV), f32)],
        compiler_params=_cparams(("arbitrary",)),
        name="retention_sample",
    )(z_all, z_all, z_all, z_all, jnp.asarray(cos8), jnp.asarray(sin8), jnp.asarray(spread_m, dtype=bf16),
      gn_w.reshape(1, HEADS * DV), state)


def _layer_norm(x, w, b):
    mu = jnp.mean(x, axis=-1, keepdims=True)
    var = jnp.mean(jnp.square(x - mu), axis=-1, keepdims=True)
    return (x - mu) * lax.rsqrt(var + EPS) * w + b


def _gmlp_chunk(u_ref, gv_ref, ws_ref, bst_ref, lnw_ref, lnb_ref, o_ref, slot):
    vn = _layer_norm(jax.nn.gelu(gv_ref[...]), lnw_ref[...], lnb_ref[...]).astype(bf16)
    r = lax.broadcasted_iota(i32, (CHUNK, CHUNK), 0)
    c = lax.broadcasted_iota(i32, (CHUNK, CHUNK), 1)
    causal = r >= c
    bst = bst_ref[...]
    for g in range(GROUPS):
        w = jnp.where(causal, ws_ref[g], 0.0).astype(bf16)
        mixed = jnp.dot(w, vn[:, g * GC:(g + 1) * GC], preferred_element_type=f32) + bst[:, g:g + 1]
        o_ref[slot, :, g * GC:(g + 1) * GC] = (
            jax.nn.gelu(u_ref[:, g * GC:(g + 1) * GC]) * mixed).astype(o_ref.dtype)


def _gmlp_sample_kernel(u_ref, gv_ref, w0_ref, b0_ref, lnw_ref, lnb_ref, vn_ref, o_ref):
    vn = _layer_norm(jax.nn.gelu(gv_ref[...]), lnw_ref[...], lnb_ref[...])
    vn_ref[...] = vn
    o_ref[...] = (jax.nn.gelu(u_ref[...]) * (vn * w0_ref[...] + b0_ref[...])).astype(o_ref.dtype)


def _gmlp_sample(z_all, gm_ws, gm_bs, ln_w, ln_b):
    w0 = jnp.repeat(gm_ws[:, 0, 0], GC).reshape(1, GW)
    b0 = jnp.repeat(gm_bs[:, 0], GC).reshape(1, GW)
    blk = 0
    return pl.pallas_call(
        _gmlp_sample_kernel,
        grid=(1,),
        in_specs=[pl.BlockSpec((T_S, GW), lambda i: (blk, COL_U)),
                  pl.BlockSpec((T_S, GW), lambda i: (blk, COL_GV)),
                  pl.BlockSpec((1, GW), lambda i: (0, 0)),
                  pl.BlockSpec((1, GW), lambda i: (0, 0)),
                  pl.BlockSpec((1, GW), lambda i: (0, 0)),
                  pl.BlockSpec((1, GW), lambda i: (0, 0))],
        out_specs=[pl.BlockSpec((T_S, GW), lambda i: (0, 0)),
                   pl.BlockSpec((T_S, GW), lambda i: (0, 0))],
        out_shape=[jax.ShapeDtypeStruct((T_S, GW), f32),
                   jax.ShapeDtypeStruct((T_S, GW), bf16)],
        compiler_params=_cparams(("arbitrary",)),
        name="gmlp_sample",
    )(z_all, z_all, w0, b0, ln_w.reshape(1, GW), ln_b.reshape(1, GW))


def _merge_kernel(a_ref, b_ref, ga_ref, gb_ref, woa_ref, wob_ref, o_ref):
    ya = jnp.dot(a_ref[...].astype(bf16), woa_ref[...], preferred_element_type=f32)
    yb = jnp.dot(b_ref[...].astype(bf16), wob_ref[...], preferred_element_type=f32)
    o_ref[...] = (jax.nn.sigmoid(ga_ref[...]) * ya + jax.nn.sigmoid(gb_ref[...]) * yb).astype(o_ref.dtype)


def _merge(a, bm, z_all, w_oa, w_ob, tm, z_blk0):
    m = a.shape[0]
    resident = lambda: pl.BlockSpec((D, D), lambda i: (0, 0), pipeline_mode=pl.Buffered(1))
    return pl.pallas_call(
        _merge_kernel,
        grid=(m // tm,),
        in_specs=[pl.BlockSpec((tm, D), lambda i: (i, 0)),
                  pl.BlockSpec((tm, D), lambda i: (i, 0)),
                  pl.BlockSpec((tm, D), lambda i: (z_blk0 + i, COL_GA)),
                  pl.BlockSpec((tm, D), lambda i: (z_blk0 + i, COL_GB)),
                  resident(), resident()],
        out_specs=pl.BlockSpec((tm, D), lambda i: (i, 0)),
        out_shape=jax.ShapeDtypeStruct((m, D), bf16),
        compiler_params=_cparams(("arbitrary",)),
        name="merge",
    )(a, bm, z_all, z_all, w_oa, w_ob)


MIX_NCH = SEQ // CHUNK
MIX_NSTEP = BATCH * MIX_NCH


def _mixer_prompt_kernel(s_dec, q_ref, k_ref, v_ref, g_ref, u_ref, gv_ref, ga_ref, gb_ref,
                         cos_ref, sin_ref, intra_ref, qd_ref, kd_ref, gnw_ref,
                         ws_ref, bst_ref, lnw_ref, lnb_ref, woa_ref, wob_ref,
                         m_ref, s_ref, a_scr, b_scr):
    s = pl.program_id(0)
    live = s < MIX_NSTEP
    c = jnp.minimum(s, MIX_NSTEP - 1)
    cur = s & 1

    @pl.when(s == 0)
    def _():
        a_scr[...] = jnp.zeros_like(a_scr)
        b_scr[...] = jnp.zeros_like(b_scr)

    @pl.when((c & (MIX_NCH - 1)) == 0)
    def _():
        s_ref[...] = jnp.zeros_like(s_ref)

    ya = jnp.dot(a_scr[1 - cur], woa_ref[...], preferred_element_type=f32)
    yb = jnp.dot(b_scr[1 - cur], wob_ref[...], preferred_element_type=f32)
    m_ref[...] = (jax.nn.sigmoid(ga_ref[...]) * ya + jax.nn.sigmoid(gb_ref[...]) * yb).astype(m_ref.dtype)

    _retention_chunk(s_dec, live, q_ref, k_ref, v_ref, g_ref, cos_ref, sin_ref, intra_ref, qd_ref, kd_ref,
                     gnw_ref, s_ref, a_scr, cur)
    _gmlp_chunk(u_ref, gv_ref, ws_ref, bst_ref, lnw_ref, lnb_ref, b_scr, cur)


def _mixer_prompt(z_all, gn_w, gm_ws, gm_bs, ln_w, ln_b, w_oa, w_ob):
    intra, q_dec, k_dec, s_dec, _ = _retention_consts()
    cos_t, sin_t = _rope_tables(np.arange(SEQ))
    cur = lambda s: jnp.minimum(s, MIX_NSTEP - 1)
    lag = lambda s: jnp.maximum(s - 1, 0)
    zspec = lambda width, row, col: pl.BlockSpec((CHUNK, width), lambda s: (row(s), col))
    const = lambda shape: pl.BlockSpec(shape, lambda s: (0,) * len(shape))
    resident = lambda: pl.BlockSpec((D, D), lambda s: (0, 0), pipeline_mode=pl.Buffered(1))
    return pl.pallas_call(
        functools.partial(_mixer_prompt_kernel, s_dec),
        grid=(MIX_NSTEP + 1,),
        in_specs=[zspec(HEADS * DK, cur, COL_Q), zspec(HEADS * DK, cur, COL_K),
                  zspec(HEADS * DV, cur, COL_V), zspec(HEADS * DV, cur, COL_G),
                  zspec(GW, cur, COL_U), zspec(GW, cur, COL_GV),
                  zspec(D, lag, COL_GA), zspec(D, lag, COL_GB),
                  pl.BlockSpec((CHUNK, DK), lambda s: (cur(s) & (MIX_NCH - 1), 0)),
                  pl.BlockSpec((CHUNK, DK), lambda s: (cur(s) & (MIX_NCH - 1), 0)),
                  const((HEADS, CHUNK, CHUNK)), const((CHUNK, HEADS)), const((CHUNK, HEADS)),
                  const((1, HEADS * DV)),
                  const((GROUPS, CHUNK, CHUNK)), const((CHUNK, GROUPS)), const((1, GW)), const((1, GW)),
                  resident(), resident()],
        out_specs=[pl.BlockSpec((CHUNK, D), lambda s: (lag(s), 0)),
                   pl.BlockSpec((None, HEADS, DK, DV), lambda s: (cur(s) // MIX_NCH, 0, 0, 0))],
        out_shape=[jax.ShapeDtypeStruct((T_P, D), bf16),
                   jax.ShapeDtypeStruct((BATCH, HEADS, DK, DV), f32)],
        scratch_shapes=[pltpu.VMEM((2, CHUNK, HEADS * DV), bf16), pltpu.VMEM((2, CHUNK, GW), bf16)],
        compiler_params=_cparams(("arbitrary",)),
        name="mixer_prompt",
    )(z_all, z_all, z_all, z_all, z_all, z_all, z_all, z_all,
      jnp.asarray(cos_t), jnp.asarray(sin_t), jnp.asarray(intra), jnp.asarray(q_dec), jnp.asarray(k_dec),
      gn_w.reshape(1, HEADS * DV), gm_ws, gm_bs.T, ln_w.reshape(1, GW), ln_b.reshape(1, GW), w_oa, w_ob)


def _post_kernel(m_ref, x_ref, g1_ref, sc_ref, sh_ref, nw_ref, wout_ref, wr_ref, br_ref,
                 x1_ref, h2_ref, idx_ref, gate_ref, rank_ref, cnt_ref):
    i = pl.program_id(0)

    @pl.when(i == 0)
    def _():
        cnt_ref[...] = jnp.zeros_like(cnt_ref)

    tm = m_ref.shape[0]
    y = jnp.dot(m_ref[...], wout_ref[...], preferred_element_type=f32)
    x1 = x_ref[...] + g1_ref[...] * y
    x1_ref[...] = x1
    xn = x1 * lax.rsqrt(jnp.mean(x1 * x1, axis=-1, keepdims=True) + EPS) * nw_ref[...]
    h2 = xn * (1.0 + sc_ref[...]) + sh_ref[...]
    h2_ref[...] = h2

    h_hi = h2.astype(bf16)
    h_lo = (h2 - h_hi.astype(f32)).astype(bf16)
    p_hi = jnp.dot(h_hi, wr_ref[...], preferred_element_type=f32)
    p_lo = jnp.dot(h_lo, wr_ref[...], preferred_element_type=f32)
    logits = p_hi[:, :LANES] + p_hi[:, LANES:] + p_lo[:, :LANES] + br_ref[...]

    lane = lax.broadcasted_iota(i32, (tm, LANES), 1).astype(f32)
    col = lax.broadcasted_iota(i32, (tm, TOP_K), 1)
    work = jnp.where(lane < N_EXPERTS, logits, -jnp.inf)
    member = jnp.zeros((tm, LANES), f32)
    vals, sels = [], []
    idx_out = jnp.zeros((tm, TOP_K), i32)
    for k in range(TOP_K):
        mx = jnp.max(work, axis=1, keepdims=True)
        ix = jnp.min(jnp.where(work == mx, lane, float(LANES)), axis=1, keepdims=True)
        sel = lane == ix
        vals.append(mx)
        sels.append(sel)
        idx_out = jnp.where(col == k, ix.astype(i32), idx_out)
        member = jnp.where(sel, 1.0, member)
        work = jnp.where(sel, -jnp.inf, work)
    idx_ref[...] = idx_out

    exps = [jnp.exp(v - vals[0]) for v in vals]
    den = exps[0] + exps[1] + exps[2] + exps[3]
    gate_out = jnp.zeros((tm, TOP_K), f32)
    for k in range(TOP_K):
        gate_out = jnp.where(col == k, exps[k] / den, gate_out)
    gate_ref[...] = gate_out

    r = lax.broadcasted_iota(i32, (tm, tm), 0)
    c = lax.broadcasted_iota(i32, (tm, tm), 1)
    lower = jnp.where(c < r, 1.0, 0.0).astype(bf16)
    before = jnp.dot(lower, member.astype(bf16), preferred_element_type=f32) + cnt_ref[...]
    rank_out = jnp.zeros((tm, TOP_K), i32)
    for k in range(TOP_K):
        rk = jnp.sum(jnp.where(sels[k], before, 0.0), axis=1, keepdims=True)
        rank_out = jnp.where(col == k, rk.astype(i32), rank_out)
    rank_ref[...] = rank_out
    cnt_ref[...] += jnp.sum(member, axis=0, keepdims=True)


def _post(m, x, mod, mod_is_rows, nw, w_out, w_router, b_router, tm, per_b):
    mm = m.shape[0]
    if mod_is_rows:
        mspec = lambda chunk: pl.BlockSpec((tm, D), lambda i: (i, chunk))
    else:
        mspec = lambda chunk: pl.BlockSpec((None, 1, D), lambda i: (i // per_b, 0, chunk))
    in_specs = [pl.BlockSpec((tm, D), lambda i: (i, 0)),
                pl.BlockSpec((tm, D), lambda i: (i, 0)),
                mspec(MOD_G1), mspec(MOD_SC2), mspec(MOD_SH2),
                pl.BlockSpec((1, D), lambda i: (0, 0)),
                pl.BlockSpec((D, D), lambda i: (0, 0), pipeline_mode=pl.Buffered(1)),
                pl.BlockSpec((D, 2 * LANES), lambda i: (0, 0)),
                pl.BlockSpec((1, LANES), lambda i: (0, 0))]
    args = [m, x, mod, mod, mod, nw, w_out, w_router, b_router]
    return pl.pallas_call(
        _post_kernel,
        grid=(mm // tm,),
        in_specs=in_specs,
        out_specs=[pl.BlockSpec((tm, D), lambda i: (i, 0)),
                   pl.BlockSpec((tm, D), lambda i: (i, 0)),
                   pl.BlockSpec((tm, TOP_K), lambda i: (i, 0)),
                   pl.BlockSpec((tm, TOP_K), lambda i: (i, 0)),
                   pl.BlockSpec((tm, TOP_K), lambda i: (i, 0)),
                   pl.BlockSpec((1, LANES), lambda i: (0, 0))],
        out_shape=[jax.ShapeDtypeStruct((mm, D), f32),
                   jax.ShapeDtypeStruct((mm, D), f32),
                   jax.ShapeDtypeStruct((mm, TOP_K), i32),
                   jax.ShapeDtypeStruct((mm, TOP_K), f32),
                   jax.ShapeDtypeStruct((mm, TOP_K), i32),
                   jax.ShapeDtypeStruct((1, LANES), f32)],
        compiler_params=_cparams(("arbitrary",)),
        name="post_mixer",
    )(*args)


DISP_TB = 128


DISP_PT = T_P // DISP_TB


def _dispatch_kernel(zblk_ref, dest_ref, h2p_ref, h2s_ref, xbuf_ref, zero_ref, sem):
    i = pl.program_id(0)

    def row_copy(src_ref, t, d):
        return pltpu.make_async_copy(src_ref.at[pl.ds(t, 1)], xbuf_ref.at[pl.ds(d, 1)], sem)

    def zero_copy(blk):
        return pltpu.make_async_copy(zero_ref, xbuf_ref.at[pl.ds(blk * MOE_BM, MOE_BM)], sem)

    @pl.when(i == 0)
    def _():
        zero_ref[...] = jnp.zeros_like(zero_ref)
        used = zblk_ref[N_EXPERTS]

        def zstart(e, c):
            zero_copy(zblk_ref[e]).start()
            return c

        def tstart(b, c):
            zero_copy(b).start()
            return c

        def zwait(e, c):
            zero_copy(0).wait()
            return c

        lax.fori_loop(0, N_EXPERTS, zstart, 0)
        lax.fori_loop(used, MOE_NBLK, tstart, 0)
        lax.fori_loop(0, N_EXPERTS, zwait, 0)
        lax.fori_loop(used, MOE_NBLK, zwait, 0)

    def issue(src_ref):
        for t in range(DISP_TB):
            for k in range(TOP_K):
                row_copy(src_ref, t, dest_ref[0, 0, t * TOP_K + k]).start(priority=k % 2)

    @pl.when(i < DISP_PT)
    def _():
        issue(h2p_ref)

    @pl.when(i == DISP_PT)
    def _():
        issue(h2s_ref)

    def wait_group(gi, c):
        for _ in range(SUBLANES * TOP_K):
            row_copy(h2p_ref, 0, 0).wait()
        return c

    lax.fori_loop(0, DISP_TB // SUBLANES, wait_group, 0)


def _dispatch(zero_blk, dest_all, h2_p, h2_s):
    nt = T_ALL // DISP_TB
    return pl.pallas_call(
        _dispatch_kernel,
        grid_spec=pltpu.PrefetchScalarGridSpec(
            num_scalar_prefetch=1,
            grid=(nt,),
            in_specs=[pl.BlockSpec((1, 1, DISP_TB * TOP_K), lambda i, zb: (i, 0, 0), memory_space=pltpu.SMEM),
                      pl.BlockSpec((DISP_TB, D), lambda i, zb: (jnp.minimum(i, DISP_PT - 1), 0)),
                      pl.BlockSpec((DISP_TB, D), lambda i, zb: (0, 0))],
            out_specs=pl.BlockSpec(memory_space=pl.ANY),
            scratch_shapes=[pltpu.VMEM((MOE_BM, D), f32), pltpu.SemaphoreType.DMA(())]),
        out_shape=jax.ShapeDtypeStruct((MOE_NROWS, D), f32),
        compiler_params=_cparams(("arbitrary",)),
        name="moe_dispatch",
    )(zero_blk, dest_all.reshape(nt, 1, DISP_TB * TOP_K), h2_p, h2_s)


def _moe_ffn_kernel(e_ref, sb_ref, nb_ref, xbuf_ref, wg_ref, wu_ref, bg_ref, bu_ref, wd_ref, bd_ref,
                    ybuf_ref, x_ref, acc_ref, wgb_ref, wub_ref, wdb_ref, sem_in, sem_out):
    p = pl.program_id(0)
    j = pl.program_id(1)
    nb = nb_ref[p]
    sb = sb_ref[p]

    def in_copy(r):
        return pltpu.make_async_copy(xbuf_ref.at[pl.ds((sb + r) * MOE_BM, MOE_BM)],
                                     x_ref.at[pl.ds(r * MOE_BM, MOE_BM)], sem_in.at[r])

    def out_copy(r):
        return pltpu.make_async_copy(acc_ref.at[pl.ds(r * MOE_BM, MOE_BM)],
                                     ybuf_ref.at[pl.ds((sb + r) * MOE_BM, MOE_BM)], sem_out)

    def for_blocks(fn):
        def body(r, c):
            fn(r)
            return c
        lax.fori_loop(0, nb, body, 0)

    @pl.when((p == 0) & (j == 0))
    def _():
        used = nb_ref[MOE_PMAX]
        acc_ref[0:MOE_BM, :] = jnp.zeros((MOE_BM, D), f32)

        def tail_copy(b):
            return pltpu.make_async_copy(acc_ref.at[pl.ds(0, MOE_BM)],
                                         ybuf_ref.at[pl.ds(b * MOE_BM, MOE_BM)], sem_out)

        def tstart(b, c):
            tail_copy(b).start()
            return c

        def twait(b, c):
            tail_copy(0).wait()
            return c

        lax.fori_loop(used, MOE_NBLK, tstart, 0)
        lax.fori_loop(used, MOE_NBLK, twait, 0)

    first = j == 0
    last = j == MOE_J - 1

    @pl.when(nb > 0)
    def _():
        @pl.when(first)
        def _():
            for_blocks(lambda r: in_copy(r).start())
            bd = jnp.broadcast_to(bd_ref[...], (MOE_BM, D))

            def init(r):
                acc_ref[pl.ds(pl.multiple_of(r * MOE_BM, MOE_BM), MOE_BM), :] = bd
            for_blocks(init)

        bg = bg_ref[...]
        bu = bu_ref[...]

        def cast_weights():
            wgb_ref[...] = wg_ref[...].astype(bf16)
            wub_ref[...] = wu_ref[...].astype(bf16)
            wdb_ref[...] = wd_ref[...].astype(bf16)

        def ffn_rows(b0, n_blk):
            rows = pl.ds(pl.multiple_of(b0 * MOE_BM, MOE_BM), n_blk * MOE_BM)
            x = x_ref[rows, :].astype(bf16)
            gate = jnp.dot(x, wgb_ref[...], preferred_element_type=f32) + bg
            up = jnp.dot(x, wub_ref[...], preferred_element_type=f32) + bu
            gate = jnp.minimum(gate, SWIGLU_LIMIT)
            up = jnp.clip(up, -SWIGLU_LIMIT, SWIGLU_LIMIT)
            act = (up + 1.0) * (gate * jax.nn.sigmoid(gate * SWIGLU_ALPHA))
            acc_ref[rows, :] += jnp.dot(act.astype(bf16), wdb_ref[...], preferred_element_type=f32)

        def trip(b0, groups, with_cast=False):
            n_blk = sum(groups)

            @pl.when(first)
            def _():
                for b in range(n_blk):
                    in_copy(b0 + b).wait()

            if with_cast:
                cast_weights()
            off = 0
            for g in groups:
                ffn_rows(b0 + off, g)
                off += g

            @pl.when(last)
            def _():
                for b in range(n_blk):
                    out_copy(b0 + b).start()

        full = (3, 3)
        per_trip = sum(full)
        ntrip = sum((nb >= per_trip * m).astype(i32) for m in range(1, MOE_RB // per_trip + 1))
        rem = nb - ntrip * per_trip
        head = jnp.where(rem == 0, per_trip, rem)
        for n_head, groups in ((1, (1,)), (2, (2,)), (3, (3,)), (4, (2, 2)), (5, (3, 2)), (per_trip, full)):
            @pl.when(head == n_head)
            def _(groups=groups):
                trip(0, groups, with_cast=True)

        def full_trip(q, c):
            trip(head + q * per_trip, full)
            return c
        lax.fori_loop(0, ntrip - (rem == 0).astype(i32), full_trip, 0)

        @pl.when(last)
        def _():
            for_blocks(lambda r: out_copy(r).wait())


def _moe_ffn(e_arr, sb_arr, nb_arr, xbuf, w_gate_up, b_gate_up, w_down, b_down):
    def jj(p, j, nb):
        return jnp.where(nb[p] > 0, j, MOE_J - 1)

    in_specs = [
        pl.BlockSpec(memory_space=pl.ANY),
        pl.BlockSpec((None, D, MOE_TF), lambda p, j, e, sb, nb: (e[p], 0, jj(p, j, nb))),
        pl.BlockSpec((None, D, MOE_TF), lambda p, j, e, sb, nb: (e[p], 0, MOE_J + jj(p, j, nb))),
        pl.BlockSpec((None, 1, MOE_TF), lambda p, j, e, sb, nb: (e[p], 0, jj(p, j, nb))),
        pl.BlockSpec((None, 1, MOE_TF), lambda p, j, e, sb, nb: (e[p], 0, MOE_J + jj(p, j, nb))),
        pl.BlockSpec((None, MOE_TF, D), lambda p, j, e, sb, nb: (e[p], jj(p, j, nb), 0)),
        pl.BlockSpec((None, 1, D), lambda p, j, e, sb, nb: (e[p], 0, 0)),
    ]
    return pl.pallas_call(
        _moe_ffn_kernel,
        grid_spec=pltpu.PrefetchScalarGridSpec(
            num_scalar_prefetch=3,
            grid=(MOE_PMAX, MOE_J),
            in_specs=in_specs,
            out_specs=pl.BlockSpec(memory_space=pl.ANY),
            scratch_shapes=[pltpu.VMEM((MOE_RB * MOE_BM, D), f32),
                            pltpu.VMEM((MOE_RB * MOE_BM, D), f32),
                            pltpu.VMEM((D, MOE_TF), bf16),
                            pltpu.VMEM((D, MOE_TF), bf16),
                            pltpu.VMEM((MOE_TF, D), bf16),
                            pltpu.SemaphoreType.DMA((MOE_RB,)),
                            pltpu.SemaphoreType.DMA(())]),
        out_shape=jax.ShapeDtypeStruct((MOE_NROWS, D), f32),
        compiler_params=_cparams(("arbitrary", "arbitrary"), vmem=56 * 1024 * 1024),
        name="moe_ffn",
    )(e_arr, sb_arr, nb_arr, xbuf, w_gate_up, w_gate_up,
      b_gate_up.reshape(N_EXPERTS, 1, 2 * D_EXPERT), b_gate_up.reshape(N_EXPERTS, 1, 2 * D_EXPERT),
      w_down, b_down.reshape(N_EXPERTS, 1, D))


COMB_TB = 128


def _combine_kernel(dcur_ref, dnext_ref, ybuf_ref, gate_ref, x1_ref, g2_ref, fw_ref, o_ref, rows_ref, sem):
    i = pl.program_id(0)
    slot = i & 1

    def row_copy(buf, n, d):
        return pltpu.make_async_copy(ybuf_ref.at[pl.ds(d, 1)], rows_ref.at[buf, pl.ds(n, 1)], sem.at[buf])

    def wait_tile(buf):
        def wait_group(gi, c):
            for _ in range(SUBLANES * TOP_K):
                row_copy(buf, 0, 0).wait()
            return c
        lax.fori_loop(0, COMB_TB // SUBLANES, wait_group, 0)

    @pl.when(i == 0)
    def _():
        def start_group(gi, c):
            for tt in range(SUBLANES):
                t = pl.multiple_of(gi * SUBLANES, SUBLANES) + tt
                for k in range(TOP_K):
                    d = dcur_ref[0, 0, (gi * SUBLANES + tt) * TOP_K + k]
                    row_copy(0, k * COMB_TB + t, d).start(priority=k % 2)
            return c
        lax.fori_loop(0, COMB_TB // SUBLANES, start_group, 0)

    wait_tile(slot)

    def step(buf):
        for t in range(COMB_TB):
            for k in range(TOP_K):
                row_copy(1 - buf, k * COMB_TB + t, dnext_ref[0, 0, t * TOP_K + k]).start(priority=k % 2)

        gates = gate_ref[...]
        f = gates[:, 0:1] * rows_ref[buf, 0:COMB_TB, :]
        for k in range(1, TOP_K):
            f = f + gates[:, k:k + 1] * rows_ref[buf, k * COMB_TB:(k + 1) * COMB_TB, :]
        x2 = x1_ref[...] + g2_ref[...] * f
        o_ref[...] = x2 * lax.rsqrt(jnp.mean(x2 * x2, axis=-1, keepdims=True) + EPS) * fw_ref[...]

    for buf in range(2):
        @pl.when(slot == buf)
        def _(buf=buf):
            step(buf)

    @pl.when(i == pl.num_programs(0) - 1)
    def _():
        wait_tile(1 - slot)


def _combine(dest, ybuf, gates, x1, mod, mod_is_rows, per_b, final_w):
    mm = x1.shape[0]
    nt = mm // COMB_TB
    dest3 = dest.reshape(nt, 1, COMB_TB * TOP_K)
    if mod_is_rows:
        g2spec = pl.BlockSpec((COMB_TB, D), lambda i: (i, MOD_G2))
    else:
        g2spec = pl.BlockSpec((None, 1, D), lambda i: (i // per_b, 0, MOD_G2))
    return pl.pallas_call(
        _combine_kernel,
        grid=(nt,),
        in_specs=[pl.BlockSpec((1, 1, COMB_TB * TOP_K), lambda i: (i, 0, 0), memory_space=pltpu.SMEM),
                  pl.BlockSpec((1, 1, COMB_TB * TOP_K), lambda i: (jnp.minimum(i + 1, nt - 1), 0, 0),
                               memory_space=pltpu.SMEM),
                  pl.BlockSpec(memory_space=pl.ANY),
                  pl.BlockSpec((COMB_TB, TOP_K), lambda i: (i, 0)),
                  pl.BlockSpec((COMB_TB, D), lambda i: (i, 0)),
                  g2spec,
                  pl.BlockSpec((1, D), lambda i: (0, 0))],
        out_specs=pl.BlockSpec((COMB_TB, D), lambda i: (i, 0)),
        out_shape=jax.ShapeDtypeStruct((mm, D), f32),
        scratch_shapes=[pltpu.VMEM((2, TOP_K * COMB_TB, D), f32), pltpu.SemaphoreType.DMA((2,))],
        compiler_params=_cparams(("arbitrary",)),
        name="moe_combine",
    )(dest3, dest3, ybuf, gates, x1, mod, final_w.reshape(1, D))


def _routing_tables(cnt_p, cnt_s, idx_p, rank_p, idx_s, rank_s):
    cp = cnt_p[0, :N_EXPERTS].astype(i32)
    cs = cnt_s[0, :N_EXPERTS].astype(i32)
    nblk = (cp + cs + MOE_BM - 1) // MOE_BM
    blk_end = jnp.cumsum(nblk)
    blk_start = blk_end - nblk
    row_start = blk_start * MOE_BM
    experts = jnp.arange(N_EXPERTS, dtype=i32)

    def seg_row(idx, rank, base):
        flat = idx.reshape(-1)
        return jnp.sum(jnp.where(flat[:, None] == experts[None, :], base[None, :], 0), axis=1) + rank.reshape(-1)

    dest_p = seg_row(idx_p, rank_p, row_start)
    dest_s = seg_row(idx_s, rank_s, row_start + cp)
    used = blk_end[-1:]
    zero_blk = jnp.concatenate([jnp.maximum(blk_end - 1, 0), used]).astype(i32)
    npass = (nblk + MOE_RB - 1) // MOE_RB
    pass_end = jnp.cumsum(npass)
    total = pass_end[-1]
    pid = jnp.arange(MOE_PMAX, dtype=i32)
    pid_c = jnp.minimum(pid, total - 1)
    pe = jnp.minimum(jnp.sum((pass_end[None, :] <= pid_c[:, None]).astype(i32), axis=1), N_EXPERTS - 1)
    local = pid_c - (pass_end - npass)[pe]
    active = pid < total
    sb = jnp.where(active, blk_start[pe] + local * MOE_RB, 0).astype(i32)
    nb = jnp.where(active, jnp.clip(nblk[pe] - local * MOE_RB, 0, MOE_RB), 0)
    nb = jnp.concatenate([nb, used]).astype(i32)
    return dest_p.astype(i32), dest_s.astype(i32), zero_blk, pe, sb, nb


def kernel(x_prompt, x_sample, c_prompt, c_sample, state_ret, norm1_w, norm2_w, w_ada, b_ada, w_in,
           ret_gn_w, gm_ln_w, gm_ln_b, gm_ws, gm_bs, w_oa, w_ob, w_out, w_router, b_router,
           w_gate_up, b_gate_up, w_down, b_down, final_norm_w):
    x_p = x_prompt.reshape(T_P, D)
    x_s = x_sample.reshape(T_S, D)
    per_b = lambda tm: SEQ // tm

    mod = _ada(jnp.concatenate([c_prompt, c_sample], axis=0), w_ada[0], b_ada[0])
    mod_p = mod[:BATCH].reshape(BATCH, 1, 6 * D)
    mod_s = mod[BATCH:]

    n1 = norm1_w[0].reshape(1, D)
    h_p = _norm1(x_p, n1, mod_p, False, 512, per_b(512))
    h_s = _norm1(x_s, n1, mod_s, True, T_S, 1)
    z_p, z_s = _proj_in(h_p, h_s, w_in[0])

    woa, wob, wout = w_oa[0].astype(bf16), w_ob[0].astype(bf16), w_out[0].astype(bf16)
    m_p, s_p = _mixer_prompt(z_p, ret_gn_w[0], gm_ws[0], gm_bs[0], gm_ln_w[0], gm_ln_b[0], woa, wob)
    a_s, s_s = _ret_sample(z_s, ret_gn_w[0], state_ret[0])
    vn_s, bm_s = _gmlp_sample(z_s, gm_ws[0], gm_bs[0], gm_ln_w[0], gm_ln_b[0])
    m_s = _merge(a_s, bm_s, z_s, woa, wob, T_S, 0)

    n2 = norm2_w[0].reshape(1, D)
    wr_hi = w_router[0].astype(bf16)
    wr_lo = (w_router[0] - wr_hi.astype(f32)).astype(bf16)
    lane_pad = ((0, 0), (0, LANES - N_EXPERTS))
    wr_cat = jnp.concatenate([jnp.pad(wr_hi, lane_pad), jnp.pad(wr_lo, lane_pad)], axis=1)
    br = jnp.pad(b_router[0].reshape(1, N_EXPERTS), lane_pad)
    x1_p, h2_p, idx_p, gate_p, rank_p, cnt_p = _post(
        m_p, x_p, mod_p, False, n2, wout, wr_cat, br, 512, per_b(512))
    x1_s, h2_s, idx_s, gate_s, rank_s, cnt_s = _post(
        m_s, x_s, mod_s, True, n2, wout, wr_cat, br, T_S, 1)

    dest_p, dest_s, zero_blk, pe, sb, nb = _routing_tables(cnt_p, cnt_s, idx_p, rank_p, idx_s, rank_s)
    xbuf = _dispatch(zero_blk, jnp.concatenate([dest_p, dest_s], axis=0), h2_p, h2_s)
    ybuf = _moe_ffn(pe, sb, nb, xbuf, w_gate_up[0], b_gate_up[0], w_down[0], b_down[0])
    y_p = _combine(dest_p, ybuf, gate_p, x1_p, mod_p, False, per_b(COMB_TB), final_norm_w)
    y_s = _combine(dest_s, ybuf, gate_s, x1_s, mod_s, True, 1, final_norm_w)

    return (y_p.reshape(BATCH, SEQ, D),
            y_s.reshape(DEC_BATCH, 1, D),
            s_p.reshape(1, BATCH, HEADS, DK, DV),
            s_s.reshape(1, DEC_BATCH, HEADS, DK, DV),
            vn_s.reshape(1, DEC_BATCH, 1, GW))
```
